```python
import math
import jax, jax.numpy as jnp
from jax import lax
import numpy as np

D_MODEL = 1024
BATCH = 2
SEQ = 16384
DEPTH = 4
DEC_BATCH = 32
DEC_SEQ = 32
PAST_LEN = 1024

CHUNK = 64
N_META = 16
N_A_LAYERS = DEPTH // 2
N_B_LAYERS = DEPTH - N_A_LAYERS
SSM_GROUP = 16
N_GROUPS = D_MODEL // SSM_GROUP
SSM_STATE = 64
HEAD_DIM = 64
N_HEADS = D_MODEL // HEAD_DIM
N_KV_HEADS = 4
Q_PER_KV = N_HEADS // N_KV_HEADS
WINDOW = 128
LEFT_CHUNKS = WINDOW // CHUNK
BAND = WINDOW + CHUNK
N_BUCKETS = 32
MAX_DISTANCE = 128
D_FF = ((8 * D_MODEL // 3 + 255) // 256) * 256
EPS = 1e-6
NEG = -1e30

kernel_name = "yoco_s5_swa_sink_stream_step"

F32 = jnp.float32


def rmsnorm(x, g):
    xf = x.astype(F32)
    y = xf * lax.rsqrt(jnp.mean(xf * xf, axis=-1, keepdims=True) + EPS)
    return (y * g.astype(F32)).astype(x.dtype)


def swiglu(x, w_in, w_out):
    gate, up = jnp.split(x @ w_in, 2, axis=-1)
    return (jax.nn.silu(gate) * up) @ w_out


def s5_discretize(a_re, a_im, log_dt, b_re, b_im):
    a_re = a_re.astype(F32)
    a_im = a_im.astype(F32)
    dt = jnp.exp(log_dt.astype(F32))[:, None]
    mag = jnp.exp(a_re * dt)
    ang = a_im * dt
    abar_re = mag * jnp.cos(ang)
    abar_im = mag * jnp.sin(ang)
    num_re = abar_re - 1.0
    num_im = abar_im
    inv = 1.0 / (a_re * a_re + a_im * a_im)
    f_re = (num_re * a_re + num_im * a_im) * inv
    f_im = (num_im * a_re - num_re * a_im) * inv
    b_re = b_re.astype(F32)
    b_im = b_im.astype(F32)
    bbar_re = f_re[..., None] * b_re - f_im[..., None] * b_im
    bbar_im = f_re[..., None] * b_im + f_im[..., None] * b_re
    return abar_re, abar_im, bbar_re, bbar_im


def _linrec_combine(e1, e2):
    a1r, a1i, b1r, b1i = e1
    a2r, a2i, b2r, b2i = e2
    return (a2r * a1r - a2i * a1i, a2r * a1i + a2i * a1r,
            a2r * b1r - a2i * b1i + b2r, a2r * b1i + a2i * b1r + b2i)


def s5_block(u, h_re, h_im, disc, c_re, c_im, d_skip):
    abar_re, abar_im, bbar_re, bbar_im = disc
    bu_re = jnp.einsum('blgc,gpc->blgp', u, bbar_re)
    bu_im = jnp.einsum('blgc,gpc->blgp', u, bbar_im)
    first_re = abar_re * h_re - abar_im * h_im + bu_re[:, 0]
    first_im = abar_re * h_im + abar_im * h_re + bu_im[:, 0]
    bu_re = bu_re.at[:, 0].set(first_re)
    bu_im = bu_im.at[:, 0].set(first_im)
    a_re = jnp.broadcast_to(abar_re, bu_re.shape)
    a_im = jnp.broadcast_to(abar_im, bu_im.shape)
    _, _, hs_re, hs_im = lax.associative_scan(_linrec_combine, (a_re, a_im, bu_re, bu_im), axis=1)
    y = (jnp.einsum('blgp,gcp->blgc', hs_re, c_re)
         - jnp.einsum('blgp,gcp->blgc', hs_im, c_im) + d_skip * u)
    return y, hs_re[:, -1], hs_im[:, -1]


def s5_scan(u, h_re, h_im, disc, c_re, c_im, d_skip, block):
    bsz, length = u.shape[:2]
    nb = length // block
    ub = jnp.swapaxes(u.reshape(bsz, nb, block, N_GROUPS, SSM_GROUP), 0, 1)

    def step(carry, u_blk):
        y, hr, hi = s5_block(u_blk, carry[0], carry[1], disc, c_re, c_im, d_skip)
        return (hr, hi), y

    (h_re, h_im), ys = lax.scan(step, (h_re, h_im), ub)
    return jnp.swapaxes(ys, 0, 1).reshape(bsz, length, D_MODEL), h_re, h_im


def a_layer(x, h_re, h_im, l, block, ssm, norm_mix, norm_ffn, w_ffn_in, w_ffn_out):
    a_re, a_im, log_dt, b_re, b_im, c_re, c_im, d_skip, w_glu = ssm
    disc = s5_discretize(a_re[l], a_im[l], log_dt[l], b_re[l], b_im[l])
    bsz, length, _ = x.shape
    u = rmsnorm(x, norm_mix[l]).astype(F32).reshape(bsz, length, N_GROUPS, SSM_GROUP)
    y, h_re, h_im = s5_scan(u, h_re.astype(F32), h_im.astype(F32), disc,
                            c_re[l].astype(F32), c_im[l].astype(F32), d_skip[l].astype(F32), block)
    val, gate = jnp.split(jax.nn.gelu(y).astype(x.dtype) @ w_glu[l], 2, axis=-1)
    x = x + val * jax.nn.sigmoid(gate)
    x = x + swiglu(rmsnorm(x, norm_ffn[l]), w_ffn_in[l], w_ffn_out[l])
    return x, h_re, h_im


def shared_kv(x, norm_kv, w_kv, k_norm):
    k, v = jnp.split(rmsnorm(x, norm_kv) @ w_kv, 2, axis=-1)
    k = k.reshape(*x.shape[:-1], N_KV_HEADS, HEAD_DIM)
    v = v.reshape(*x.shape[:-1], N_KV_HEADS, HEAD_DIM)
    return rmsnorm(k, k_norm), v


def rel_bucket(rel):
    half = N_BUCKETS // 2
    max_exact = half // 2
    n = jnp.abs(rel)
    nf = jnp.maximum(n, 1).astype(F32)
    large = max_exact + (jnp.log(nf / max_exact) / math.log(MAX_DISTANCE / max_exact)
                         * (half - max_exact)).astype(jnp.int32)
    large = jnp.minimum(large, half - 1)
    return jnp.where(rel > 0, half, 0) + jnp.where(n < max_exact, n, large)


def rel_bias_lookup(rel, rel_bias):
    b = rel_bias.astype(F32)[rel_bucket(rel)]
    b = b.reshape(*rel.shape, N_KV_HEADS, Q_PER_KV)
    return jnp.moveaxis(b, 1, -1)


def queries(xn, w_q_l, q_norm_l):
    q = (xn @ w_q_l).reshape(*xn.shape[:-1], N_KV_HEADS, Q_PER_KV, HEAD_DIM)
    return rmsnorm(q, q_norm_l).astype(F32) * (HEAD_DIM ** -0.5)


def sink_attention(q, k_meta, v_meta, k, v, bias_meta, bias_band, band_mask, sinks):
    lm = jnp.einsum('...qkgd,mkd->...qkgm', q, k_meta.astype(F32)) + bias_meta
    lb = jnp.einsum('...qkgd,...skd->...qkgs', q, k.astype(F32)) + bias_band
    lb = jnp.where(band_mask, lb, NEG)
    s = sinks.astype(F32).reshape(N_KV_HEADS, Q_PER_KV, 1)
    m = jnp.maximum(jnp.maximum(lm.max(-1, keepdims=True), lb.max(-1, keepdims=True)), s)
    pm = jnp.exp(lm - m)
    pb = jnp.exp(lb - m)
    den = pm.sum(-1, keepdims=True) + pb.sum(-1, keepdims=True) + jnp.exp(s - m)
    out = (jnp.einsum('...qkgm,mkd->...qkgd', pm, v_meta.astype(F32))
           + jnp.einsum('...qkgs,...skd->...qkgd', pb, v.astype(F32)))
    return out / den


def window_attention_prompt(xn, w_q_l, q_norm_l, sinks_l, w_o_l, k_fr, v_fr, k_meta, v_meta, rel_bias):
    bsz, seq, _ = xn.shape
    nc = seq // CHUNK
    q = queries(xn, w_q_l, q_norm_l).reshape(bsz, nc, CHUNK, N_KV_HEADS, Q_PER_KV, HEAD_DIM)
    pad = ((0, 0), (WINDOW, 0), (0, 0), (0, 0))
    kp = jnp.pad(k_fr, pad).reshape(bsz, nc + LEFT_CHUNKS, CHUNK, N_KV_HEADS, HEAD_DIM)
    vp = jnp.pad(v_fr, pad).reshape(bsz, nc + LEFT_CHUNKS, CHUNK, N_KV_HEADS, HEAD_DIM)
    k_band = jnp.concatenate([kp[:, i:i + nc] for i in range(LEFT_CHUNKS + 1)], axis=2)
    v_band = jnp.concatenate([vp[:, i:i + nc] for i in range(LEFT_CHUNKS + 1)], axis=2)
    qi = jnp.arange(CHUNK)
    sj = jnp.arange(BAND)
    bias_band = rel_bias_lookup(sj[None, :] - WINDOW - qi[:, None], rel_bias)
    t = jnp.arange(seq)
    mj = jnp.arange(N_META)
    bias_meta = rel_bias_lookup(mj[None, :] - N_META - t[:, None], rel_bias).reshape(
        nc, CHUNK, N_KV_HEADS, Q_PER_KV, N_META)
    key_chunk = jnp.arange(nc)[:, None] - LEFT_CHUNKS + sj[None, :] // CHUNK
    mask = (key_chunk >= 0)[:, None, None, None, :]
    out = sink_attention(q, k_meta, v_meta, k_band, v_band, bias_meta, bias_band, mask, sinks_l)
    return out.reshape(bsz, seq, N_HEADS * HEAD_DIM).astype(xn.dtype) @ w_o_l


def window_attention_sample(xn, w_q_l, q_norm_l, sinks_l, w_o_l, cache_k, cache_v, k_new, v_new,
                            k_meta, v_meta, rel_bias):
    bsz, ds, _ = xn.shape
    q = queries(xn, w_q_l, q_norm_l)
    k = jnp.concatenate([cache_k.astype(k_new.dtype), k_new], axis=1)
    v = jnp.concatenate([cache_v.astype(v_new.dtype), v_new], axis=1)
    qi = jnp.arange(ds)
    sj = jnp.arange(WINDOW + ds)
    bias_band = rel_bias_lookup(sj[None, :] - WINDOW - qi[:, None], rel_bias)
    t = PAST_LEN + qi
    mj = jnp.arange(N_META)
    bias_meta = rel_bias_lookup(mj[None, :] - N_META - t[:, None], rel_bias)
    out = sink_attention(q, k_meta, v_meta, k, v, bias_meta, bias_band, jnp.asarray(True), sinks_l)
    return out.reshape(bsz, ds, N_HEADS * HEAD_DIM).astype(xn.dtype) @ w_o_l


def setup_inputs(seed: int = 0) -> dict:
    key = jax.random.key(seed)
    k = jax.random.split(key, 28)

    def nrm(kk, shape, scale):
        return jax.random.normal(kk, shape, F32) * scale

    def gain(kk, shape):
        return 1.0 + 0.02 * jax.random.normal(kk, shape, F32)

    na, nb, g, p, c = N_A_LAYERS, N_B_LAYERS, N_GROUPS, SSM_STATE, SSM_GROUP
    hd_all = N_HEADS * HEAD_DIM
    kv_all = N_KV_HEADS * HEAD_DIM
    n_idx = jnp.arange(p, dtype=F32)
    return {
        "x_prompt": nrm(k[0], (BATCH, SEQ, D_MODEL), 1.0),
        "x_sample": nrm(k[1], (DEC_BATCH, DEC_SEQ, D_MODEL), 1.0),
        "state_ssm_re": nrm(k[2], (na, DEC_BATCH, g, p), 0.1),
        "state_ssm_im": nrm(k[3], (na, DEC_BATCH, g, p), 0.1),
        "cache_k": nrm(k[4], (DEC_BATCH, WINDOW, N_KV_HEADS, HEAD_DIM), 1.0),
        "cache_v": nrm(k[5], (DEC_BATCH, WINDOW, N_KV_HEADS, HEAD_DIM), 1.0),
        "meta_tokens": nrm(k[6], (N_META, D_MODEL), 1.0),
        "norm_mix": gain(k[7], (DEPTH, D_MODEL)),
        "norm_ffn": gain(k[8], (DEPTH, D_MODEL)),
        "ssm_a_re": -0.5 + nrm(k[9], (na, g, p), 0.01),
        "ssm_a_im": math.pi * n_idx + nrm(k[10], (na, g, p), 0.01),
        "ssm_log_dt": jax.random.uniform(k[11], (na, g), F32, math.log(1e-3), math.log(1e-1)),
        "ssm_b_re": nrm(k[12], (na, g, p, c), (2 * c) ** -0.5),
        "ssm_b_im": nrm(k[13], (na, g, p, c), (2 * c) ** -0.5),
        "ssm_c_re": nrm(k[14], (na, g, c, p), 2.0 * p ** -0.5),
        "ssm_c_im": nrm(k[15], (na, g, c, p), 2.0 * p ** -0.5),
        "ssm_d": nrm(k[16], (na, g, c), 0.5),
        "w_glu": nrm(k[17], (na, D_MODEL, 2 * D_MODEL), D_MODEL ** -0.5),
        "w_ffn_in": nrm(k[18], (DEPTH, D_MODEL, 2 * D_FF), D_MODEL ** -0.5),
        "w_ffn_out": nrm(k[19], (DEPTH, D_FF, D_MODEL), D_FF ** -0.5),
        "norm_kv": gain(k[20], (D_MODEL,)),
        "w_kv": nrm(k[21], (D_MODEL, 2 * kv_all), D_MODEL ** -0.5),
        "k_norm": gain(k[22], (HEAD_DIM,)),
        "w_q": nrm(k[23], (nb, D_MODEL, hd_all), D_MODEL ** -0.5),
        "q_norm": gain(k[24], (nb, HEAD_DIM)),
        "attn_sinks": nrm(k[25], (nb, N_HEADS), 1.0),
        "w_o": nrm(k[26], (nb, hd_all, D_MODEL), hd_all ** -0.5),
        "rel_bias": nrm(k[27], (N_BUCKETS, N_HEADS), 0.5),
    }


def reference(x_prompt, x_sample, state_ssm_re, state_ssm_im, cache_k, cache_v, meta_tokens,
              norm_mix, norm_ffn, ssm_a_re, ssm_a_im, ssm_log_dt, ssm_b_re, ssm_b_im, ssm_c_re, ssm_c_im,
              ssm_d, w_glu, w_ffn_in, w_ffn_out, norm_kv, w_kv, k_norm, w_q, q_norm, attn_sinks, w_o,
              rel_bias):
    ssm = (ssm_a_re, ssm_a_im, ssm_log_dt, ssm_b_re, ssm_b_im, ssm_c_re, ssm_c_im, ssm_d, w_glu)

    meta = meta_tokens[None].astype(x_prompt.dtype)
    zero = jnp.zeros((1, N_GROUPS, SSM_STATE), F32)
    meta_h = []
    for l in range(N_A_LAYERS):
        meta, hr, hi = a_layer(meta, zero, zero, l, N_META, ssm, norm_mix, norm_ffn, w_ffn_in, w_ffn_out)
        meta_h.append((hr, hi))
    k_meta, v_meta = shared_kv(meta[0], norm_kv, w_kv, k_norm)

    x = x_prompt
    bsz = x.shape[0]
    p_re, p_im = [], []
    k_p = v_p = None
    for l in range(DEPTH):
        if l < N_A_LAYERS:
            h0r = jnp.broadcast_to(meta_h[l][0], (bsz, N_GROUPS, SSM_STATE))
            h0i = jnp.broadcast_to(meta_h[l][1], (bsz, N_GROUPS, SSM_STATE))
            x, hr, hi = a_layer(x, h0r, h0i, l, CHUNK, ssm, norm_mix, norm_ffn, w_ffn_in, w_ffn_out)
            p_re.append(hr)
            p_im.append(hi)
            if l == N_A_LAYERS - 1:
                k_p, v_p = shared_kv(x, norm_kv, w_kv, k_norm)
        else:
            j = l - N_A_LAYERS
            x = x + window_attention_prompt(rmsnorm(x, norm_mix[l]), w_q[j], q_norm[j], attn_sinks[j], w_o[j],
                                            k_p, v_p, k_meta, v_meta, rel_bias)
            x = x + swiglu(rmsnorm(x, norm_ffn[l]), w_ffn_in[l], w_ffn_out[l])
    y_prompt = x

    x = x_sample
    ds = x.shape[1]
    s_re, s_im = [], []
    k_s = v_s = None
    for l in range(DEPTH):
        if l < N_A_LAYERS:
            x, hr, hi = a_layer(x, state_ssm_re[l], state_ssm_im[l], l, ds, ssm, norm_mix, norm_ffn,
                                w_ffn_in, w_ffn_out)
            s_re.append(hr)
            s_im.append(hi)
            if l == N_A_LAYERS - 1:
                k_s, v_s = shared_kv(x, norm_kv, w_kv, k_norm)
        else:
            j = l - N_A_LAYERS
            x = x + window_attention_sample(rmsnorm(x, norm_mix[l]), w_q[j], q_norm[j], attn_sinks[j], w_o[j],
                                            cache_k, cache_v, k_s, v_s, k_meta, v_meta, rel_bias)
            x = x + swiglu(rmsnorm(x, norm_ffn[l]), w_ffn_in[l], w_ffn_out[l])
    y_sample = x

    p_ssm_re = jnp.stack(p_re)
    p_ssm_im = jnp.stack(p_im)
    p_k = k_p[:, -WINDOW:]
    p_v = v_p[:, -WINDOW:]
    s_ssm_re = jnp.stack(s_re)
    s_ssm_im = jnp.stack(s_im)
    return (y_prompt, y_sample, p_ssm_re, p_ssm_im, p_k, p_v, s_ssm_re, s_ssm_im, k_s, v_s)
```

```python
import functools
import math

import jax
import jax.numpy as jnp
from jax import lax
from jax.experimental import pallas as pl
from jax.experimental.pallas import tpu as pltpu

F32 = jnp.float32
BF16 = jnp.bfloat16

CHUNK = 64
N_META = 16
SSM_GROUP = 16
SSM_STATE = 64
HEAD_DIM = 64
N_KV_HEADS = 4
WINDOW = 128
N_BUCKETS = 32
MAX_DISTANCE = 128
PAST_LEN = 1024
EPS = 1e-6
NEG = -1e30

LANES = 128
SUBLANES = 8
VMEM_LIMIT_BYTES = 56 * 1024 * 1024

S5_T = 16
S5_PAIRS_PER_STEP = 4
S5_ROWS = 512
ROW_TILE = 512
FFN_CHUNK = 256
ATTN_CHUNKS_PROMPT = 4
ATTN_SEQS_SAMPLE = 4


def _row_tile(n, target):
    if n <= target:
        return n
    best = None
    for t in range(SUBLANES, target + 1, SUBLANES):
        if n % t == 0:
            best = t
    assert best is not None, n
    return best


def _cparams(*sem):
    return pltpu.CompilerParams(dimension_semantics=sem, vmem_limit_bytes=VMEM_LIMIT_BYTES)


def _resident(shape):
    nd = len(shape)
    return pl.BlockSpec(shape, lambda *_: (0,) * nd, pipeline_mode=pl.Buffered(1))


def _rms(x, g):
    ms = jnp.mean(x * x, axis=-1, keepdims=True)
    return (x * lax.rsqrt(ms + EPS)) * g


def _dot(a, b):
    return jnp.dot(a, b, preferred_element_type=F32)


def _dot_nt(a, b):
    return lax.dot_general(a, b, (((1,), (1,)), ((), ())), preferred_element_type=F32)


def _head_rms(x, e, et, gain):
    ss = _dot((x * x).astype(BF16), e)
    rinv = lax.rsqrt(ss * (1.0 / HEAD_DIM) + EPS)
    hi = rinv.astype(BF16)
    lo = (rinv - hi.astype(F32)).astype(BF16)
    rb = _dot(hi, et) + _dot(lo, et)
    return (x * rb) * gain


def _segment_mats(n_heads):
    idx = jnp.arange(n_heads * HEAD_DIM) // HEAD_DIM
    e = (idx[:, None] == jnp.arange(LANES)[None, :]).astype(BF16)
    return e, e.T


def _rms_body(x_ref, g_ref, o_ref):
    o_ref[...] = _rms(x_ref[...], g_ref[...])


def _rms_call(x, g):
    n, d = x.shape
    tm = _row_tile(n, 1024)
    return pl.pallas_call(
        _rms_body,
        grid=(n // tm,),
        in_specs=[pl.BlockSpec((tm, d), lambda i: (i, 0)), _resident((1, d))],
        out_specs=pl.BlockSpec((tm, d), lambda i: (i, 0)),
        out_shape=jax.ShapeDtypeStruct((n, d), F32),
        compiler_params=_cparams("parallel"),
        name="rms",
    )(x, g.reshape(1, d))


def _s5_body(nseq, ug_ref, m_ref, wor_ref, woi_ref, wir_ref, wii_ref, atr_ref, ati_ref, h0r_ref, h0i_ref,
             yg_ref, hfr_ref, hfi_ref, sr, si, hin_r, hin_i, hr, hi):
    npair, rows, width = ug_ref.shape
    half = width // 2
    sl = 2 * SSM_STATE

    @pl.when(pl.program_id(1) == 0)
    def _():
        hr[...] = h0r_ref[...]
        hi[...] = h0i_ref[...]

    for q in range(npair):
        u = ug_ref[q].astype(BF16)
        sr[:, q * sl:(q + 1) * sl] = _dot(u, wor_ref[q])
        si[:, q * sl:(q + 1) * sl] = _dot(u, woi_ref[q])

    ar = atr_ref[...]
    ai = ati_ref[...]
    if nseq % SUBLANES == 0:
        def step(i, carry):
            cr, ci = carry
            off = pl.multiple_of(i * nseq, SUBLANES)
            hin_r[pl.ds(off, nseq), :] = cr
            hin_i[pl.ds(off, nseq), :] = ci
            tr = sr[pl.ds(off, nseq), :]
            ti = si[pl.ds(off, nseq), :]
            return ar * cr - ai * ci + tr, ar * ci + ai * cr + ti

        cr, ci = lax.fori_loop(0, rows // nseq, step, (hr[...], hi[...]))
    else:
        per = SUBLANES // nseq

        def step(i, carry):
            cr, ci = carry
            off = pl.multiple_of(i * SUBLANES, SUBLANES)
            tr = sr[pl.ds(off, SUBLANES), :]
            ti = si[pl.ds(off, SUBLANES), :]
            outs_r, outs_i = [], []
            for j in range(per):
                outs_r.append(cr)
                outs_i.append(ci)
                sj_r = tr[j * nseq:(j + 1) * nseq]
                sj_i = ti[j * nseq:(j + 1) * nseq]
                cr, ci = ar * cr - ai * ci + sj_r, ar * ci + ai * cr + sj_i
            hin_r[pl.ds(off, SUBLANES), :] = jnp.concatenate(outs_r, axis=0)
            hin_i[pl.ds(off, SUBLANES), :] = jnp.concatenate(outs_i, axis=0)
            return cr, ci

        cr, ci = lax.fori_loop(0, rows // SUBLANES, step, (hr[...], hi[...]))
    hr[...] = cr
    hi[...] = ci
    hfr_ref[...] = cr
    hfi_ref[...] = ci

    for q in range(npair):
        h_re = hin_r[:, q * sl:(q + 1) * sl].astype(BF16)
        h_im = hin_i[:, q * sl:(q + 1) * sl].astype(BF16)
        yst = _dot(h_re, wir_ref[q]) + _dot(h_im, wii_ref[q])
        for gl in range(2):
            u = ug_ref[q, :, gl * half:(gl + 1) * half].astype(BF16)
            yg_ref[q, :, gl * half:(gl + 1) * half] = _dot(u, m_ref[q, gl]) + yst[:, gl * half:(gl + 1) * half]


def _s5_call(ug, tabs, h0r, h0i, nseq):
    m, wor, woi, wir, wii, atr, ati = tabs
    npairs, nrows, width = ug.shape
    pb = S5_PAIRS_PER_STEP
    rows = _row_tile(nrows, S5_ROWS)
    assert rows % nseq == 0 or SUBLANES % nseq == 0
    sl = 2 * SSM_STATE
    lanes = pb * sl
    half = width // 2
    grid = (npairs // pb, nrows // rows)
    state_spec = pl.BlockSpec((nseq, lanes), lambda p, r: (0, p))
    return pl.pallas_call(
        functools.partial(_s5_body, nseq),
        grid=grid,
        in_specs=[
            pl.BlockSpec((pb, rows, width), lambda p, r: (p, r, 0)),
            pl.BlockSpec((pb, 2, half, half), lambda p, r: (p, 0, 0, 0)),
            pl.BlockSpec((pb, width, sl), lambda p, r: (p, 0, 0)),
            pl.BlockSpec((pb, width, sl), lambda p, r: (p, 0, 0)),
            pl.BlockSpec((pb, sl, width), lambda p, r: (p, 0, 0)),
            pl.BlockSpec((pb, sl, width), lambda p, r: (p, 0, 0)),
            pl.BlockSpec((1, lanes), lambda p, r: (0, p)),
            pl.BlockSpec((1, lanes), lambda p, r: (0, p)),
            state_spec,
            state_spec,
        ],
        out_specs=[
            pl.BlockSpec((pb, rows, width), lambda p, r: (p, r, 0)),
            state_spec,
            state_spec,
        ],
        out_shape=[
            jax.ShapeDtypeStruct(ug.shape, F32),
            jax.ShapeDtypeStruct(h0r.shape, F32),
            jax.ShapeDtypeStruct(h0i.shape, F32),
        ],
        scratch_shapes=[pltpu.VMEM((rows, lanes), F32)] * 4 + [pltpu.VMEM((nseq, lanes), F32)] * 2,
        compiler_params=_cparams("parallel", "arbitrary"),
        name="s5",
    )(ug, m, wor, woi, wir, wii, atr, ati, h0r, h0i)


def _s5_tables(a_re, a_im, log_dt, b_re, b_im, c_re, c_im, d_skip):
    t = S5_T
    g, p = a_re.shape
    c = b_re.shape[-1]
    hp = lax.Precision.HIGHEST
    dt = jnp.exp(log_dt)[:, None]
    mag = jnp.exp(a_re * dt)
    ang = a_im * dt
    ar = mag * jnp.cos(ang)
    ai = mag * jnp.sin(ang)
    num_re = ar - 1.0
    num_im = ai
    inv = 1.0 / (a_re * a_re + a_im * a_im)
    f_re = (num_re * a_re + num_im * a_im) * inv
    f_im = (num_im * a_re - num_re * a_im) * inv
    bb_re = f_re[..., None] * b_re - f_im[..., None] * b_im
    bb_im = f_re[..., None] * b_im + f_im[..., None] * b_re
    pr = jnp.ones((1, g, p), F32)
    pi = jnp.zeros((1, g, p), F32)
    sr, si = ar, ai
    while pr.shape[0] < t + 1:
        pr, pi = (jnp.concatenate([pr, pr * sr - pi * si]), jnp.concatenate([pi, pr * si + pi * sr]))
        sr, si = sr * sr - si * si, 2.0 * sr * si
    pr = pr[:t + 1]
    pi = pi[:t + 1]
    ca_r = c_re[None] * pr[:t, :, None, :] - c_im[None] * pi[:t, :, None, :]
    ca_i = c_re[None] * pi[:t, :, None, :] + c_im[None] * pr[:t, :, None, :]
    k = (jnp.einsum('tgdp,gpc->tgdc', ca_r, bb_re, precision=hp)
         - jnp.einsum('tgdp,gpc->tgdc', ca_i, bb_im, precision=hp))
    k = k.at[0].add(d_skip[:, None, :] * jnp.eye(c, dtype=F32)[None])
    kt = jnp.transpose(k, (1, 3, 2, 0))
    v = jnp.concatenate([kt, jnp.zeros_like(kt)], axis=-1)
    toep = jnp.tile(v, (1, 1, 1, t))[..., :t * (2 * t - 1)].reshape(g, c, c, t, 2 * t - 1)[..., :t]
    m = jnp.transpose(toep, (0, 3, 1, 4, 2)).reshape(g // 2, 2, t * c, t * c)
    c1_r = c_re[None] * pr[1:, :, None, :] - c_im[None] * pi[1:, :, None, :]
    c1_i = c_re[None] * pi[1:, :, None, :] + c_im[None] * pr[1:, :, None, :]
    win_r = jnp.transpose(c1_r, (1, 3, 0, 2)).reshape(g, p, t * c)
    win_i = -jnp.transpose(c1_i, (1, 3, 0, 2)).reshape(g, p, t * c)
    qr = pr[:t][::-1]
    qi = pi[:t][::-1]
    wo_r = qr[..., None] * bb_re[None] - qi[..., None] * bb_im[None]
    wo_i = qr[..., None] * bb_im[None] + qi[..., None] * bb_re[None]
    wout_r = jnp.transpose(wo_r, (1, 0, 3, 2)).reshape(g, t * c, p)
    wout_i = jnp.transpose(wo_i, (1, 0, 3, 2)).reshape(g, t * c, p)
    eye2 = jnp.eye(2, dtype=F32)

    def pair_out(w):
        w = w.reshape(g // 2, 2, t * c, p)
        return (w[:, :, :, None, :] * eye2[None, :, None, :, None]).reshape(g // 2, 2 * t * c, 2 * p)

    def pair_in(w):
        w = w.reshape(g // 2, 2, p, t * c)
        return (w[:, :, :, None, :] * eye2[None, :, None, :, None]).reshape(g // 2, 2 * p, 2 * t * c)

    return (m.astype(BF16), pair_out(wout_r).astype(BF16), pair_out(wout_i).astype(BF16),
            pair_in(win_r).astype(BF16), pair_in(win_i).astype(BF16),
            pr[t].reshape(1, g * p), pi[t].reshape(1, g * p))


def _to_groups(u):
    nseq, length, d = u.shape
    t = S5_T
    x = u.reshape(nseq, length // t, t, d // (2 * SSM_GROUP), 2, SSM_GROUP)
    x = jnp.transpose(x, (3, 1, 0, 4, 2, 5))
    return x.reshape(d // (2 * SSM_GROUP), (length // t) * nseq, 2 * t * SSM_GROUP)


def _from_groups(yg, nseq, length):
    t = S5_T
    npairs = yg.shape[0]
    x = yg.reshape(npairs, length // t, nseq, 2, t, SSM_GROUP)
    x = jnp.transpose(x, (2, 1, 4, 0, 3, 5))
    return x.reshape(nseq, length, npairs * 2 * SSM_GROUP)


def _glu_body(x_ref, y_ref, w_ref, o_ref):
    d = x_ref.shape[1]
    a = jax.nn.gelu(y_ref[...]).astype(BF16)
    r = _dot(a, w_ref[...])
    o_ref[...] = x_ref[...] + r[:, :d] * jax.nn.sigmoid(r[:, d:])


def _glu_call(x, y, w):
    n, d = x.shape
    tm = _row_tile(n, ROW_TILE)
    row = pl.BlockSpec((tm, d), lambda i: (i, 0))
    return pl.pallas_call(
        _glu_body,
        grid=(n // tm,),
        in_specs=[row, row, _resident(w.shape)],
        out_specs=row,
        out_shape=jax.ShapeDtypeStruct((n, d), F32),
        compiler_params=_cparams("parallel"),
        name="glu",
    )(x, y, w)


def _ffn_body(x_ref, g_ref, win_ref, wout_ref, o_ref):
    dff = wout_ref.shape[0]
    x = x_ref[...]
    xn = _rms(x, g_ref[...]).astype(BF16)
    acc = x
    for j in range(dff // FFN_CHUNK):
        lo = j * FFN_CHUNK
        gate = _dot(xn, win_ref[:, lo:lo + FFN_CHUNK])
        up = _dot(xn, win_ref[:, dff + lo:dff + lo + FFN_CHUNK])
        a = (jax.nn.silu(gate) * up).astype(BF16)
        acc = acc + _dot(a, wout_ref[lo:lo + FFN_CHUNK, :])
    o_ref[...] = acc


def _ffn_call(x, g, w_in, w_out):
    n, d = x.shape
    tm = _row_tile(n, ROW_TILE)
    row = pl.BlockSpec((tm, d), lambda i: (i, 0))
    return pl.pallas_call(
        _ffn_body,
        grid=(n // tm,),
        in_specs=[row, _resident((1, d)), _resident(w_in.shape), _resident(w_out.shape)],
        out_specs=row,
        out_shape=jax.ShapeDtypeStruct((n, d), F32),
        compiler_params=_cparams("parallel"),
        name="ffn",
    )(x, g.reshape(1, d), w_in, w_out)


def _kv_body(x_ref, g_ref, w_ref, kn_ref, e_ref, et_ref, k_ref, v_ref):
    kvw = k_ref.shape[1]
    xn = _rms(x_ref[...], g_ref[...]).astype(BF16)
    r = _dot(xn, w_ref[...])
    k_ref[...] = _head_rms(r[:, :kvw], e_ref[...], et_ref[...], kn_ref[...])
    v_ref[...] = r[:, kvw:]


def _kv_call(x, g, w, k_norm):
    n, d = x.shape
    kvw = w.shape[1] // 2
    tm = _row_tile(n, ROW_TILE)
    e, et = _segment_mats(kvw // HEAD_DIM)
    out = pl.BlockSpec((tm, kvw), lambda i: (i, 0))
    return pl.pallas_call(
        _kv_body,
        grid=(n // tm,),
        in_specs=[pl.BlockSpec((tm, d), lambda i: (i, 0)), _resident((1, d)), _resident(w.shape),
                  _resident((1, kvw)), _resident(e.shape), _resident(et.shape)],
        out_specs=[out, out],
        out_shape=[jax.ShapeDtypeStruct((n, kvw), F32)] * 2,
        compiler_params=_cparams("parallel"),
        name="kv",
    )(x, g.reshape(1, d), w, jnp.tile(k_norm, kvw // HEAD_DIM).reshape(1, kvw), e, et)


def _attn_body(chunk, shared, x_ref, g_ref, wq_ref, qn_ref, wo_ref, sink_ref, kown_ref, vown_ref, kleft_ref,
               vleft_ref, kmeta_ref, vmeta_ref, bband_ref, bmeta_ref, e_ref, et_ref, o_ref, att_ref):
    rows, d = x_ref.shape
    nchunks = rows // chunk
    n_heads = d // HEAD_DIM
    q_per_kv = n_heads // N_KV_HEADS
    band = WINDOW + chunk

    x = x_ref[...]
    xn = _rms(x, g_ref[...]).astype(BF16)
    q = _dot(xn, wq_ref[...])
    q = _head_rms(q, e_ref[...], et_ref[...], qn_ref[...]) * (HEAD_DIM ** -0.5)
    qb = q.astype(BF16)

    k_meta = kmeta_ref[...].astype(BF16)
    v_meta = vmeta_ref[...].astype(BF16)
    k_own = kown_ref[...].astype(BF16)
    v_own = vown_ref[...].astype(BF16)
    if shared:
        k_all = jnp.concatenate([kleft_ref[...].astype(BF16), k_own], axis=0)
        v_all = jnp.concatenate([vleft_ref[...].astype(BF16), v_own], axis=0)
        first_chunk = pl.program_id(1) * nchunks
        col = lax.broadcasted_iota(jnp.int32, (q_per_kv * chunk, band), 1)

    for j in range(nchunks):
        r0 = j * chunk
        if shared:
            kb = k_all[r0:r0 + band]
            vb = v_all[r0:r0 + band]
            valid = col >= WINDOW - chunk * (first_chunk + j)
            bm = bmeta_ref[j]
        else:
            kb = jnp.concatenate([kleft_ref[j].astype(BF16), k_own[r0:r0 + chunk]], axis=0)
            vb = jnp.concatenate([vleft_ref[j].astype(BF16), v_own[r0:r0 + chunk]], axis=0)
            bm = bmeta_ref[0]
        for kv in range(N_KV_HEADS):
            hs = slice(kv * HEAD_DIM, (kv + 1) * HEAD_DIM)
            qk = jnp.concatenate(
                [qb[r0:r0 + chunk, (kv * q_per_kv + gq) * HEAD_DIM:(kv * q_per_kv + gq + 1) * HEAD_DIM]
                 for gq in range(q_per_kv)], axis=0)
            lb = _dot_nt(qk, kb[:, hs]) + bband_ref[kv]
            if shared:
                lb = jnp.where(valid, lb, NEG)
            lm = _dot_nt(qk, k_meta[:, hs]) + bm[kv]
            s = sink_ref[kv]
            mx = jnp.maximum(jnp.maximum(jnp.max(lb, axis=-1, keepdims=True), jnp.max(lm, axis=-1, keepdims=True)), s)
            pb = jnp.exp(lb - mx)
            pm = jnp.exp(lm - mx)
            den = jnp.sum(pm, axis=-1, keepdims=True) + jnp.sum(pb, axis=-1, keepdims=True) + jnp.exp(s - mx)
            o = (_dot(pm.astype(BF16), v_meta[:, hs]) + _dot(pb.astype(BF16), vb[:, hs])) / den
            for gq in range(q_per_kv):
                h0 = (kv * q_per_kv + gq) * HEAD_DIM
                att_ref[r0:r0 + chunk, h0:h0 + HEAD_DIM] = o[gq * chunk:(gq + 1) * chunk]

    o_ref[...] = x + _dot(att_ref[...].astype(BF16), wo_ref[...])


def _attn_call(x, g, w_q, q_norm, w_o, sinks, k_own, v_own, k_left, v_left, k_meta, v_meta, b_band, b_meta,
               chunk, shared, nseq, tiles_per_seq, chunks_per_tile):
    n, d = x.shape
    kvw = k_own.shape[1]
    n_heads = d // HEAD_DIM
    q_per_kv = n_heads // N_KV_HEADS
    rows = chunks_per_tile * chunk
    band = WINDOW + chunk
    e, et = _segment_mats(n_heads)
    sink_col = jnp.broadcast_to(sinks.reshape(N_KV_HEADS, q_per_kv, 1, 1),
                                (N_KV_HEADS, q_per_kv, chunk, 1)).reshape(N_KV_HEADS, q_per_kv * chunk, 1)
    row_idx = lambda b, i: (b * tiles_per_seq + i, 0)
    if shared:
        halo_blocks = rows // WINDOW
        left_spec = pl.BlockSpec(
            (WINDOW, kvw), lambda b, i: (jnp.maximum(b * tiles_per_seq * halo_blocks + i * halo_blocks - 1, 0), 0))
        bmeta_spec = pl.BlockSpec((chunks_per_tile,) + b_meta.shape[1:], lambda b, i: (i, 0, 0, 0))
    else:
        left_spec = pl.BlockSpec((chunks_per_tile, WINDOW, kvw), lambda b, i: (b * tiles_per_seq + i, 0, 0))
        bmeta_spec = _resident(b_meta.shape)
    return pl.pallas_call(
        functools.partial(_attn_body, chunk, shared),
        grid=(nseq, tiles_per_seq),
        in_specs=[
            pl.BlockSpec((rows, d), row_idx), _resident((1, d)), _resident(w_q.shape), _resident((1, d)),
            _resident(w_o.shape), _resident(sink_col.shape),
            pl.BlockSpec((rows, kvw), row_idx), pl.BlockSpec((rows, kvw), row_idx), left_spec, left_spec,
            _resident(k_meta.shape), _resident(v_meta.shape), _resident(b_band.shape), bmeta_spec,
            _resident(e.shape), _resident(et.shape),
        ],
        out_specs=pl.BlockSpec((rows, d), row_idx),
        out_shape=jax.ShapeDtypeStruct((n, d), F32),
        scratch_shapes=[pltpu.VMEM((rows, d), F32)],
        compiler_params=_cparams("parallel", "arbitrary"),
        name="attn_prompt" if shared else "attn_sample",
    )(x, g.reshape(1, d), w_q, jnp.tile(q_norm, n_heads).reshape(1, d), w_o, sink_col, k_own, v_own, k_left, v_left,
      k_meta, v_meta, b_band, b_meta, e, et)


def _rel_bucket(rel):
    half = N_BUCKETS // 2
    max_exact = half // 2
    n = jnp.abs(rel)
    nf = jnp.maximum(n, 1).astype(F32)
    large = max_exact + (jnp.log(nf / max_exact) / math.log(MAX_DISTANCE / max_exact)
                         * (half - max_exact)).astype(jnp.int32)
    large = jnp.minimum(large, half - 1)
    return jnp.where(rel > 0, half, 0) + jnp.where(n < max_exact, n, large)


def _bias_rows(rel, rel_bias, chunk):
    n_heads = rel_bias.shape[1]
    q_per_kv = n_heads // N_KV_HEADS
    b = rel_bias[_rel_bucket(rel)]
    nq, ns = rel.shape
    b = b.reshape(nq // chunk, chunk, ns, N_KV_HEADS, q_per_kv)
    b = jnp.transpose(b, (0, 3, 4, 1, 2))
    return b.reshape(nq // chunk, N_KV_HEADS, q_per_kv * chunk, ns)


def kernel(x_prompt, x_sample, state_ssm_re, state_ssm_im, cache_k, cache_v, meta_tokens, norm_mix, norm_ffn,
           ssm_a_re, ssm_a_im, ssm_log_dt, ssm_b_re, ssm_b_im, ssm_c_re, ssm_c_im, ssm_d, w_glu, w_ffn_in,
           w_ffn_out, norm_kv, w_kv, k_norm, w_q, q_norm, attn_sinks, w_o, rel_bias):
    bsz, seq, d = x_prompt.shape
    dbsz, dseq, _ = x_sample.shape
    n_a = ssm_a_re.shape[0]
    n_b = w_q.shape[0]
    n_groups, n_state = ssm_a_re.shape[1:]
    kvw = N_KV_HEADS * HEAD_DIM

    w_glu_b = w_glu.astype(BF16)
    w_in_b = w_ffn_in.astype(BF16)
    w_out_b = w_ffn_out.astype(BF16)
    w_kv_b = w_kv.astype(BF16)
    w_q_b = w_q.astype(BF16)
    w_o_b = w_o.astype(BF16)
    tabs = [_s5_tables(ssm_a_re[l], ssm_a_im[l], ssm_log_dt[l], ssm_b_re[l], ssm_b_im[l], ssm_c_re[l],
                       ssm_c_im[l], ssm_d[l]) for l in range(n_a)]

    n_s = dbsz * dseq
    nseq_small = ((dbsz + 1 + SUBLANES - 1) // SUBLANES) * SUBLANES
    xs = jnp.concatenate([x_sample.reshape(n_s, d), meta_tokens], axis=0)
    s_re, s_im, meta_h = [], [], []
    zeros_state = jnp.zeros((nseq_small - dbsz, n_groups * n_state), F32)
    for l in range(n_a):
        u = _rms_call(xs, norm_mix[l])
        u_seq = jnp.concatenate([
            u[:n_s].reshape(dbsz, dseq, d),
            jnp.concatenate([jnp.zeros((dseq - N_META, d), F32), u[n_s:]], axis=0)[None],
            jnp.zeros((nseq_small - dbsz - 1, dseq, d), F32)], axis=0)
        h0r = jnp.concatenate([state_ssm_re[l].reshape(dbsz, -1), zeros_state], axis=0)
        h0i = jnp.concatenate([state_ssm_im[l].reshape(dbsz, -1), zeros_state], axis=0)
        yg, hfr, hfi = _s5_call(_to_groups(u_seq), tabs[l], h0r, h0i, nseq_small)
        y_seq = _from_groups(yg, nseq_small, dseq)
        y = jnp.concatenate([y_seq[:dbsz].reshape(n_s, d), y_seq[dbsz, dseq - N_META:]], axis=0)
        xs = _glu_call(xs, y, w_glu_b[l])
        xs = _ffn_call(xs, norm_ffn[l], w_in_b[l], w_out_b[l])
        s_re.append(hfr[:dbsz].reshape(dbsz, n_groups, n_state))
        s_im.append(hfi[:dbsz].reshape(dbsz, n_groups, n_state))
        meta_h.append((hfr[dbsz:dbsz + 1], hfi[dbsz:dbsz + 1]))
    k_small, v_small = _kv_call(xs, norm_kv, w_kv_b, k_norm)
    k_s, v_s = k_small[:n_s], v_small[:n_s]
    k_meta, v_meta = k_small[n_s:], v_small[n_s:]

    qi = jnp.arange(dseq)
    sj = jnp.arange(WINDOW + dseq)
    mj = jnp.arange(N_META)
    bband_s = _bias_rows(sj[None, :] - WINDOW - qi[:, None], rel_bias, dseq)[0]
    bmeta_s = _bias_rows(mj[None, :] - N_META - (PAST_LEN + qi)[:, None], rel_bias, dseq)
    xq = xs[:n_s]
    ck = cache_k.reshape(dbsz, WINDOW, kvw)
    cv = cache_v.reshape(dbsz, WINDOW, kvw)
    for j in range(n_b):
        l = n_a + j
        xq = _attn_call(xq, norm_mix[l], w_q_b[j], q_norm[j], w_o_b[j], attn_sinks[j], k_s, v_s, ck, cv, k_meta,
                        v_meta, bband_s, bmeta_s, chunk=dseq, shared=False, nseq=1,
                        tiles_per_seq=dbsz // ATTN_SEQS_SAMPLE, chunks_per_tile=ATTN_SEQS_SAMPLE)
        xq = _ffn_call(xq, norm_ffn[l], w_in_b[l], w_out_b[l])
    y_sample = xq.reshape(dbsz, dseq, d)

    xp = x_prompt.reshape(bsz * seq, d)
    p_re, p_im = [], []
    for l in range(n_a):
        u = _rms_call(xp, norm_mix[l])
        h0r = jnp.broadcast_to(meta_h[l][0], (bsz, n_groups * n_state))
        h0i = jnp.broadcast_to(meta_h[l][1], (bsz, n_groups * n_state))
        yg, hfr, hfi = _s5_call(_to_groups(u.reshape(bsz, seq, d)), tabs[l], h0r, h0i, bsz)
        y = _from_groups(yg, bsz, seq).reshape(bsz * seq, d)
        xp = _glu_call(xp, y, w_glu_b[l])
        xp = _ffn_call(xp, norm_ffn[l], w_in_b[l], w_out_b[l])
        p_re.append(hfr.reshape(bsz, n_groups, n_state))
        p_im.append(hfi.reshape(bsz, n_groups, n_state))
    k_p, v_p = _kv_call(xp, norm_kv, w_kv_b, k_norm)

    qi = jnp.arange(CHUNK)
    sj = jnp.arange(WINDOW + CHUNK)
    bband_p = _bias_rows(sj[None, :] - WINDOW - qi[:, None], rel_bias, CHUNK)[0]
    bmeta_p = _bias_rows(mj[None, :] - N_META - jnp.arange(seq)[:, None], rel_bias, CHUNK)
    tile_rows = ATTN_CHUNKS_PROMPT * CHUNK
    for j in range(n_b):
        l = n_a + j
        xp = _attn_call(xp, norm_mix[l], w_q_b[j], q_norm[j], w_o_b[j], attn_sinks[j], k_p, v_p, k_p, v_p, k_meta,
                        v_meta, bband_p, bmeta_p, chunk=CHUNK, shared=True, nseq=bsz,
                        tiles_per_seq=seq // tile_rows, chunks_per_tile=ATTN_CHUNKS_PROMPT)
        xp = _ffn_call(xp, norm_ffn[l], w_in_b[l], w_out_b[l])
    y_prompt = xp.reshape(bsz, seq, d)

    k_p4 = k_p.reshape(bsz, seq, N_KV_HEADS, HEAD_DIM)
    v_p4 = v_p.reshape(bsz, seq, N_KV_HEADS, HEAD_DIM)
    return (y_prompt, y_sample, jnp.stack(p_re), jnp.stack(p_im), k_p4[:, -WINDOW:], v_p4[:, -WINDOW:],
            jnp.stack(s_re), jnp.stack(s_im), k_s.reshape(dbsz, dseq, N_KV_HEADS, HEAD_DIM),
            v_s.reshape(dbsz, dseq, N_KV_HEADS, HEAD_DIM))
```

```python
import functools
import math

import jax
import jax.numpy as jnp
from jax import lax
from jax.experimental import pallas as pl
from jax.experimental.pallas import tpu as pltpu

F32 = jnp.float32
BF16 = jnp.bfloat16

CHUNK = 64
N_META = 16
SSM_GROUP = 16
SSM_STATE = 64
HEAD_DIM = 64
N_KV_HEADS = 4
WINDOW = 128
N_BUCKETS = 32
MAX_DISTANCE = 128
PAST_LEN = 1024
EPS = 1e-6
NEG = -1e30

LANES = 128
SUBLANES = 8
VMEM_LIMIT_BYTES = 56 * 1024 * 1024

S5_T = 16
S5_PAIR = 2 * SSM_GROUP
S5_PAIRS = LANES // S5_PAIR
S5_TOKENS = 2048
ROW_TILE = 512
FFN_CHUNK = 256
ATTN_CHUNKS_PROMPT = 4
ATTN_SEQS_SAMPLE = 4


def _row_tile(n, target):
    if n <= target:
        return n
    best = None
    for t in range(SUBLANES, target + 1, SUBLANES):
        if n % t == 0:
            best = t
    assert best is not None, n
    return best


def _cparams(*sem):
    return pltpu.CompilerParams(dimension_semantics=sem, vmem_limit_bytes=VMEM_LIMIT_BYTES)


def _resident(shape):
    nd = len(shape)
    return pl.BlockSpec(shape, lambda *_: (0,) * nd, pipeline_mode=pl.Buffered(1))


def _rms(x, g):
    ms = jnp.mean(x * x, axis=-1, keepdims=True)
    return (x * lax.rsqrt(ms + EPS)) * g


def _dot(a, b):
    return jnp.dot(a, b, preferred_element_type=F32)


def _dot_nt(a, b):
    return lax.dot_general(a, b, (((1,), (1,)), ((), ())), preferred_element_type=F32)


def _head_rms(x, e, et, gain):
    ss = _dot((x * x).astype(BF16), e)
    rinv = lax.rsqrt(ss * (1.0 / HEAD_DIM) + EPS)
    hi = rinv.astype(BF16)
    lo = (rinv - hi.astype(F32)).astype(BF16)
    rb = _dot(hi, et) + _dot(lo, et)
    return (x * rb) * gain


def _segment_mats(n_heads):
    idx = jnp.arange(n_heads * HEAD_DIM) // HEAD_DIM
    e = (idx[:, None] == jnp.arange(LANES)[None, :]).astype(BF16)
    return e, e.T


def _rms_body(x_ref, g_ref, o_ref):
    o_ref[...] = _rms(x_ref[...], g_ref[...])


def _rms_call(x, g):
    n, d = x.shape
    tm = _row_tile(n, 1024)
    return pl.pallas_call(
        _rms_body,
        grid=(n // tm,),
        in_specs=[pl.BlockSpec((tm, d), lambda i: (i, 0)), _resident((1, d))],
        out_specs=pl.BlockSpec((tm, d), lambda i: (i, 0)),
        out_shape=jax.ShapeDtypeStruct((n, d), F32),
        compiler_params=_cparams("parallel"),
        name="rms",
    )(x, g.reshape(1, d))


def _xpose4(a, lo64, lo32):
    a0, a1, a2, a3 = a
    b0 = jnp.where(lo64, a0, pltpu.roll(a2, 64, 1))
    b2 = jnp.where(lo64, pltpu.roll(a0, 64, 1), a2)
    b1 = jnp.where(lo64, a1, pltpu.roll(a3, 64, 1))
    b3 = jnp.where(lo64, pltpu.roll(a1, 64, 1), a3)
    return (jnp.where(lo32, b0, pltpu.roll(b1, 32, 1)), jnp.where(lo32, pltpu.roll(b0, 96, 1), b1),
            jnp.where(lo32, b2, pltpu.roll(b3, 32, 1)), jnp.where(lo32, pltpu.roll(b2, 96, 1), b3))


def _s5_body(spb, cps, u_ref, m_ref, wo_ref, wi_ref, at_ref, h0_ref, y_ref, hf_ref, s_scr, hin_scr, h_scr):
    nb = u_ref.shape[0]
    t = S5_T
    rows_b = spb * cps
    nc = nb * rows_b
    ns = nb * spb
    ncol = 2 * S5_PAIRS

    @pl.when(pl.program_id(1) == 0)
    def _():
        h_scr[...] = h0_ref[...]

    lane = lax.broadcasted_iota(jnp.int32, (nc, LANES), 1)
    lo64 = lane < 64
    lo32 = (lane % 64) < 32

    v = []
    for tok in range(t):
        parts = [u_ref[b, pl.ds(tok, rows_b, stride=t), :] for b in range(nb)]
        v.append(parts[0] if nb == 1 else jnp.concatenate(parts, axis=0))
    quads = [_xpose4(v[4 * tq:4 * tq + 4], lo64, lo32) for tq in range(t // 4)]
    u_p = [jnp.concatenate([quads[tq][p] for tq in range(t // 4)], axis=1).astype(BF16) for p in range(S5_PAIRS)]

    for p in range(S5_PAIRS):
        s = _dot(u_p[p], wo_ref[p])
        s_scr[2 * p, 0:nc, :] = s[:, :LANES]
        s_scr[2 * p + 1, 0:nc, :] = s[:, LANES:]

    def step(k, carry):
        new = []
        for p in range(S5_PAIRS):
            cr, ci = carry[2 * p], carry[2 * p + 1]
            hin_scr[2 * p, pl.ds(k, ns, stride=cps), :] = cr
            hin_scr[2 * p + 1, pl.ds(k, ns, stride=cps), :] = ci
            ar = at_ref[2 * p]
            ai = at_ref[2 * p + 1]
            tr = s_scr[2 * p, pl.ds(k, ns, stride=cps), :]
            ti = s_scr[2 * p + 1, pl.ds(k, ns, stride=cps), :]
            new += [ar * cr - ai * ci + tr, ar * ci + ai * cr + ti]
        return tuple(new)

    carry = lax.fori_loop(0, cps, step, tuple(h_scr[c] for c in range(ncol)))
    for c in range(ncol):
        h_scr[c] = carry[c]
        hf_ref[c] = carry[c]

    y_p = []
    for p in range(S5_PAIRS):
        h = jnp.concatenate([hin_scr[2 * p, 0:nc, :], hin_scr[2 * p + 1, 0:nc, :]], axis=1).astype(BF16)
        y_p.append(_dot(u_p[p], m_ref[p]) + _dot(h, wi_ref[p]))
    for tq in range(t // 4):
        w = _xpose4([y_p[p][:, LANES * tq:LANES * (tq + 1)] for p in range(S5_PAIRS)], lo64, lo32)
        for i in range(4):
            for b in range(nb):
                y_ref[b, pl.ds(4 * tq + i, rows_b, stride=t), :] = w[i][b * rows_b:(b + 1) * rows_b]


def _s5_call(u, tabs, h0, spb, cps, tokens):
    m, wo, wi, at = tabs
    nb, length, d = u.shape
    assert tokens == spb * cps * S5_T and length % tokens == 0
    ns = nb * spb
    ncol = 2 * S5_PAIRS
    nc_pad = -(-(nb * spb * cps) // SUBLANES) * SUBLANES
    width = S5_T * S5_PAIR
    blk = pl.BlockSpec((nb, tokens, LANES), lambda b, r: (0, r, b))
    state = pl.BlockSpec((ncol, ns, LANES), lambda b, r: (b, 0, 0))
    return pl.pallas_call(
        functools.partial(_s5_body, spb, cps),
        grid=(d // LANES, length // tokens),
        in_specs=[
            blk,
            pl.BlockSpec((S5_PAIRS, width, width), lambda b, r: (b, 0, 0)),
            pl.BlockSpec((S5_PAIRS, width, 2 * LANES), lambda b, r: (b, 0, 0)),
            pl.BlockSpec((S5_PAIRS, 2 * LANES, width), lambda b, r: (b, 0, 0)),
            pl.BlockSpec((ncol, 1, LANES), lambda b, r: (b, 0, 0)),
            state,
        ],
        out_specs=[blk, state],
        out_shape=[jax.ShapeDtypeStruct(u.shape, F32), jax.ShapeDtypeStruct(h0.shape, F32)],
        scratch_shapes=[pltpu.VMEM((ncol, nc_pad, LANES), F32), pltpu.VMEM((ncol, nc_pad, LANES), F32),
                        pltpu.VMEM((ncol, ns, LANES), F32)],
        compiler_params=_cparams("parallel", "arbitrary"),
        name="s5",
    )(u, m, wo, wi, at, h0)


def _s5_tables(a_re, a_im, log_dt, b_re, b_im, c_re, c_im, d_skip):
    t = S5_T
    g, p = a_re.shape
    c = b_re.shape[-1]
    hp = lax.Precision.HIGHEST
    dt = jnp.exp(log_dt)[:, None]
    mag = jnp.exp(a_re * dt)
    ang = a_im * dt
    ar = mag * jnp.cos(ang)
    ai = mag * jnp.sin(ang)
    num_re = ar - 1.0
    num_im = ai
    inv = 1.0 / (a_re * a_re + a_im * a_im)
    f_re = (num_re * a_re + num_im * a_im) * inv
    f_im = (num_im * a_re - num_re * a_im) * inv
    bb_re = f_re[..., None] * b_re - f_im[..., None] * b_im
    bb_im = f_re[..., None] * b_im + f_im[..., None] * b_re
    pr = jnp.ones((1, g, p), F32)
    pi = jnp.zeros((1, g, p), F32)
    sr, si = ar, ai
    while pr.shape[0] < t + 1:
        pr, pi = (jnp.concatenate([pr, pr * sr - pi * si]), jnp.concatenate([pi, pr * si + pi * sr]))
        sr, si = sr * sr - si * si, 2.0 * sr * si
    pr = pr[:t + 1]
    pi = pi[:t + 1]
    eye2 = jnp.eye(2, dtype=F32)
    ca_r = c_re[None] * pr[:t, :, None, :] - c_im[None] * pi[:t, :, None, :]
    ca_i = c_re[None] * pi[:t, :, None, :] + c_im[None] * pr[:t, :, None, :]
    k = (jnp.einsum('tgdp,gpc->tgdc', ca_r, bb_re, precision=hp)
         - jnp.einsum('tgdp,gpc->tgdc', ca_i, bb_im, precision=hp))
    k = k.at[0].add(d_skip[:, None, :] * jnp.eye(c, dtype=F32)[None])
    kt = jnp.transpose(k, (1, 3, 2, 0))
    v = jnp.concatenate([kt, jnp.zeros_like(kt)], axis=-1)
    toep = jnp.tile(v, (1, 1, 1, t))[..., :t * (2 * t - 1)].reshape(g // 2, 2, c, c, t, 2 * t - 1)[..., :t]
    m = jnp.transpose(toep, (0, 4, 1, 2, 5, 3))
    m = (m[:, :, :, :, :, None, :] * eye2[None, None, :, None, None, :, None]).reshape(g // 2, 2 * t * c, 2 * t * c)
    qr = pr[:t][::-1]
    qi = pi[:t][::-1]
    wo = jnp.stack([qr[..., None] * bb_re[None] - qi[..., None] * bb_im[None],
                    qr[..., None] * bb_im[None] + qi[..., None] * bb_re[None]])
    wo = jnp.transpose(wo.reshape(2, t, g // 2, 2, p, c), (2, 1, 3, 5, 0, 4))
    wo = (wo[:, :, :, :, :, None, :] * eye2[None, None, :, None, None, :, None]).reshape(g // 2, 2 * t * c, 4 * p)
    wi = jnp.stack([c_re[None] * pr[1:, :, None, :] - c_im[None] * pi[1:, :, None, :],
                    -(c_re[None] * pi[1:, :, None, :] + c_im[None] * pr[1:, :, None, :])])
    wi = jnp.transpose(wi.reshape(2, t, g // 2, 2, c, p), (2, 0, 3, 5, 1, 4))
    wi = (wi[:, :, :, :, :, None, :] * eye2[None, None, :, None, None, :, None]).reshape(g // 2, 4 * p, 2 * t * c)
    at = jnp.stack([pr[t].reshape(g // 2, 2 * p), pi[t].reshape(g // 2, 2 * p)], axis=1).reshape(g, 1, 2 * p)
    return m.astype(BF16), wo.astype(BF16), wi.astype(BF16), at


def _state_to_cols(h_re, h_im):
    ns, g, p = h_re.shape
    f = lambda h: jnp.transpose(h.reshape(ns, g // 2, 2 * p), (1, 0, 2))
    return jnp.stack([f(h_re), f(h_im)], axis=1).reshape(g, ns, 2 * p)


def _cols_to_state(h):
    g, ns, w = h.shape
    h = jnp.transpose(h.reshape(g // 2, 2, ns, 2, w // 2), (1, 2, 0, 3, 4)).reshape(2, ns, g, w // 2)
    return h[0], h[1]


def _glu_body(x_ref, y_ref, w_ref, o_ref):
    d = x_ref.shape[1]
    a = jax.nn.gelu(y_ref[...]).astype(BF16)
    r = _dot(a, w_ref[...])
    o_ref[...] = x_ref[...] + r[:, :d] * jax.nn.sigmoid(r[:, d:])


def _glu_call(x, y, w):
    n, d = x.shape
    tm = _row_tile(n, ROW_TILE)
    row = pl.BlockSpec((tm, d), lambda i: (i, 0))
    return pl.pallas_call(
        _glu_body,
        grid=(n // tm,),
        in_specs=[row, row, _resident(w.shape)],
        out_specs=row,
        out_shape=jax.ShapeDtypeStruct((n, d), F32),
        compiler_params=_cparams("parallel"),
        name="glu",
    )(x, y, w)


def _ffn_body(x_ref, g_ref, win_ref, wout_ref, o_ref):
    dff = wout_ref.shape[0]
    x = x_ref[...]
    xn = _rms(x, g_ref[...]).astype(BF16)
    acc = x
    for j in range(dff // FFN_CHUNK):
        lo = j * FFN_CHUNK
        gate = _dot(xn, win_ref[:, lo:lo + FFN_CHUNK])
        up = _dot(xn, win_ref[:, dff + lo:dff + lo + FFN_CHUNK])
        a = (jax.nn.silu(gate) * up).astype(BF16)
        acc = acc + _dot(a, wout_ref[lo:lo + FFN_CHUNK, :])
    o_ref[...] = acc


def _ffn_call(x, g, w_in, w_out):
    n, d = x.shape
    tm = _row_tile(n, ROW_TILE)
    row = pl.BlockSpec((tm, d), lambda i: (i, 0))
    return pl.pallas_call(
        _ffn_body,
        grid=(n // tm,),
        in_specs=[row, _resident((1, d)), _resident(w_in.shape), _resident(w_out.shape)],
        out_specs=row,
        out_shape=jax.ShapeDtypeStruct((n, d), F32),
        compiler_params=_cparams("parallel"),
        name="ffn",
    )(x, g.reshape(1, d), w_in, w_out)


def _kv_body(x_ref, g_ref, w_ref, kn_ref, e_ref, et_ref, k_ref, v_ref):
    kvw = k_ref.shape[1]
    xn = _rms(x_ref[...], g_ref[...]).astype(BF16)
    r = _dot(xn, w_ref[...])
    k_ref[...] = _head_rms(r[:, :kvw], e_ref[...], et_ref[...], kn_ref[...])
    v_ref[...] = r[:, kvw:]


def _kv_call(x, g, w, k_norm):
    n, d = x.shape
    kvw = w.shape[1] // 2
    tm = _row_tile(n, ROW_TILE)
    e, et = _segment_mats(kvw // HEAD_DIM)
    out = pl.BlockSpec((tm, kvw), lambda i: (i, 0))
    return pl.pallas_call(
        _kv_body,
        grid=(n // tm,),
        in_specs=[pl.BlockSpec((tm, d), lambda i: (i, 0)), _resident((1, d)), _resident(w.shape),
                  _resident((1, kvw)), _resident(e.shape), _resident(et.shape)],
        out_specs=[out, out],
        out_shape=[jax.ShapeDtypeStruct((n, kvw), F32)] * 2,
        compiler_params=_cparams("parallel"),
        name="kv",
    )(x, g.reshape(1, d), w, jnp.tile(k_norm, kvw // HEAD_DIM).reshape(1, kvw), e, et)


def _attn_body(chunk, shared, x_ref, g_ref, wq_ref, qn_ref, wo_ref, sink_ref, kown_ref, vown_ref, kleft_ref,
               vleft_ref, kmeta_ref, vmeta_ref, bband_ref, bmeta_ref, e_ref, et_ref, o_ref, att_ref):
    rows, d = x_ref.shape
    nchunks = rows // chunk
    n_heads = d // HEAD_DIM
    q_per_kv = n_heads // N_KV_HEADS
    band = WINDOW + chunk

    x = x_ref[...]
    xn = _rms(x, g_ref[...]).astype(BF16)
    q = _dot(xn, wq_ref[...])
    q = _head_rms(q, e_ref[...], et_ref[...], qn_ref[...]) * (HEAD_DIM ** -0.5)
    qb = q.astype(BF16)

    k_meta = kmeta_ref[...].astype(BF16)
    v_meta = vmeta_ref[...].astype(BF16)
    k_own = kown_ref[...].astype(BF16)
    v_own = vown_ref[...].astype(BF16)
    if shared:
        k_all = jnp.concatenate([kleft_ref[...].astype(BF16), k_own], axis=0)
        v_all = jnp.concatenate([vleft_ref[...].astype(BF16), v_own], axis=0)
        first_chunk = pl.program_id(1) * nchunks
        col = lax.broadcasted_iota(jnp.int32, (q_per_kv * chunk, band), 1)

    for j in range(nchunks):
        r0 = j * chunk
        if shared:
            kb = k_all[r0:r0 + band]
            vb = v_all[r0:r0 + band]
            valid = col >= WINDOW - chunk * (first_chunk + j)
            bm = bmeta_ref[j]
        else:
            kb = jnp.concatenate([kleft_ref[j].astype(BF16), k_own[r0:r0 + chunk]], axis=0)
            vb = jnp.concatenate([vleft_ref[j].astype(BF16), v_own[r0:r0 + chunk]], axis=0)
            bm = bmeta_ref[0]
        for kv in range(N_KV_HEADS):
            hs = slice(kv * HEAD_DIM, (kv + 1) * HEAD_DIM)
            qk = jnp.concatenate(
                [qb[r0:r0 + chunk, (kv * q_per_kv + gq) * HEAD_DIM:(kv * q_per_kv + gq + 1) * HEAD_DIM]
                 for gq in range(q_per_kv)], axis=0)
            lb = _dot_nt(qk, kb[:, hs]) + bband_ref[kv]
            if shared:
                lb = jnp.where(valid, lb, NEG)
            lm = _dot_nt(qk, k_meta[:, hs]) + bm[kv]
            s = sink_ref[kv]
            mx = jnp.maximum(jnp.maximum(jnp.max(lb, axis=-1, keepdims=True), jnp.max(lm, axis=-1, keepdims=True)), s)
            pb = jnp.exp(lb - mx)
            pm = jnp.exp(lm - mx)
            den = jnp.sum(pm, axis=-1, keepdims=True) + jnp.sum(pb, axis=-1, keepdims=True) + jnp.exp(s - mx)
            o = (_dot(pm.astype(BF16), v_meta[:, hs]) + _dot(pb.astype(BF16), vb[:, hs])) / den
            for gq in range(q_per_kv):
                h0 = (kv * q_per_kv + gq) * HEAD_DIM
                att_ref[r0:r0 + chunk, h0:h0 + HEAD_DIM] = o[gq * chunk:(gq + 1) * chunk]

    o_ref[...] = x + _dot(att_ref[...].astype(BF16), wo_ref[...])


def _attn_call(x, g, w_q, q_norm, w_o, sinks, k_own, v_own, k_left, v_left, k_meta, v_meta, b_band, b_meta,
               chunk, shared, nseq, tiles_per_seq, chunks_per_tile):
    n, d = x.shape
    kvw = k_own.shape[1]
    n_heads = d // HEAD_DIM
    q_per_kv = n_heads // N_KV_HEADS
    rows = chunks_per_tile * chunk
    e, et = _segment_mats(n_heads)
    sink_col = jnp.broadcast_to(sinks.reshape(N_KV_HEADS, q_per_kv, 1, 1),
                                (N_KV_HEADS, q_per_kv, chunk, 1)).reshape(N_KV_HEADS, q_per_kv * chunk, 1)
    row_idx = lambda b, i: (b * tiles_per_seq + i, 0)
    if shared:
        halo_blocks = rows // WINDOW
        left_spec = pl.BlockSpec(
            (WINDOW, kvw), lambda b, i: (jnp.maximum(b * tiles_per_seq * halo_blocks + i * halo_blocks - 1, 0), 0))
        bmeta_spec = pl.BlockSpec((chunks_per_tile,) + b_meta.shape[1:], lambda b, i: (i, 0, 0, 0))
    else:
        left_spec = pl.BlockSpec((chunks_per_tile, WINDOW, kvw), lambda b, i: (b * tiles_per_seq + i, 0, 0))
        bmeta_spec = _resident(b_meta.shape)
    return pl.pallas_call(
        functools.partial(_attn_body, chunk, shared),
        grid=(nseq, tiles_per_seq),
        in_specs=[
            pl.BlockSpec((rows, d), row_idx), _resident((1, d)), _resident(w_q.shape), _resident((1, d)),
            _resident(w_o.shape), _resident(sink_col.shape),
            pl.BlockSpec((rows, kvw), row_idx), pl.BlockSpec((rows, kvw), row_idx), left_spec, left_spec,
            _resident(k_meta.shape), _resident(v_meta.shape), _resident(b_band.shape), bmeta_spec,
            _resident(e.shape), _resident(et.shape),
        ],
        out_specs=pl.BlockSpec((rows, d), row_idx),
        out_shape=jax.ShapeDtypeStruct((n, d), F32),
        scratch_shapes=[pltpu.VMEM((rows, d), F32)],
        compiler_params=_cparams("parallel", "arbitrary"),
        name="attn_prompt" if shared else "attn_sample",
    )(x, g.reshape(1, d), w_q, jnp.tile(q_norm, n_heads).reshape(1, d), w_o, sink_col, k_own, v_own, k_left, v_left,
      k_meta, v_meta, b_band, b_meta, e, et)


def _rel_bucket(rel):
    half = N_BUCKETS // 2
    max_exact = half // 2
    n = jnp.abs(rel)
    nf = jnp.maximum(n, 1).astype(F32)
    large = max_exact + (jnp.log(nf / max_exact) / math.log(MAX_DISTANCE / max_exact)
                         * (half - max_exact)).astype(jnp.int32)
    large = jnp.minimum(large, half - 1)
    return jnp.where(rel > 0, half, 0) + jnp.where(n < max_exact, n, large)


def _bias_rows(rel, rel_bias, chunk):
    n_heads = rel_bias.shape[1]
    q_per_kv = n_heads // N_KV_HEADS
    b = rel_bias[_rel_bucket(rel)]
    nq, ns = rel.shape
    b = b.reshape(nq // chunk, chunk, ns, N_KV_HEADS, q_per_kv)
    b = jnp.transpose(b, (0, 3, 4, 1, 2))
    return b.reshape(nq // chunk, N_KV_HEADS, q_per_kv * chunk, ns)


def kernel(x_prompt, x_sample, state_ssm_re, state_ssm_im, cache_k, cache_v, meta_tokens, norm_mix, norm_ffn,
           ssm_a_re, ssm_a_im, ssm_log_dt, ssm_b_re, ssm_b_im, ssm_c_re, ssm_c_im, ssm_d, w_glu, w_ffn_in,
           w_ffn_out, norm_kv, w_kv, k_norm, w_q, q_norm, attn_sinks, w_o, rel_bias):
    bsz, seq, d = x_prompt.shape
    dbsz, dseq, _ = x_sample.shape
    n_a = ssm_a_re.shape[0]
    n_b = w_q.shape[0]
    n_groups, n_state = ssm_a_re.shape[1:]
    kvw = N_KV_HEADS * HEAD_DIM

    w_glu_b = w_glu.astype(BF16)
    w_in_b = w_ffn_in.astype(BF16)
    w_out_b = w_ffn_out.astype(BF16)
    w_kv_b = w_kv.astype(BF16)
    w_q_b = w_q.astype(BF16)
    w_o_b = w_o.astype(BF16)
    tabs = [_s5_tables(ssm_a_re[l], ssm_a_im[l], ssm_log_dt[l], ssm_b_re[l], ssm_b_im[l], ssm_c_re[l],
                       ssm_c_im[l], ssm_d[l]) for l in range(n_a)]

    n_s = dbsz * dseq
    n_small = n_s + dseq
    xs = jnp.concatenate([x_sample.reshape(n_s, d), jnp.zeros((dseq - N_META, d), F32), meta_tokens], axis=0)
    zero_state = jnp.zeros((1, n_groups, n_state), F32)
    s_re, s_im, meta_h = [], [], []
    for l in range(n_a):
        u = _rms_call(xs, norm_mix[l])
        h0 = _state_to_cols(jnp.concatenate([state_ssm_re[l], zero_state], axis=0),
                            jnp.concatenate([state_ssm_im[l], zero_state], axis=0))
        y, hf = _s5_call(u.reshape(1, n_small, d), tabs[l], h0, spb=dbsz + 1, cps=dseq // S5_T, tokens=n_small)
        xs = _glu_call(xs, y.reshape(n_small, d), w_glu_b[l])
        xs = _ffn_call(xs, norm_ffn[l], w_in_b[l], w_out_b[l])
        hr, hi = _cols_to_state(hf)
        s_re.append(hr[:dbsz])
        s_im.append(hi[:dbsz])
        meta_h.append((hr[dbsz:], hi[dbsz:]))
    k_small, v_small = _kv_call(xs, norm_kv, w_kv_b, k_norm)
    k_s, v_s = k_small[:n_s], v_small[:n_s]
    k_meta, v_meta = k_small[n_small - N_META:], v_small[n_small - N_META:]

    qi = jnp.arange(dseq)
    sj = jnp.arange(WINDOW + dseq)
    mj = jnp.arange(N_META)
    bband_s = _bias_rows(sj[None, :] - WINDOW - qi[:, None], rel_bias, dseq)[0]
    bmeta_s = _bias_rows(mj[None, :] - N_META - (PAST_LEN + qi)[:, None], rel_bias, dseq)
    xq = xs[:n_s]
    ck = cache_k.reshape(dbsz, WINDOW, kvw)
    cv = cache_v.reshape(dbsz, WINDOW, kvw)
    for j in range(n_b):
        l = n_a + j
        xq = _attn_call(xq, norm_mix[l], w_q_b[j], q_norm[j], w_o_b[j], attn_sinks[j], k_s, v_s, ck, cv, k_meta,
                        v_meta, bband_s, bmeta_s, chunk=dseq, shared=False, nseq=1,
                        tiles_per_seq=dbsz // ATTN_SEQS_SAMPLE, chunks_per_tile=ATTN_SEQS_SAMPLE)
        xq = _ffn_call(xq, norm_ffn[l], w_in_b[l], w_out_b[l])
    y_sample = xq.reshape(dbsz, dseq, d)

    xp = x_prompt.reshape(bsz * seq, d)
    s5_tokens = min(S5_TOKENS, seq)
    p_re, p_im = [], []
    for l in range(n_a):
        u = _rms_call(xp, norm_mix[l])
        h0 = _state_to_cols(jnp.broadcast_to(meta_h[l][0], (bsz, n_groups, n_state)),
                            jnp.broadcast_to(meta_h[l][1], (bsz, n_groups, n_state)))
        y, hf = _s5_call(u.reshape(bsz, seq, d), tabs[l], h0, spb=1, cps=s5_tokens // S5_T, tokens=s5_tokens)
        xp = _glu_call(xp, y.reshape(bsz * seq, d), w_glu_b[l])
        xp = _ffn_call(xp, norm_ffn[l], w_in_b[l], w_out_b[l])
        hr, hi = _cols_to_state(hf)
        p_re.append(hr)
        p_im.append(hi)
    k_p, v_p = _kv_call(xp, norm_kv, w_kv_b, k_norm)

    qi = jnp.arange(CHUNK)
    sj = jnp.arange(WINDOW + CHUNK)
    bband_p = _bias_rows(sj[None, :] - WINDOW - qi[:, None], rel_bias, CHUNK)[0]
    bmeta_p = _bias_rows(mj[None, :] - N_META - jnp.arange(seq)[:, None], rel_bias, CHUNK)
    tile_rows = ATTN_CHUNKS_PROMPT * CHUNK
    for j in range(n_b):
        l = n_a + j
        xp = _attn_call(xp, norm_mix[l], w_q_b[j], q_norm[j], w_o_b[j], attn_sinks[j], k_p, v_p, k_p, v_p, k_meta,
                        v_meta, bband_p, bmeta_p, chunk=CHUNK, shared=True, nseq=bsz,
                        tiles_per_seq=seq // tile_rows, chunks_per_tile=ATTN_CHUNKS_PROMPT)
        xp = _ffn_call(xp, norm_ffn[l], w_in_b[l], w_out_b[l])
    y_prompt = xp.reshape(bsz, seq, d)

    k_p4 = k_p.reshape(bsz, seq, N_KV_HEADS, HEAD_DIM)
    v_p4 = v_p.reshape(bsz, seq, N_KV_HEADS, HEAD_DIM)
    return (y_prompt, y_sample, jnp.stack(p_re), jnp.stack(p_im), k_p4[:, -WINDOW:], v_p4[:, -WINDOW:],
            jnp.stack(s_re), jnp.stack(s_im), k_s.reshape(dbsz, dseq, N_KV_HEADS, HEAD_DIM),
            v_s.reshape(dbsz, dseq, N_KV_HEADS, HEAD_DIM))
```

```python
import functools
import math

import jax
import jax.numpy as jnp
from jax import lax
from jax.experimental import pallas as pl
from jax.experimental.pallas import tpu as pltpu

F32 = jnp.float32
BF16 = jnp.bfloat16

CHUNK = 64
N_META = 16
SSM_GROUP = 16
SSM_STATE = 64
HEAD_DIM = 64
N_KV_HEADS = 4
WINDOW = 128
N_BUCKETS = 32
MAX_DISTANCE = 128
PAST_LEN = 1024
EPS = 1e-6
NEG = -1e30

LANES = 128
SUBLANES = 8
VMEM_LIMIT_BYTES = 56 * 1024 * 1024

S5_T = 16
S5_PAIR = 2 * SSM_GROUP
S5_PAIRS = LANES // S5_PAIR
S5_TOKENS = 2048
ROW_TILE = 512
FFN_CHUNK = 256
ATTN_CHUNKS_PROMPT = 8
ATTN_SEQS_SAMPLE = 8


def _row_tile(n, target):
    if n <= target:
        return n
    best = None
    for t in range(SUBLANES, target + 1, SUBLANES):
        if n % t == 0:
            best = t
    assert best is not None, n
    return best


def _cparams(*sem):
    return pltpu.CompilerParams(dimension_semantics=sem, vmem_limit_bytes=VMEM_LIMIT_BYTES)


def _resident(shape):
    nd = len(shape)
    return pl.BlockSpec(shape, lambda *_: (0,) * nd, pipeline_mode=pl.Buffered(1))


def _rms(x, g):
    ms = jnp.mean(x * x, axis=-1, keepdims=True)
    return (x * lax.rsqrt(ms + EPS)) * g


def _dot(a, b):
    return jnp.dot(a, b, preferred_element_type=F32)


def _dot_nt(a, b):
    return lax.dot_general(a, b, (((1,), (1,)), ((), ())), preferred_element_type=F32)


def _head_rms(x, e, et, gain):
    ss = _dot((x * x).astype(BF16), e)
    rinv = lax.rsqrt(ss * (1.0 / HEAD_DIM) + EPS)
    hi = rinv.astype(BF16)
    lo = (rinv - hi.astype(F32)).astype(BF16)
    rb = _dot(hi, et) + _dot(lo, et)
    return (x * rb) * gain


def _segment_mats(n_heads):
    idx = jnp.arange(n_heads * HEAD_DIM) // HEAD_DIM
    e = (idx[:, None] == jnp.arange(LANES)[None, :]).astype(BF16)
    return e, e.T


def _rms_body(x_ref, g_ref, o_ref):
    o_ref[...] = _rms(x_ref[...], g_ref[...])


def _rms_call(x, g):
    n, d = x.shape
    tm = _row_tile(n, 1024)
    return pl.pallas_call(
        _rms_body,
        grid=(n // tm,),
        in_specs=[pl.BlockSpec((tm, d), lambda i: (i, 0)), _resident((1, d))],
        out_specs=pl.BlockSpec((tm, d), lambda i: (i, 0)),
        out_shape=jax.ShapeDtypeStruct((n, d), F32),
        compiler_params=_cparams("parallel"),
        name="rms",
    )(x, g.reshape(1, d))


def _xpose4(a, lo64, lo32):
    a0, a1, a2, a3 = a
    b0 = jnp.where(lo64, a0, pltpu.roll(a2, 64, 1))
    b2 = jnp.where(lo64, pltpu.roll(a0, 64, 1), a2)
    b1 = jnp.where(lo64, a1, pltpu.roll(a3, 64, 1))
    b3 = jnp.where(lo64, pltpu.roll(a1, 64, 1), a3)
    return (jnp.where(lo32, b0, pltpu.roll(b1, 32, 1)), jnp.where(lo32, pltpu.roll(b0, 96, 1), b1),
            jnp.where(lo32, b2, pltpu.roll(b3, 32, 1)), jnp.where(lo32, pltpu.roll(b2, 96, 1), b3))


def _s5_body(spb, cps, u_ref, k_ref, wo_ref, wi_ref, at_ref, h0_ref, y_ref, hf_ref, s_scr, hin_scr, h_scr, m_scr):
    nb = u_ref.shape[0]
    t = S5_T
    rows_b = spb * cps
    nc = nb * rows_b
    ns = nb * spb
    ncol = 2 * S5_PAIRS
    width = t * S5_PAIR

    @pl.when(pl.program_id(1) == 0)
    def _():
        h_scr[...] = h0_ref[...]
        strip_lane = lax.broadcasted_iota(jnp.int32, (S5_PAIR, width), 1)
        for p in range(S5_PAIRS):
            strip = k_ref[p]
            for s in range(t):
                blk = strip if s == 0 else jnp.where(strip_lane >= S5_PAIR * s, pltpu.roll(strip, S5_PAIR * s, 1), 0.0)
                m_scr[p, S5_PAIR * s:S5_PAIR * (s + 1), :] = blk.astype(BF16)

    lane = lax.broadcasted_iota(jnp.int32, (nc, LANES), 1)
    lo64 = lane < 64
    lo32 = (lane % 64) < 32

    v = []
    for tok in range(t):
        parts = [u_ref[b, pl.ds(tok, rows_b, stride=t), :] for b in range(nb)]
        v.append(parts[0] if nb == 1 else jnp.concatenate(parts, axis=0))
    quads = [_xpose4(v[4 * tq:4 * tq + 4], lo64, lo32) for tq in range(t // 4)]
    u_p = [jnp.concatenate([quads[tq][p] for tq in range(t // 4)], axis=1).astype(BF16) for p in range(S5_PAIRS)]

    for p in range(S5_PAIRS):
        s = _dot(u_p[p], wo_ref[p])
        s_scr[2 * p, 0:nc, :] = s[:, :LANES]
        s_scr[2 * p + 1, 0:nc, :] = s[:, LANES:]

    def step(k, carry):
        new = []
        for p in range(S5_PAIRS):
            cr, ci = carry[2 * p], carry[2 * p + 1]
            hin_scr[2 * p, pl.ds(k, ns, stride=cps), :] = cr
            hin_scr[2 * p + 1, pl.ds(k, ns, stride=cps), :] = ci
            ar = at_ref[2 * p]
            ai = at_ref[2 * p + 1]
            tr = s_scr[2 * p, pl.ds(k, ns, stride=cps), :]
            ti = s_scr[2 * p + 1, pl.ds(k, ns, stride=cps), :]
            new += [ar * cr - ai * ci + tr, ar * ci + ai * cr + ti]
        return tuple(new)

    carry = lax.fori_loop(0, cps, step, tuple(h_scr[c] for c in range(ncol)))
    for c in range(ncol):
        h_scr[c] = carry[c]
        hf_ref[c] = carry[c]

    y_p = []
    for p in range(S5_PAIRS):
        h = jnp.concatenate([hin_scr[2 * p, 0:nc, :], hin_scr[2 * p + 1, 0:nc, :]], axis=1).astype(BF16)
        y_p.append(_dot(u_p[p], m_scr[p]) + _dot(h, wi_ref[p]))
    for tq in range(t // 4):
        w = _xpose4([y_p[p][:, LANES * tq:LANES * (tq + 1)] for p in range(S5_PAIRS)], lo64, lo32)
        for i in range(4):
            for b in range(nb):
                y_ref[b, pl.ds(4 * tq + i, rows_b, stride=t), :] = w[i][b * rows_b:(b + 1) * rows_b]


def _s5_call(u, tabs, h0, spb, cps, tokens):
    strip, wo, wi, at = tabs
    nb, length, d = u.shape
    assert tokens == spb * cps * S5_T and length % tokens == 0
    ns = nb * spb
    ncol = 2 * S5_PAIRS
    nc_pad = -(-(nb * spb * cps) // SUBLANES) * SUBLANES
    width = S5_T * S5_PAIR
    blk = pl.BlockSpec((nb, tokens, LANES), lambda b, r: (0, r, b))
    state = pl.BlockSpec((ncol, ns, LANES), lambda b, r: (b, 0, 0))
    return pl.pallas_call(
        functools.partial(_s5_body, spb, cps),
        grid=(d // LANES, length // tokens),
        in_specs=[
            blk,
            pl.BlockSpec((S5_PAIRS, S5_PAIR, width), lambda b, r: (b, 0, 0)),
            pl.BlockSpec((S5_PAIRS, width, 2 * LANES), lambda b, r: (b, 0, 0)),
            pl.BlockSpec((S5_PAIRS, 2 * LANES, width), lambda b, r: (b, 0, 0)),
            pl.BlockSpec((ncol, 1, LANES), lambda b, r: (b, 0, 0)),
            state,
        ],
        out_specs=[blk, state],
        out_shape=[jax.ShapeDtypeStruct(u.shape, F32), jax.ShapeDtypeStruct(h0.shape, F32)],
        scratch_shapes=[pltpu.VMEM((ncol, nc_pad, LANES), F32), pltpu.VMEM((ncol, nc_pad, LANES), F32),
                        pltpu.VMEM((ncol, ns, LANES), F32), pltpu.VMEM((S5_PAIRS, width, width), BF16)],
        compiler_params=_cparams("parallel", "arbitrary"),
        name="s5",
    )(u, strip, wo, wi, at, h0)


def _s5_tables(a_re, a_im, log_dt, b_re, b_im, c_re, c_im, d_skip):
    t = S5_T
    g, p = a_re.shape
    c = b_re.shape[-1]
    hp = lax.Precision.HIGHEST
    dt = jnp.exp(log_dt)[:, None]
    mag = jnp.exp(a_re * dt)
    ang = a_im * dt
    ar = mag * jnp.cos(ang)
    ai = mag * jnp.sin(ang)
    num_re = ar - 1.0
    num_im = ai
    inv = 1.0 / (a_re * a_re + a_im * a_im)
    f_re = (num_re * a_re + num_im * a_im) * inv
    f_im = (num_im * a_re - num_re * a_im) * inv
    bb_re = f_re[..., None] * b_re - f_im[..., None] * b_im
    bb_im = f_re[..., None] * b_im + f_im[..., None] * b_re
    pr = jnp.ones((1, g, p), F32)
    pi = jnp.zeros((1, g, p), F32)
    sr, si = ar, ai
    while pr.shape[0] < t + 1:
        pr, pi = (jnp.concatenate([pr, pr * sr - pi * si]), jnp.concatenate([pi, pr * si + pi * sr]))
        sr, si = sr * sr - si * si, 2.0 * sr * si
    pr = pr[:t + 1]
    pi = pi[:t + 1]
    eye2 = jnp.eye(2, dtype=F32)
    ca_r = c_re[None] * pr[:t, :, None, :] - c_im[None] * pi[:t, :, None, :]
    ca_i = c_re[None] * pi[:t, :, None, :] + c_im[None] * pr[:t, :, None, :]
    k = (jnp.einsum('tgdp,gpc->tgdc', ca_r, bb_re, precision=hp)
         - jnp.einsum('tgdp,gpc->tgdc', ca_i, bb_im, precision=hp))
    k = k.at[0].add(d_skip[:, None, :] * jnp.eye(c, dtype=F32)[None])
    strip = jnp.transpose(k.reshape(t, g // 2, 2, c, c), (1, 2, 4, 0, 3))
    strip = (strip[:, :, :, :, None, :] * eye2[None, :, None, None, :, None]).reshape(g // 2, 2 * c, 2 * t * c)
    qr = jnp.transpose(pr[:t][::-1].reshape(t, g // 2, 2, p), (1, 0, 2, 3))[:, :, :, None, :]
    qi = jnp.transpose(pi[:t][::-1].reshape(t, g // 2, 2, p), (1, 0, 2, 3))[:, :, :, None, :]
    bt_r = jnp.transpose(bb_re, (0, 2, 1)).reshape(g // 2, 1, 2, c, p)
    bt_i = jnp.transpose(bb_im, (0, 2, 1)).reshape(g // 2, 1, 2, c, p)
    wo = jnp.stack([qr * bt_r - qi * bt_i, qr * bt_i + qi * bt_r], axis=-2)
    wo = (wo[:, :, :, :, :, None, :] * eye2[None, None, :, None, None, :, None]).reshape(g // 2, 2 * t * c, 4 * p)
    ct_r = jnp.transpose(c_re, (0, 2, 1)).reshape(g // 2, 2, p, 1, c)
    ct_i = jnp.transpose(c_im, (0, 2, 1)).reshape(g // 2, 2, p, 1, c)
    p1_r = jnp.transpose(pr[1:].reshape(t, g // 2, 2, p), (1, 2, 3, 0))[..., None]
    p1_i = jnp.transpose(pi[1:].reshape(t, g // 2, 2, p), (1, 2, 3, 0))[..., None]
    wi = jnp.stack([ct_r * p1_r - ct_i * p1_i, -(ct_r * p1_i + ct_i * p1_r)], axis=1)
    wi = (wi[:, :, :, :, :, None, :] * eye2[None, None, :, None, None, :, None]).reshape(g // 2, 4 * p, 2 * t * c)
    at = jnp.stack([pr[t].reshape(g // 2, 2 * p), pi[t].reshape(g // 2, 2 * p)], axis=1).reshape(g, 1, 2 * p)
    return strip, wo.astype(BF16), wi.astype(BF16), at


def _state_to_cols(h_re, h_im):
    ns, g, p = h_re.shape
    f = lambda h: jnp.transpose(h.reshape(ns, g // 2, 2 * p), (1, 0, 2))
    return jnp.stack([f(h_re), f(h_im)], axis=1).reshape(g, ns, 2 * p)


def _cols_to_state(h):
    g, ns, w = h.shape
    h = jnp.transpose(h.reshape(g // 2, 2, ns, 2, w // 2), (1, 2, 0, 3, 4)).reshape(2, ns, g, w // 2)
    return h[0], h[1]


def _glu_body(x_ref, y_ref, w_ref, o_ref):
    d = x_ref.shape[1]
    a = jax.nn.gelu(y_ref[...]).astype(BF16)
    r = _dot(a, w_ref[...])
    o_ref[...] = x_ref[...] + r[:, :d] * jax.nn.sigmoid(r[:, d:])


def _glu_call(x, y, w):
    n, d = x.shape
    tm = _row_tile(n, ROW_TILE)
    row = pl.BlockSpec((tm, d), lambda i: (i, 0))
    return pl.pallas_call(
        _glu_body,
        grid=(n // tm,),
        in_specs=[row, row, _resident(w.shape)],
        out_specs=row,
        out_shape=jax.ShapeDtypeStruct((n, d), F32),
        compiler_params=_cparams("parallel"),
        name="glu",
    )(x, y, w)


def _ffn_body(x_ref, g_ref, win_ref, wout_ref, o_ref):
    dff = wout_ref.shape[0]
    x = x_ref[...]
    xn = _rms(x, g_ref[...]).astype(BF16)
    acc = x
    for j in range(dff // FFN_CHUNK):
        lo = j * FFN_CHUNK
        gate = _dot(xn, win_ref[:, lo:lo + FFN_CHUNK])
        up = _dot(xn, win_ref[:, dff + lo:dff + lo + FFN_CHUNK])
        a = (jax.nn.silu(gate) * up).astype(BF16)
        acc = acc + _dot(a, wout_ref[lo:lo + FFN_CHUNK, :])
    o_ref[...] = acc


def _ffn_call(x, g, w_in, w_out):
    n, d = x.shape
    tm = _row_tile(n, ROW_TILE)
    row = pl.BlockSpec((tm, d), lambda i: (i, 0))
    return pl.pallas_call(
        _ffn_body,
        grid=(n // tm,),
        in_specs=[row, _resident((1, d)), _resident(w_in.shape), _resident(w_out.shape)],
        out_specs=row,
        out_shape=jax.ShapeDtypeStruct((n, d), F32),
        compiler_params=_cparams("parallel"),
        name="ffn",
    )(x, g.reshape(1, d), w_in, w_out)


def _kv_body(x_ref, g_ref, w_ref, kn_ref, e_ref, et_ref, k_ref, v_ref):
    kvw = k_ref.shape[1]
    xn = _rms(x_ref[...], g_ref[...]).astype(BF16)
    r = _dot(xn, w_ref[...])
    k_ref[...] = _head_rms(r[:, :kvw], e_ref[...], et_ref[...], kn_ref[...])
    v_ref[...] = r[:, kvw:]


def _kv_call(x, g, w, k_norm):
    n, d = x.shape
    kvw = w.shape[1] // 2
    tm = _row_tile(n, ROW_TILE)
    e, et = _segment_mats(kvw // HEAD_DIM)
    out = pl.BlockSpec((tm, kvw), lambda i: (i, 0))
    return pl.pallas_call(
        _kv_body,
        grid=(n // tm,),
        in_specs=[pl.BlockSpec((tm, d), lambda i: (i, 0)), _resident((1, d)), _resident(w.shape),
                  _resident((1, kvw)), _resident(e.shape), _resident(et.shape)],
        out_specs=[out, out],
        out_shape=[jax.ShapeDtypeStruct((n, kvw), F32)] * 2,
        compiler_params=_cparams("parallel"),
        name="kv",
    )(x, g.reshape(1, d), w, jnp.tile(k_norm, kvw // HEAD_DIM).reshape(1, kvw), e, et)


def _attn_body(chunk, shared, x_ref, g_ref, wq_ref, gain_ref, wo_ref, sink_ref, kown_ref, vown_ref, kleft_ref,
               vleft_ref, kmeta_ref, vmeta_ref, bband_ref, bmeta_ref, e_ref, et_ref, o_ref,
               q_scr, kd_scr, vt_scr, lg_scr, att_scr):
    rows, d = x_ref.shape
    nchunks = rows // chunk
    n_heads = d // HEAD_DIM
    q_per_kv = n_heads // N_KV_HEADS
    band = WINDOW + chunk
    qlanes = q_per_kv * chunk
    kvw = N_KV_HEADS * HEAD_DIM

    x = x_ref[...]
    q = _dot(_rms(x, g_ref[...]).astype(BF16), wq_ref[...])
    q_scr[...] = _head_rms(q, e_ref[...], et_ref[...], 1.0)

    gain = gain_ref[...]
    low = lax.broadcasted_iota(jnp.int32, (1, LANES), 1) < HEAD_DIM

    def dup(kf):
        out = []
        for kv in range(N_KV_HEADS):
            t = kf[:, (kv // 2) * LANES:(kv // 2 + 1) * LANES]
            r = pltpu.roll(t, HEAD_DIM, 1)
            out.append((jnp.where(low, t, r) if kv % 2 == 0 else jnp.where(low, r, t)).astype(BF16))
        return out

    def put_keys(row0, kf, vf):
        for kv, t in enumerate(dup(kf * gain)):
            kd_scr[kv, row0:row0 + LANES, :] = t
        vt_scr[:, row0:row0 + LANES] = vf.T.astype(BF16)

    km = dup(kmeta_ref[...] * gain)
    vmt = vmeta_ref[...].astype(BF16)
    if shared:
        put_keys(0, kleft_ref[...], vleft_ref[...])
        for t in range(rows // LANES):
            put_keys(WINDOW + t * LANES, kown_ref[t * LANES:(t + 1) * LANES, :], vown_ref[t * LANES:(t + 1) * LANES, :])
        first_chunk = pl.program_id(1) * nchunks
    else:
        zpad = jnp.zeros((LANES - chunk, kvw), F32)
        for j in range(nchunks):
            put_keys(j * 2 * LANES, kleft_ref[j], vleft_ref[j])
            put_keys(j * 2 * LANES + LANES, jnp.concatenate([kown_ref[j * chunk:(j + 1) * chunk, :], zpad], axis=0),
                     jnp.concatenate([vown_ref[j * chunk:(j + 1) * chunk, :], zpad], axis=0))

    for j in range(nchunks):
        r0 = j * chunk
        if shared:
            base = r0
            blocks = [(base + i * chunk, chunk, i * chunk, WINDOW // chunk - i) for i in range(band // chunk)]
        else:
            base = j * 2 * LANES
            blocks = [(base, WINDOW, 0, None), (base + WINDOW, chunk, WINDOW, None)]
        tile0 = (base // LANES) * LANES
        bm = bmeta_ref[j] if shared else bmeta_ref[0]
        for pair in range(N_KV_HEADS // 2):
            o_pair = []
            for kv in (2 * pair, 2 * pair + 1):
                qx = []
                for gq in range(q_per_kv):
                    h = kv * q_per_kv + gq
                    t = q_scr[r0:r0 + chunk, (h // 2) * LANES:(h // 2 + 1) * LANES]
                    qx.append(jnp.where(low if h % 2 == 0 else jnp.logical_not(low), t, 0.0))
                qx = jnp.concatenate(qx, axis=0).astype(BF16)
                s = sink_ref[kv]
                run = jnp.full((SUBLANES, qlanes), NEG, F32)
                for (k0, nk, b0, back) in blocks:
                    lg = _dot_nt(kd_scr[kv, k0:k0 + nk, :], qx) + bband_ref[kv, b0:b0 + nk, :]
                    if back:
                        lg = jnp.where(first_chunk + j >= back, lg, NEG)
                    lg_scr[kv, b0:b0 + nk, :] = lg
                    run = jnp.maximum(run, jnp.max(lg.reshape(nk // SUBLANES, SUBLANES, qlanes), axis=0))
                lgm = _dot_nt(km[kv], qx) + bm[kv]
                run = jnp.maximum(run, jnp.max(lgm.reshape(N_META // SUBLANES, SUBLANES, qlanes), axis=0))
                mx = jnp.maximum(jnp.max(run, axis=0, keepdims=True), s)
                pm = jnp.exp(lgm - mx)
                den = jnp.sum(pm, axis=0, keepdims=True) + jnp.exp(s - mx)
                ot = _dot(vmt[kv * HEAD_DIM:(kv + 1) * HEAD_DIM, :], pm.astype(BF16))
                parts = {}
                for (k0, nk, b0, back) in blocks:
                    p = jnp.exp(lg_scr[kv, b0:b0 + nk, :] - mx)
                    den = den + jnp.sum(p, axis=0, keepdims=True)
                    parts[k0] = p.astype(BF16)
                for t0 in (tile0, tile0 + LANES):
                    segs, pos = [], t0
                    while pos < t0 + LANES:
                        if pos in parts:
                            segs.append(parts[pos])
                            pos += parts[pos].shape[0]
                        else:
                            nxt = min([k0 for k0 in parts if k0 > pos] + [t0 + LANES])
                            segs.append(jnp.zeros((nxt - pos, qlanes), BF16))
                            pos = nxt
                    pt = segs[0] if len(segs) == 1 else jnp.concatenate(segs, axis=0)
                    ot = ot + _dot(vt_scr[kv * HEAD_DIM:(kv + 1) * HEAD_DIM, t0:t0 + LANES], pt)
                o_pair.append(ot / den)
            o2 = jnp.concatenate(o_pair, axis=0).T
            for gq in range(q_per_kv):
                tile = pair * q_per_kv + gq
                att_scr[r0:r0 + chunk, tile * LANES:(tile + 1) * LANES] = o2[gq * chunk:(gq + 1) * chunk, :]

    o_ref[...] = x + _dot(att_scr[...].astype(BF16), wo_ref[...])


def _attn_call(x, g, w_q, q_norm, w_o, sinks, k_own, v_own, k_left, v_left, k_meta, v_meta, b_band, b_meta,
               chunk, shared, nseq, tiles_per_seq, chunks_per_tile):
    n, d = x.shape
    kvw = k_own.shape[1]
    n_heads = d // HEAD_DIM
    q_per_kv = n_heads // N_KV_HEADS
    rows = chunks_per_tile * chunk
    band = WINDOW + chunk
    qlanes = q_per_kv * chunk
    assert rows % LANES == 0 and LANES % chunk == 0
    key_rows = WINDOW + rows if shared else chunks_per_tile * 2 * LANES
    e, et = _segment_mats(n_heads)
    gain = jnp.tile(q_norm * (HEAD_DIM ** -0.5), N_KV_HEADS).reshape(1, kvw)
    sink_row = jnp.broadcast_to(sinks.reshape(N_KV_HEADS, 1, q_per_kv, 1),
                                (N_KV_HEADS, 1, q_per_kv, chunk)).reshape(N_KV_HEADS, 1, qlanes)
    w_o = jnp.transpose(w_o.reshape(N_KV_HEADS // 2, 2, q_per_kv, HEAD_DIM, d), (0, 2, 1, 3, 4)).reshape(d, d)
    row_idx = lambda b, i: (b * tiles_per_seq + i, 0)
    if shared:
        halo_blocks = rows // WINDOW
        left_spec = pl.BlockSpec(
            (WINDOW, kvw), lambda b, i: (jnp.maximum(b * tiles_per_seq * halo_blocks + i * halo_blocks - 1, 0), 0))
        bmeta_spec = pl.BlockSpec((chunks_per_tile,) + b_meta.shape[1:], lambda b, i: (i, 0, 0, 0))
    else:
        left_spec = pl.BlockSpec((chunks_per_tile, WINDOW, kvw), lambda b, i: (b * tiles_per_seq + i, 0, 0))
        bmeta_spec = _resident(b_meta.shape)
    return pl.pallas_call(
        functools.partial(_attn_body, chunk, shared),
        grid=(nseq, tiles_per_seq),
        in_specs=[
            pl.BlockSpec((rows, d), row_idx), _resident((1, d)), _resident(w_q.shape), _resident((1, kvw)),
            _resident(w_o.shape), _resident(sink_row.shape),
            pl.BlockSpec((rows, kvw), row_idx), pl.BlockSpec((rows, kvw), row_idx), left_spec, left_spec,
            _resident(k_meta.shape), _resident((kvw, N_META)), _resident(b_band.shape), bmeta_spec,
            _resident(e.shape), _resident(et.shape),
        ],
        out_specs=pl.BlockSpec((rows, d), row_idx),
        out_shape=jax.ShapeDtypeStruct((n, d), F32),
        scratch_shapes=[pltpu.VMEM((rows, d), F32), pltpu.VMEM((N_KV_HEADS, key_rows, LANES), BF16),
                        pltpu.VMEM((kvw, key_rows), BF16), pltpu.VMEM((N_KV_HEADS, band, qlanes), F32),
                        pltpu.VMEM((rows, d), F32)],
        compiler_params=_cparams("parallel", "arbitrary"),
        name="attn_prompt" if shared else "attn_sample",
    )(x, g.reshape(1, d), w_q, gain, w_o, sink_row, k_own, v_own, k_left, v_left, k_meta, v_meta.T, b_band, b_meta,
      e, et)


def _rel_bucket(rel):
    half = N_BUCKETS // 2
    max_exact = half // 2
    n = jnp.abs(rel)
    nf = jnp.maximum(n, 1).astype(F32)
    large = max_exact + (jnp.log(nf / max_exact) / math.log(MAX_DISTANCE / max_exact)
                         * (half - max_exact)).astype(jnp.int32)
    large = jnp.minimum(large, half - 1)
    return jnp.where(rel > 0, half, 0) + jnp.where(n < max_exact, n, large)


def _band_bias(rel_bias, chunk):
    n_heads = rel_bias.shape[1]
    q_per_kv = n_heads // N_KV_HEADS
    band = WINDOW + chunk
    rel = jnp.arange(band)[None, :] - WINDOW - jnp.arange(chunk)[:, None]
    b = rel_bias[_rel_bucket(rel)].reshape(chunk, band, N_KV_HEADS, q_per_kv)
    return jnp.transpose(b, (2, 1, 3, 0)).reshape(N_KV_HEADS, band, q_per_kv * chunk)


def _meta_bias(t0, count, rel_bias, chunk):
    n_heads = rel_bias.shape[1]
    q_per_kv = n_heads // N_KV_HEADS
    dist = jnp.arange(t0 + 1, t0 + count + N_META + 1)
    tab = rel_bias[_rel_bucket(-dist)].T
    b = jnp.stack([tab[:, N_META - 1 - m:N_META - 1 - m + count] for m in range(N_META)])
    b = b.reshape(N_META, N_KV_HEADS, q_per_kv, count // chunk, chunk)
    return jnp.transpose(b, (3, 1, 0, 2, 4)).reshape(count // chunk, N_KV_HEADS, N_META, q_per_kv * chunk)


def kernel(x_prompt, x_sample, state_ssm_re, state_ssm_im, cache_k, cache_v, meta_tokens, norm_mix, norm_ffn,
           ssm_a_re, ssm_a_im, ssm_log_dt, ssm_b_re, ssm_b_im, ssm_c_re, ssm_c_im, ssm_d, w_glu, w_ffn_in,
           w_ffn_out, norm_kv, w_kv, k_norm, w_q, q_norm, attn_sinks, w_o, rel_bias):
    bsz, seq, d = x_prompt.shape
    dbsz, dseq, _ = x_sample.shape
    n_a = ssm_a_re.shape[0]
    n_b = w_q.shape[0]
    n_groups, n_state = ssm_a_re.shape[1:]
    kvw = N_KV_HEADS * HEAD_DIM

    w_glu_b = w_glu.astype(BF16)
    w_in_b = w_ffn_in.astype(BF16)
    w_out_b = w_ffn_out.astype(BF16)
    w_kv_b = w_kv.astype(BF16)
    w_q_b = w_q.astype(BF16)
    w_o_b = w_o.astype(BF16)
    tabs = [_s5_tables(ssm_a_re[l], ssm_a_im[l], ssm_log_dt[l], ssm_b_re[l], ssm_b_im[l], ssm_c_re[l],
                       ssm_c_im[l], ssm_d[l]) for l in range(n_a)]

    n_s = dbsz * dseq
    n_small = n_s + dseq
    xs = jnp.concatenate([x_sample.reshape(n_s, d), jnp.zeros((dseq - N_META, d), F32), meta_tokens], axis=0)
    zero_state = jnp.zeros((1, n_groups, n_state), F32)
    s_re, s_im, meta_h = [], [], []
    for l in range(n_a):
        u = _rms_call(xs, norm_mix[l])
        h0 = _state_to_cols(jnp.concatenate([state_ssm_re[l], zero_state], axis=0),
                            jnp.concatenate([state_ssm_im[l], zero_state], axis=0))
        y, hf = _s5_call(u.reshape(1, n_small, d), tabs[l], h0, spb=dbsz + 1, cps=dseq // S5_T, tokens=n_small)
        xs = _glu_call(xs, y.reshape(n_small, d), w_glu_b[l])
        xs = _ffn_call(xs, norm_ffn[l], w_in_b[l], w_out_b[l])
        hr, hi = _cols_to_state(hf)
        s_re.append(hr[:dbsz])
        s_im.append(hi[:dbsz])
        meta_h.append((hr[dbsz:], hi[dbsz:]))
    k_small, v_small = _kv_call(xs, norm_kv, w_kv_b, k_norm)
    k_s, v_s = k_small[:n_s], v_small[:n_s]
    k_meta, v_meta = k_small[n_small - N_META:], v_small[n_small - N_META:]

    bband_s = _band_bias(rel_bias, dseq)
    bmeta_s = _meta_bias(PAST_LEN, dseq, rel_bias, dseq)
    xq = xs[:n_s]
    ck = cache_k.reshape(dbsz, WINDOW, kvw)
    cv = cache_v.reshape(dbsz, WINDOW, kvw)
    for j in range(n_b):
        l = n_a + j
        xq = _attn_call(xq, norm_mix[l], w_q_b[j], q_norm[j], w_o_b[j], attn_sinks[j], k_s, v_s, ck, cv, k_meta,
                        v_meta, bband_s, bmeta_s, chunk=dseq, shared=False, nseq=1,
                        tiles_per_seq=dbsz // ATTN_SEQS_SAMPLE, chunks_per_tile=ATTN_SEQS_SAMPLE)
        xq = _ffn_call(xq, norm_ffn[l], w_in_b[l], w_out_b[l])
    y_sample = xq.reshape(dbsz, dseq, d)

    xp = x_prompt.reshape(bsz * seq, d)
    s5_tokens = min(S5_TOKENS, seq)
    p_re, p_im = [], []
    for l in range(n_a):
        u = _rms_call(xp, norm_mix[l])
        h0 = _state_to_cols(jnp.broadcast_to(meta_h[l][0], (bsz, n_groups, n_state)),
                            jnp.broadcast_to(meta_h[l][1], (bsz, n_groups, n_state)))
        y, hf = _s5_call(u.reshape(bsz, seq, d), tabs[l], h0, spb=1, cps=s5_tokens // S5_T, tokens=s5_tokens)
        xp = _glu_call(xp, y.reshape(bsz * seq, d), w_glu_b[l])
        xp = _ffn_call(xp, norm_ffn[l], w_in_b[l], w_out_b[l])
        hr, hi = _cols_to_state(hf)
        p_re.append(hr)
        p_im.append(hi)
    k_p, v_p = _kv_call(xp, norm_kv, w_kv_b, k_norm)

    bband_p = _band_bias(rel_bias, CHUNK)
    bmeta_p = _meta_bias(0, seq, rel_bias, CHUNK)
    tile_rows = ATTN_CHUNKS_PROMPT * CHUNK
    for j in range(n_b):
        l = n_a + j
        xp = _attn_call(xp, norm_mix[l], w_q_b[j], q_norm[j], w_o_b[j], attn_sinks[j], k_p, v_p, k_p, v_p, k_meta,
                        v_meta, bband_p, bmeta_p, chunk=CHUNK, shared=True, nseq=bsz,
                        tiles_per_seq=seq // tile_rows, chunks_per_tile=ATTN_CHUNKS_PROMPT)
        xp = _ffn_call(xp, norm_ffn[l], w_in_b[l], w_out_b[l])
    y_prompt = xp.reshape(bsz, seq, d)

    k_p4 = k_p.reshape(bsz, seq, N_KV_HEADS, HEAD_DIM)
    v_p4 = v_p.reshape(bsz, seq, N_KV_HEADS, HEAD_DIM)
    return (y_prompt, y_sample, jnp.stack(p_re), jnp.stack(p_im), k_p4[:, -WINDOW:], v_p4[:, -WINDOW:],
            jnp.stack(s_re), jnp.stack(s_im), k_s.reshape(dbsz, dseq, N_KV_HEADS, HEAD_DIM),
            v_s.reshape(dbsz, dseq, N_KV_HEADS, HEAD_DIM))
```

```python
import functools
import math

import jax
import jax.numpy as jnp
from jax import lax
from jax.experimental import pallas as pl
from jax.experimental.pallas import tpu as pltpu

F32 = jnp.float32
BF16 = jnp.bfloat16

CHUNK = 64
N_META = 16
SSM_GROUP = 16
SSM_STATE = 64
HEAD_DIM = 64
N_KV_HEADS = 4
WINDOW = 128
N_BUCKETS = 32
MAX_DISTANCE = 128
PAST_LEN = 1024
EPS = 1e-6
NEG = -1e30

LANES = 128
SUBLANES = 8
VMEM_LIMIT_BYTES = 56 * 1024 * 1024

S5_T = 16
S5_PAIR = 2 * SSM_GROUP
S5_PAIRS = LANES // S5_PAIR
S5_TOKENS = 2048
S5_SCAN_UNROLL = 4
ROW_TILE = 512
FFN_CHUNK = 256
ATTN_CHUNKS_PROMPT = 8
ATTN_SEQS_SAMPLE = 8


def _row_tile(n, target):
    if n <= target:
        return n
    best = None
    for t in range(SUBLANES, target + 1, SUBLANES):
        if n % t == 0:
            best = t
    assert best is not None, n
    return best


def _cparams(*sem):
    return pltpu.CompilerParams(dimension_semantics=sem, vmem_limit_bytes=VMEM_LIMIT_BYTES)


def _resident(shape):
    nd = len(shape)
    return pl.BlockSpec(shape, lambda *_: (0,) * nd, pipeline_mode=pl.Buffered(1))


def _rms(x, g):
    ms = jnp.mean(x * x, axis=-1, keepdims=True)
    return (x * lax.rsqrt(ms + EPS)) * g


def _dot(a, b):
    return jnp.dot(a, b, preferred_element_type=F32)


def _dot_nt(a, b):
    return lax.dot_general(a, b, (((1,), (1,)), ((), ())), preferred_element_type=F32)


def _head_rms(x, e, et, gain):
    ss = _dot((x * x).astype(BF16), e)
    rinv = lax.rsqrt(ss * (1.0 / HEAD_DIM) + EPS)
    hi = rinv.astype(BF16)
    lo = (rinv - hi.astype(F32)).astype(BF16)
    rb = _dot(hi, et) + _dot(lo, et)
    return (x * rb) * gain


def _segment_mats(n_heads):
    idx = jnp.arange(n_heads * HEAD_DIM) // HEAD_DIM
    e = (idx[:, None] == jnp.arange(LANES)[None, :]).astype(BF16)
    return e, e.T


def _rms_body(x_ref, g_ref, o_ref):
    o_ref[...] = _rms(x_ref[...], g_ref[...])


def _rms_call(x, g):
    n, d = x.shape
    tm = _row_tile(n, 1024)
    return pl.pallas_call(
        _rms_body,
        grid=(n // tm,),
        in_specs=[pl.BlockSpec((tm, d), lambda i: (i, 0)), _resident((1, d))],
        out_specs=pl.BlockSpec((tm, d), lambda i: (i, 0)),
        out_shape=jax.ShapeDtypeStruct((n, d), F32),
        compiler_params=_cparams("parallel"),
        name="rms",
    )(x, g.reshape(1, d))


def _xpose4(a, lo64, lo32):
    a0, a1, a2, a3 = a
    r02 = pltpu.roll(jnp.where(lo64, a2, a0), 64, 1)
    r13 = pltpu.roll(jnp.where(lo64, a3, a1), 64, 1)
    b0 = jnp.where(lo64, a0, r02)
    b2 = jnp.where(lo64, r02, a2)
    b1 = jnp.where(lo64, a1, r13)
    b3 = jnp.where(lo64, r13, a3)
    return (jnp.where(lo32, b0, pltpu.roll(b1, 32, 1)), jnp.where(lo32, pltpu.roll(b0, 96, 1), b1),
            jnp.where(lo32, b2, pltpu.roll(b3, 32, 1)), jnp.where(lo32, pltpu.roll(b2, 96, 1), b3))


def _s5_body(spb, cps, u_ref, k_ref, xb_ref, xc_ref, pw_ref, h0_ref, y_ref, hf_ref,
             s_scr, hin_scr, h_scr, m_scr, wo_scr, wit_scr):
    nb = u_ref.shape[0]
    t = S5_T
    rows_b = spb * cps
    nc = nb * rows_b
    ns = nb * spb
    ncol = 2 * S5_PAIRS
    width = t * S5_PAIR

    @pl.when(pl.program_id(1) == 0)
    def _():
        h_scr[...] = h0_ref[...]
        strip_lane = lax.broadcasted_iota(jnp.int32, (S5_PAIR, width), 1)
        for p in range(S5_PAIRS):
            strip = k_ref[p]
            for s in range(t):
                blk = strip if s == 0 else jnp.where(strip_lane >= S5_PAIR * s, pltpu.roll(strip, S5_PAIR * s, 1), 0.0)
                m_scr[p, S5_PAIR * s:S5_PAIR * (s + 1), :] = blk.astype(BF16)
            br, bi = xb_ref[p, 0], xb_ref[p, 1]
            cr, ci = xc_ref[p, 0], xc_ref[p, 1]
            for s in range(t):
                rs = slice(S5_PAIR * s, S5_PAIR * (s + 1))
                ar, ai = pw_ref[p, 0, t - 1 - s:t - s, :], pw_ref[p, 1, t - 1 - s:t - s, :]
                wo_scr[p, rs, 0:LANES] = (br * ar - bi * ai).astype(BF16)
                wo_scr[p, rs, LANES:2 * LANES] = (br * ai + bi * ar).astype(BF16)
                ar, ai = pw_ref[p, 0, s + 1:s + 2, :], pw_ref[p, 1, s + 1:s + 2, :]
                wit_scr[p, rs, 0:LANES] = (cr * ar - ci * ai).astype(BF16)
                wit_scr[p, rs, LANES:2 * LANES] = (-(cr * ai + ci * ar)).astype(BF16)

    lane = lax.broadcasted_iota(jnp.int32, (nc, LANES), 1)
    lo64 = lane < 64
    lo32 = (lane % 64) < 32

    v = []
    for tok in range(t):
        parts = [u_ref[b, pl.ds(tok, rows_b, stride=t), :] for b in range(nb)]
        v.append(parts[0] if nb == 1 else jnp.concatenate(parts, axis=0))
    quads = [_xpose4(v[4 * tq:4 * tq + 4], lo64, lo32) for tq in range(t // 4)]
    u_p = [jnp.concatenate([quads[tq][p] for tq in range(t // 4)], axis=1).astype(BF16) for p in range(S5_PAIRS)]

    for p in range(S5_PAIRS):
        s = _dot(u_p[p], wo_scr[p])
        s_scr[2 * p, 0:nc, :] = s[:, :LANES]
        s_scr[2 * p + 1, 0:nc, :] = s[:, LANES:]

    def step(k, carry):
        new = []
        for p in range(S5_PAIRS):
            cr, ci = carry[2 * p], carry[2 * p + 1]
            hin_scr[2 * p, pl.ds(k, ns, stride=cps), :] = cr
            hin_scr[2 * p + 1, pl.ds(k, ns, stride=cps), :] = ci
            ar = pw_ref[p, 0, t:t + 1, :]
            ai = pw_ref[p, 1, t:t + 1, :]
            tr = s_scr[2 * p, pl.ds(k, ns, stride=cps), :]
            ti = s_scr[2 * p + 1, pl.ds(k, ns, stride=cps), :]
            new += [ar * cr - ai * ci + tr, ar * ci + ai * cr + ti]
        return tuple(new)

    carry = lax.fori_loop(0, cps, step, tuple(h_scr[c] for c in range(ncol)), unroll=min(cps, S5_SCAN_UNROLL))
    for c in range(ncol):
        h_scr[c] = carry[c]
        hf_ref[c] = carry[c]

    y_p = []
    for p in range(S5_PAIRS):
        h = jnp.concatenate([hin_scr[2 * p, 0:nc, :], hin_scr[2 * p + 1, 0:nc, :]], axis=1).astype(BF16)
        y_p.append(_dot(u_p[p], m_scr[p]) + _dot_nt(h, wit_scr[p]))
    for tq in range(t // 4):
        w = _xpose4([y_p[p][:, LANES * tq:LANES * (tq + 1)] for p in range(S5_PAIRS)], lo64, lo32)
        for i in range(4):
            for b in range(nb):
                y_ref[b, pl.ds(4 * tq + i, rows_b, stride=t), :] = w[i][b * rows_b:(b + 1) * rows_b]


def _s5_call(u, tabs, h0, spb, cps, tokens):
    strip, xb, xc, pw = tabs
    nb, length, d = u.shape
    assert tokens == spb * cps * S5_T and length % tokens == 0
    ns = nb * spb
    ncol = 2 * S5_PAIRS
    nc_pad = -(-(nb * spb * cps) // SUBLANES) * SUBLANES
    width = S5_T * S5_PAIR
    blk = pl.BlockSpec((nb, tokens, LANES), lambda b, r: (0, r, b))
    state = pl.BlockSpec((ncol, ns, LANES), lambda b, r: (b, 0, 0))
    return pl.pallas_call(
        functools.partial(_s5_body, spb, cps),
        grid=(d // LANES, length // tokens),
        in_specs=[
            blk,
            pl.BlockSpec((S5_PAIRS, S5_PAIR, width), lambda b, r: (b, 0, 0)),
            pl.BlockSpec((S5_PAIRS, 2, S5_PAIR, LANES), lambda b, r: (b, 0, 0, 0)),
            pl.BlockSpec((S5_PAIRS, 2, S5_PAIR, LANES), lambda b, r: (b, 0, 0, 0)),
            pl.BlockSpec((S5_PAIRS, 2, S5_T + 1, LANES), lambda b, r: (b, 0, 0, 0)),
            state,
        ],
        out_specs=[blk, state],
        out_shape=[jax.ShapeDtypeStruct(u.shape, F32), jax.ShapeDtypeStruct(h0.shape, F32)],
        scratch_shapes=[pltpu.VMEM((ncol, nc_pad, LANES), F32), pltpu.VMEM((ncol, nc_pad, LANES), F32),
                        pltpu.VMEM((ncol, ns, LANES), F32), pltpu.VMEM((S5_PAIRS, width, width), BF16),
                        pltpu.VMEM((S5_PAIRS, width, 2 * LANES), BF16), pltpu.VMEM((S5_PAIRS, width, 2 * LANES), BF16)],
        compiler_params=_cparams("parallel", "arbitrary"),
        name="s5",
    )(u, strip, xb, xc, pw, h0)


def _s5_tables(a_re, a_im, log_dt, b_re, b_im, c_re, c_im, d_skip):
    t = S5_T
    g, p = a_re.shape
    c = b_re.shape[-1]
    hp = lax.Precision.HIGHEST
    dt = jnp.exp(log_dt)[:, None]
    mag = jnp.exp(a_re * dt)
    ang = a_im * dt
    ar = mag * jnp.cos(ang)
    ai = mag * jnp.sin(ang)
    num_re = ar - 1.0
    num_im = ai
    inv = 1.0 / (a_re * a_re + a_im * a_im)
    f_re = (num_re * a_re + num_im * a_im) * inv
    f_im = (num_im * a_re - num_re * a_im) * inv
    bb_re = f_re[..., None] * b_re - f_im[..., None] * b_im
    bb_im = f_re[..., None] * b_im + f_im[..., None] * b_re
    pr = jnp.ones((1, g, p), F32)
    pi = jnp.zeros((1, g, p), F32)
    sr, si = ar, ai
    while pr.shape[0] < t + 1:
        pr, pi = (jnp.concatenate([pr, pr * sr - pi * si]), jnp.concatenate([pi, pr * si + pi * sr]))
        sr, si = sr * sr - si * si, 2.0 * sr * si
    pr = pr[:t + 1]
    pi = pi[:t + 1]
    eye2 = jnp.eye(2, dtype=F32)
    ca_r = c_re[None] * pr[:t, :, None, :] - c_im[None] * pi[:t, :, None, :]
    ca_i = c_re[None] * pi[:t, :, None, :] + c_im[None] * pr[:t, :, None, :]
    k = (jnp.einsum('tgdp,gpc->tgdc', ca_r, bb_re, precision=hp)
         - jnp.einsum('tgdp,gpc->tgdc', ca_i, bb_im, precision=hp))
    k = k.at[0].add(d_skip[:, None, :] * jnp.eye(c, dtype=F32)[None])
    strip = jnp.transpose(k.reshape(t, g // 2, 2, c, c), (1, 2, 4, 0, 3))
    strip = (strip[:, :, :, :, None, :] * eye2[None, :, None, None, :, None]).reshape(g // 2, 2 * c, 2 * t * c)
    def pad_pair(x):
        x = x.reshape(g // 2, 2, c, p)
        return (x[:, :, :, None, :] * eye2[None, :, None, :, None]).reshape(g // 2, 2 * c, 2 * p)

    xb = jnp.stack([pad_pair(jnp.transpose(bb_re, (0, 2, 1))), pad_pair(jnp.transpose(bb_im, (0, 2, 1)))], axis=1)
    xc = jnp.stack([pad_pair(c_re), pad_pair(c_im)], axis=1)
    pw = jnp.stack([jnp.transpose(pr.reshape(t + 1, g // 2, 2 * p), (1, 0, 2)),
                    jnp.transpose(pi.reshape(t + 1, g // 2, 2 * p), (1, 0, 2))], axis=1)
    return strip, xb, xc, pw


def _state_to_cols(h_re, h_im):
    ns, g, p = h_re.shape
    f = lambda h: jnp.transpose(h.reshape(ns, g // 2, 2 * p), (1, 0, 2))
    return jnp.stack([f(h_re), f(h_im)], axis=1).reshape(g, ns, 2 * p)


def _cols_to_state(h):
    g, ns, w = h.shape
    h = jnp.transpose(h.reshape(g // 2, 2, ns, 2, w // 2), (1, 2, 0, 3, 4)).reshape(2, ns, g, w // 2)
    return h[0], h[1]


def _glu_body(x_ref, y_ref, w_ref, o_ref):
    d = x_ref.shape[1]
    a = jax.nn.gelu(y_ref[...]).astype(BF16)
    r = _dot(a, w_ref[...])
    o_ref[...] = x_ref[...] + r[:, :d] * jax.nn.sigmoid(r[:, d:])


def _glu_call(x, y, w):
    n, d = x.shape
    tm = _row_tile(n, ROW_TILE)
    row = pl.BlockSpec((tm, d), lambda i: (i, 0))
    return pl.pallas_call(
        _glu_body,
        grid=(n // tm,),
        in_specs=[row, row, _resident(w.shape)],
        out_specs=row,
        out_shape=jax.ShapeDtypeStruct((n, d), F32),
        compiler_params=_cparams("parallel"),
        name="glu",
    )(x, y, w)


def _ffn_body(x_ref, g_ref, win_ref, wout_ref, o_ref):
    dff = wout_ref.shape[0]
    x = x_ref[...]
    xn = _rms(x, g_ref[...]).astype(BF16)
    acc = x
    for j in range(dff // FFN_CHUNK):
        lo = j * FFN_CHUNK
        gate = _dot(xn, win_ref[:, lo:lo + FFN_CHUNK])
        up = _dot(xn, win_ref[:, dff + lo:dff + lo + FFN_CHUNK])
        a = (jax.nn.silu(gate) * up).astype(BF16)
        acc = acc + _dot(a, wout_ref[lo:lo + FFN_CHUNK, :])
    o_ref[...] = acc


def _ffn_call(x, g, w_in, w_out):
    n, d = x.shape
    tm = _row_tile(n, ROW_TILE)
    row = pl.BlockSpec((tm, d), lambda i: (i, 0))
    return pl.pallas_call(
        _ffn_body,
        grid=(n // tm,),
        in_specs=[row, _resident((1, d)), _resident(w_in.shape), _resident(w_out.shape)],
        out_specs=row,
        out_shape=jax.ShapeDtypeStruct((n, d), F32),
        compiler_params=_cparams("parallel"),
        name="ffn",
    )(x, g.reshape(1, d), w_in, w_out)


def _kv_body(x_ref, g_ref, w_ref, kn_ref, e_ref, et_ref, k_ref, v_ref):
    kvw = k_ref.shape[1]
    xn = _rms(x_ref[...], g_ref[...]).astype(BF16)
    r = _dot(xn, w_ref[...])
    k_ref[...] = _head_rms(r[:, :kvw], e_ref[...], et_ref[...], kn_ref[...])
    v_ref[...] = r[:, kvw:]


def _kv_call(x, g, w, k_norm):
    n, d = x.shape
    kvw = w.shape[1] // 2
    tm = _row_tile(n, ROW_TILE)
    e, et = _segment_mats(kvw // HEAD_DIM)
    out = pl.BlockSpec((tm, kvw), lambda i: (i, 0))
    return pl.pallas_call(
        _kv_body,
        grid=(n // tm,),
        in_specs=[pl.BlockSpec((tm, d), lambda i: (i, 0)), _resident((1, d)), _resident(w.shape),
                  _resident((1, kvw)), _resident(e.shape), _resident(et.shape)],
        out_specs=[out, out],
        out_shape=[jax.ShapeDtypeStruct((n, kvw), F32)] * 2,
        compiler_params=_cparams("parallel"),
        name="kv",
    )(x, g.reshape(1, d), w, jnp.tile(k_norm, kvw // HEAD_DIM).reshape(1, kvw), e, et)


def _attn_body(chunk, shared, x_ref, g_ref, wq_ref, gain_ref, wo_ref, sink_ref, kown_ref, vown_ref, kleft_ref,
               vleft_ref, kmeta_ref, vmeta_ref, bband_ref, bmeta_ref, e_ref, et_ref, o_ref,
               q_scr, kd_scr, vt_scr, lg_scr, att_scr):
    rows, d = x_ref.shape
    nchunks = rows // chunk
    n_heads = d // HEAD_DIM
    q_per_kv = n_heads // N_KV_HEADS
    band = WINDOW + chunk
    qlanes = q_per_kv * chunk
    kvw = N_KV_HEADS * HEAD_DIM

    x = x_ref[...]
    q = _dot(_rms(x, g_ref[...]).astype(BF16), wq_ref[...])
    q_scr[...] = _head_rms(q, e_ref[...], et_ref[...], 1.0)

    gain = gain_ref[...]
    low = lax.broadcasted_iota(jnp.int32, (1, LANES), 1) < HEAD_DIM

    def dup(kf):
        out = []
        for kv in range(N_KV_HEADS):
            t = kf[:, (kv // 2) * LANES:(kv // 2 + 1) * LANES]
            r = pltpu.roll(t, HEAD_DIM, 1)
            out.append((jnp.where(low, t, r) if kv % 2 == 0 else jnp.where(low, r, t)).astype(BF16))
        return out

    def put_keys(row0, kf, vf):
        for kv, t in enumerate(dup(kf * gain)):
            kd_scr[kv, row0:row0 + LANES, :] = t
        vt_scr[:, row0:row0 + LANES] = vf.T.astype(BF16)

    km = dup(kmeta_ref[...] * gain)
    vmt = vmeta_ref[...].astype(BF16)
    if shared:
        put_keys(0, kleft_ref[...], vleft_ref[...])
        for t in range(rows // LANES):
            put_keys(WINDOW + t * LANES, kown_ref[t * LANES:(t + 1) * LANES, :], vown_ref[t * LANES:(t + 1) * LANES, :])
        first_chunk = pl.program_id(1) * nchunks
    else:
        zpad = jnp.zeros((LANES - chunk, kvw), F32)
        for j in range(nchunks):
            put_keys(j * 2 * LANES, kleft_ref[j], vleft_ref[j])
            put_keys(j * 2 * LANES + LANES, jnp.concatenate([kown_ref[j * chunk:(j + 1) * chunk, :], zpad], axis=0),
                     jnp.concatenate([vown_ref[j * chunk:(j + 1) * chunk, :], zpad], axis=0))

    for j in range(nchunks):
        r0 = j * chunk
        if shared:
            base = r0
            blocks = [(base + i * chunk, chunk, i * chunk, WINDOW // chunk - i) for i in range(band // chunk)]
        else:
            base = j * 2 * LANES
            blocks = [(base, WINDOW, 0, None), (base + WINDOW, chunk, WINDOW, None)]
        tile0 = (base // LANES) * LANES
        bm = bmeta_ref[j] if shared else bmeta_ref[0]
        for pair in range(N_KV_HEADS // 2):
            o_pair = []
            for kv in (2 * pair, 2 * pair + 1):
                qx = []
                for gq in range(q_per_kv):
                    h = kv * q_per_kv + gq
                    t = q_scr[r0:r0 + chunk, (h // 2) * LANES:(h // 2 + 1) * LANES]
                    qx.append(jnp.where(low if h % 2 == 0 else jnp.logical_not(low), t, 0.0))
                qx = jnp.concatenate(qx, axis=0).astype(BF16)
                s = sink_ref[kv]
                run = jnp.full((SUBLANES, qlanes), NEG, F32)
                for (k0, nk, b0, back) in blocks:
                    lg = _dot_nt(kd_scr[kv, k0:k0 + nk, :], qx) + bband_ref[kv, b0:b0 + nk, :]
                    if back:
                        lg = jnp.where(first_chunk + j >= back, lg, NEG)
                    lg_scr[kv, b0:b0 + nk, :] = lg
                    run = jnp.maximum(run, jnp.max(lg.reshape(nk // SUBLANES, SUBLANES, qlanes), axis=0))
                lgm = _dot_nt(km[kv], qx) + bm[kv]
                run = jnp.maximum(run, jnp.max(lgm.reshape(N_META // SUBLANES, SUBLANES, qlanes), axis=0))
                mx = jnp.maximum(jnp.max(run, axis=0, keepdims=True), s)
                pm = jnp.exp(lgm - mx)
                den = jnp.sum(pm, axis=0, keepdims=True) + jnp.exp(s - mx)
                ot = _dot(vmt[kv * HEAD_DIM:(kv + 1) * HEAD_DIM, :], pm.astype(BF16))
                parts = {}
                for (k0, nk, b0, back) in blocks:
                    p = jnp.exp(lg_scr[kv, b0:b0 + nk, :] - mx)
                    den = den + jnp.sum(p, axis=0, keepdims=True)
                    parts[k0] = p.astype(BF16)
                for t0 in (tile0, tile0 + LANES):
                    segs, pos = [], t0
                    while pos < t0 + LANES:
                        if pos in parts:
                            segs.append(parts[pos])
                            pos += parts[pos].shape[0]
                        else:
                            nxt = min([k0 for k0 in parts if k0 > pos] + [t0 + LANES])
                            segs.append(jnp.zeros((nxt - pos, qlanes), BF16))
                            pos = nxt
                    pt = segs[0] if len(segs) == 1 else jnp.concatenate(segs, axis=0)
                    ot = ot + _dot(vt_scr[kv * HEAD_DIM:(kv + 1) * HEAD_DIM, t0:t0 + LANES], pt)
                o_pair.append(ot / den)
            o2 = jnp.concatenate(o_pair, axis=0).T
            for gq in range(q_per_kv):
                tile = pair * q_per_kv + gq
                att_scr[r0:r0 + chunk, tile * LANES:(tile + 1) * LANES] = o2[gq * chunk:(gq + 1) * chunk, :]

    o_ref[...] = x + _dot(att_scr[...].astype(BF16), wo_ref[...])


def _attn_call(x, g, w_q, q_norm, w_o, sinks, k_own, v_own, k_left, v_left, k_meta, v_meta, b_band, b_meta,
               chunk, shared, nseq, tiles_per_seq, chunks_per_tile):
    n, d = x.shape
    kvw = k_own.shape[1]
    n_heads = d // HEAD_DIM
    q_per_kv = n_heads // N_KV_HEADS
    rows = chunks_per_tile * chunk
    band = WINDOW + chunk
    qlanes = q_per_kv * chunk
    assert rows % LANES == 0 and LANES % chunk == 0
    key_rows = WINDOW + rows if shared else chunks_per_tile * 2 * LANES
    e, et = _segment_mats(n_heads)
    gain = jnp.tile(q_norm * (HEAD_DIM ** -0.5), N_KV_HEADS).reshape(1, kvw)
    sink_row = jnp.broadcast_to(sinks.reshape(N_KV_HEADS, 1, q_per_kv, 1),
                                (N_KV_HEADS, 1, q_per_kv, chunk)).reshape(N_KV_HEADS, 1, qlanes)
    w_o = jnp.transpose(w_o.reshape(N_KV_HEADS // 2, 2, q_per_kv, HEAD_DIM, d), (0, 2, 1, 3, 4)).reshape(d, d)
    row_idx = lambda b, i: (b * tiles_per_seq + i, 0)
    if shared:
        halo_blocks = rows // WINDOW
        left_spec = pl.BlockSpec(
            (WINDOW, kvw), lambda b, i: (jnp.maximum(b * tiles_per_seq * halo_blocks + i * halo_blocks - 1, 0), 0))
        last_tile = b_meta.shape[0] // chunks_per_tile - 1
        bmeta_spec = pl.BlockSpec((chunks_per_tile,) + b_meta.shape[1:],
                                  lambda b, i: (jnp.minimum(i, last_tile), 0, 0, 0))
    else:
        left_spec = pl.BlockSpec((chunks_per_tile, WINDOW, kvw), lambda b, i: (b * tiles_per_seq + i, 0, 0))
        bmeta_spec = _resident(b_meta.shape)
    return pl.pallas_call(
        functools.partial(_attn_body, chunk, shared),
        grid=(nseq, tiles_per_seq),
        in_specs=[
            pl.BlockSpec((rows, d), row_idx), _resident((1, d)), _resident(w_q.shape), _resident((1, kvw)),
            _resident(w_o.shape), _resident(sink_row.shape),
            pl.BlockSpec((rows, kvw), row_idx), pl.BlockSpec((rows, kvw), row_idx), left_spec, left_spec,
            _resident(k_meta.shape), _resident((kvw, N_META)), _resident(b_band.shape), bmeta_spec,
            _resident(e.shape), _resident(et.shape),
        ],
        out_specs=pl.BlockSpec((rows, d), row_idx),
        out_shape=jax.ShapeDtypeStruct((n, d), F32),
        scratch_shapes=[pltpu.VMEM((rows, d), F32), pltpu.VMEM((N_KV_HEADS, key_rows, LANES), BF16),
                        pltpu.VMEM((kvw, key_rows), BF16), pltpu.VMEM((N_KV_HEADS, band, qlanes), F32),
                        pltpu.VMEM((rows, d), F32)],
        compiler_params=_cparams("parallel", "arbitrary"),
        name="attn_prompt" if shared else "attn_sample",
    )(x, g.reshape(1, d), w_q, gain, w_o, sink_row, k_own, v_own, k_left, v_left, k_meta, v_meta.T, b_band, b_meta,
      e, et)


def _rel_bucket(rel):
    half = N_BUCKETS // 2
    max_exact = half // 2
    n = jnp.abs(rel)
    nf = jnp.maximum(n, 1).astype(F32)
    large = max_exact + (jnp.log(nf / max_exact) / math.log(MAX_DISTANCE / max_exact)
                         * (half - max_exact)).astype(jnp.int32)
    large = jnp.minimum(large, half - 1)
    return jnp.where(rel > 0, half, 0) + jnp.where(n < max_exact, n, large)


def _band_bias(rel_bias, chunk):
    n_heads = rel_bias.shape[1]
    q_per_kv = n_heads // N_KV_HEADS
    band = WINDOW + chunk
    rel = jnp.arange(band)[None, :] - WINDOW - jnp.arange(chunk)[:, None]
    b = rel_bias[_rel_bucket(rel)].reshape(chunk, band, N_KV_HEADS, q_per_kv)
    return jnp.transpose(b, (2, 1, 3, 0)).reshape(N_KV_HEADS, band, q_per_kv * chunk)


def _meta_bias(t0, count, rel_bias, chunk):
    n_heads = rel_bias.shape[1]
    q_per_kv = n_heads // N_KV_HEADS
    dist = jnp.arange(t0 + 1, t0 + count + N_META + 1)
    tab = rel_bias[_rel_bucket(-dist)].T
    b = jnp.stack([tab[:, N_META - 1 - m:N_META - 1 - m + count] for m in range(N_META)])
    b = b.reshape(N_META, N_KV_HEADS, q_per_kv, count // chunk, chunk)
    return jnp.transpose(b, (3, 1, 0, 2, 4)).reshape(count // chunk, N_KV_HEADS, N_META, q_per_kv * chunk)


def _meta_bias_tiles(seq, rel_bias, chunk, tile_rows):
    tiles = min(seq // tile_rows, 2)
    assert tiles == seq // tile_rows or (tiles - 1) * tile_rows + 1 >= MAX_DISTANCE
    return _meta_bias(0, tiles * tile_rows, rel_bias, chunk)


def kernel(x_prompt, x_sample, state_ssm_re, state_ssm_im, cache_k, cache_v, meta_tokens, norm_mix, norm_ffn,
           ssm_a_re, ssm_a_im, ssm_log_dt, ssm_b_re, ssm_b_im, ssm_c_re, ssm_c_im, ssm_d, w_glu, w_ffn_in,
           w_ffn_out, norm_kv, w_kv, k_norm, w_q, q_norm, attn_sinks, w_o, rel_bias):
    bsz, seq, d = x_prompt.shape
    dbsz, dseq, _ = x_sample.shape
    n_a = ssm_a_re.shape[0]
    n_b = w_q.shape[0]
    n_groups, n_state = ssm_a_re.shape[1:]
    kvw = N_KV_HEADS * HEAD_DIM

    w_glu_b = w_glu.astype(BF16)
    w_in_b = w_ffn_in.astype(BF16)
    w_out_b = w_ffn_out.astype(BF16)
    w_kv_b = w_kv.astype(BF16)
    w_q_b = w_q.astype(BF16)
    w_o_b = w_o.astype(BF16)
    tabs = [_s5_tables(ssm_a_re[l], ssm_a_im[l], ssm_log_dt[l], ssm_b_re[l], ssm_b_im[l], ssm_c_re[l],
                       ssm_c_im[l], ssm_d[l]) for l in range(n_a)]

    n_s = dbsz * dseq
    n_small = n_s + dseq
    xs = jnp.concatenate([x_sample.reshape(n_s, d), jnp.zeros((dseq - N_META, d), F32), meta_tokens], axis=0)
    zero_state = jnp.zeros((1, n_groups, n_state), F32)
    s_re, s_im, meta_h = [], [], []
    for l in range(n_a):
        u = _rms_call(xs, norm_mix[l])
        h0 = _state_to_cols(jnp.concatenate([state_ssm_re[l], zero_state], axis=0),
                            jnp.concatenate([state_ssm_im[l], zero_state], axis=0))
        y, hf = _s5_call(u.reshape(1, n_small, d), tabs[l], h0, spb=dbsz + 1, cps=dseq // S5_T, tokens=n_small)
        xs = _glu_call(xs, y.reshape(n_small, d), w_glu_b[l])
        xs = _ffn_call(xs, norm_ffn[l], w_in_b[l], w_out_b[l])
        hr, hi = _cols_to_state(hf)
        s_re.append(hr[:dbsz])
        s_im.append(hi[:dbsz])
        meta_h.append((hr[dbsz:], hi[dbsz:]))
    k_small, v_small = _kv_call(xs, norm_kv, w_kv_b, k_norm)
    k_s, v_s = k_small[:n_s], v_small[:n_s]
    k_meta, v_meta = k_small[n_small - N_META:], v_small[n_small - N_META:]

    bband_s = _band_bias(rel_bias, dseq)
    bmeta_s = _meta_bias(PAST_LEN, dseq, rel_bias, dseq)
    xq = xs[:n_s]
    ck = cache_k.reshape(dbsz, WINDOW, kvw)
    cv = cache_v.reshape(dbsz, WINDOW, kvw)
    for j in range(n_b):
        l = n_a + j
        xq = _attn_call(xq, norm_mix[l], w_q_b[j], q_norm[j], w_o_b[j], attn_sinks[j], k_s, v_s, ck, cv, k_meta,
                        v_meta, bband_s, bmeta_s, chunk=dseq, shared=False, nseq=1,
                        tiles_per_seq=dbsz // ATTN_SEQS_SAMPLE, chunks_per_tile=ATTN_SEQS_SAMPLE)
        xq = _ffn_call(xq, norm_ffn[l], w_in_b[l], w_out_b[l])
    y_sample = xq.reshape(dbsz, dseq, d)

    xp = x_prompt.reshape(bsz * seq, d)
    s5_tokens = min(S5_TOKENS, seq)
    p_re, p_im = [], []
    for l in range(n_a):
        u = _rms_call(xp, norm_mix[l])
        h0 = _state_to_cols(jnp.broadcast_to(meta_h[l][0], (bsz, n_groups, n_state)),
                            jnp.broadcast_to(meta_h[l][1], (bsz, n_groups, n_state)))
        y, hf = _s5_call(u.reshape(bsz, seq, d), tabs[l], h0, spb=1, cps=s5_tokens // S5_T, tokens=s5_tokens)
        xp = _glu_call(xp, y.reshape(bsz * seq, d), w_glu_b[l])
        xp = _ffn_call(xp, norm_ffn[l], w_in_b[l], w_out_b[l])
        hr, hi = _cols_to_state(hf)
        p_re.append(hr)
        p_im.append(hi)
    k_p, v_p = _kv_call(xp, norm_kv, w_kv_b, k_norm)

    bband_p = _band_bias(rel_bias, CHUNK)
    tile_rows = ATTN_CHUNKS_PROMPT * CHUNK
    bmeta_p = _meta_bias_tiles(seq, rel_bias, CHUNK, tile_rows)
    for j in range(n_b):
        l = n_a + j
        xp = _attn_call(xp, norm_mix[l], w_q_b[j], q_norm[j], w_o_b[j], attn_sinks[j], k_p, v_p, k_p, v_p, k_meta,
                        v_meta, bband_p, bmeta_p, chunk=CHUNK, shared=True, nseq=bsz,
                        tiles_per_seq=seq // tile_rows, chunks_per_tile=ATTN_CHUNKS_PROMPT)
        xp = _ffn_call(xp, norm_ffn[l], w_in_b[l], w_out_b[l])
    y_prompt = xp.reshape(bsz, seq, d)

    k_p4 = k_p.reshape(bsz, seq, N_KV_HEADS, HEAD_DIM)
    v_p4 = v_p.reshape(bsz, seq, N_KV_HEADS, HEAD_DIM)
    return (y_prompt, y_sample, jnp.stack(p_re), jnp.stack(p_im), k_p4[:, -WINDOW:], v_p4[:, -WINDOW:],
            jnp.stack(s_re), jnp.stack(s_im), k_s.reshape(dbsz, dseq, N_KV_HEADS, HEAD_DIM),
            v_s.reshape(dbsz, dseq, N_KV_HEADS, HEAD_DIM))
```

```python
import functools
import math

import jax
import jax.numpy as jnp
from jax import lax
from jax.experimental import pallas as pl
from jax.experimental.pallas import tpu as pltpu

F32 = jnp.float32
BF16 = jnp.bfloat16

CHUNK = 64
N_META = 16
SSM_GROUP = 16
SSM_STATE = 64
HEAD_DIM = 64
N_KV_HEADS = 4
WINDOW = 128
N_BUCKETS = 32
MAX_DISTANCE = 128
PAST_LEN = 1024
EPS = 1e-6
NEG = -1e30

LANES = 128
SUBLANES = 8
VMEM_LIMIT_BYTES = 56 * 1024 * 1024

S5_T = 16
S5_PAIR = 2 * SSM_GROUP
S5_PAIRS = LANES // S5_PAIR
S5_TOKENS = 2048
S5_SCAN_UNROLL = 4
ROW_TILE = 512
FFN_CHUNK = 256
ATTN_CHUNKS_PROMPT = 8
ATTN_SEQS_SAMPLE = 8


def _row_tile(n, target):
    if n <= target:
        return n
    best = None
    for t in range(SUBLANES, target + 1, SUBLANES):
        if n % t == 0:
            best = t
    assert best is not None, n
    return best


def _cparams(*sem):
    return pltpu.CompilerParams(dimension_semantics=sem, vmem_limit_bytes=VMEM_LIMIT_BYTES)


def _resident(shape):
    nd = len(shape)
    return pl.BlockSpec(shape, lambda *_: (0,) * nd, pipeline_mode=pl.Buffered(1))


def _rms(x, g):
    ms = jnp.mean(x * x, axis=-1, keepdims=True)
    return (x * lax.rsqrt(ms + EPS)) * g


def _dot(a, b):
    return jnp.dot(a, b, preferred_element_type=F32)


def _dot_nt(a, b):
    return lax.dot_general(a, b, (((1,), (1,)), ((), ())), preferred_element_type=F32)


def _head_rms(x, e, et, gain):
    ss = _dot((x * x).astype(BF16), e)
    rinv = lax.rsqrt(ss * (1.0 / HEAD_DIM) + EPS)
    hi = rinv.astype(BF16)
    lo = (rinv - hi.astype(F32)).astype(BF16)
    rb = _dot(hi, et) + _dot(lo, et)
    return (x * rb) * gain


def _segment_mats(n_heads):
    idx = jnp.arange(n_heads * HEAD_DIM) // HEAD_DIM
    e = (idx[:, None] == jnp.arange(LANES)[None, :]).astype(BF16)
    return e, e.T


def _rms_body(x_ref, g_ref, o_ref):
    o_ref[...] = _rms(x_ref[...], g_ref[...])


def _rms_call(x, g):
    n, d = x.shape
    tm = _row_tile(n, 1024)
    return pl.pallas_call(
        _rms_body,
        grid=(n // tm,),
        in_specs=[pl.BlockSpec((tm, d), lambda i: (i, 0)), _resident((1, d))],
        out_specs=pl.BlockSpec((tm, d), lambda i: (i, 0)),
        out_shape=jax.ShapeDtypeStruct((n, d), F32),
        compiler_params=_cparams("parallel"),
        name="rms",
    )(x, g.reshape(1, d))


def _xpose4(a, lo64, lo32):
    a0, a1, a2, a3 = a
    r02 = pltpu.roll(jnp.where(lo64, a2, a0), 64, 1)
    r13 = pltpu.roll(jnp.where(lo64, a3, a1), 64, 1)
    b0 = jnp.where(lo64, a0, r02)
    b2 = jnp.where(lo64, r02, a2)
    b1 = jnp.where(lo64, a1, r13)
    b3 = jnp.where(lo64, r13, a3)
    return (jnp.where(lo32, b0, pltpu.roll(b1, 32, 1)), jnp.where(lo32, pltpu.roll(b0, 96, 1), b1),
            jnp.where(lo32, b2, pltpu.roll(b3, 32, 1)), jnp.where(lo32, pltpu.roll(b2, 96, 1), b3))


def _s5_body(spb, cps, u_ref, xb_ref, xc_ref, xd_ref, pw_ref, h0_ref, y_ref, hf_ref,
             s_scr, hin_scr, h_scr, m_scr, wo_scr, wit_scr):
    nb = u_ref.shape[0]
    t = S5_T
    rows_b = spb * cps
    nc = nb * rows_b
    ns = nb * spb
    ncol = 2 * S5_PAIRS
    width = t * S5_PAIR

    @pl.when(pl.program_id(1) == 0)
    def _():
        h_scr[...] = h0_ref[...]
        strip_lane = lax.broadcasted_iota(jnp.int32, (S5_PAIR, width), 1)
        for p in range(S5_PAIRS):
            br, bi = xb_ref[p, 0], xb_ref[p, 1]
            cr, ci = xc_ref[p, 0], xc_ref[p, 1]
            cw = []
            for s in range(t + 1):
                ar, ai = pw_ref[p, 0, s:s + 1, :], pw_ref[p, 1, s:s + 1, :]
                cw.append(jnp.concatenate([cr * ar - ci * ai, -(cr * ai + ci * ar)], axis=1))
            for s in range(t):
                rs = slice(S5_PAIR * s, S5_PAIR * (s + 1))
                ar, ai = pw_ref[p, 0, t - 1 - s:t - s, :], pw_ref[p, 1, t - 1 - s:t - s, :]
                wo_scr[p, rs, 0:LANES] = (br * ar - bi * ai).astype(BF16)
                wo_scr[p, rs, LANES:2 * LANES] = (br * ai + bi * ar).astype(BF16)
                wit_scr[p, rs, :] = cw[s + 1].astype(BF16)
            xbb = jnp.concatenate([br, bi], axis=1)
            cwt = jnp.concatenate(cw[:t], axis=0)
            x_hi, c_hi = xbb.astype(BF16), cwt.astype(BF16)
            x_lo, c_lo = (xbb - x_hi.astype(F32)).astype(BF16), (cwt - c_hi.astype(F32)).astype(BF16)
            strip = _dot_nt(x_hi, c_hi) + (_dot_nt(x_hi, c_lo) + _dot_nt(x_lo, c_hi))
            strip = strip + jnp.concatenate([xd_ref[p], jnp.zeros((S5_PAIR, width - LANES), F32)], axis=1)
            for s in range(t):
                blk = strip if s == 0 else jnp.where(strip_lane >= S5_PAIR * s, pltpu.roll(strip, S5_PAIR * s, 1), 0.0)
                m_scr[p, S5_PAIR * s:S5_PAIR * (s + 1), :] = blk.astype(BF16)

    lane = lax.broadcasted_iota(jnp.int32, (nc, LANES), 1)
    lo64 = lane < 64
    lo32 = (lane % 64) < 32

    v = []
    for tok in range(t):
        parts = [u_ref[b, pl.ds(tok, rows_b, stride=t), :] for b in range(nb)]
        v.append(parts[0] if nb == 1 else jnp.concatenate(parts, axis=0))
    quads = [_xpose4(v[4 * tq:4 * tq + 4], lo64, lo32) for tq in range(t // 4)]
    u_p = [jnp.concatenate([quads[tq][p] for tq in range(t // 4)], axis=1).astype(BF16) for p in range(S5_PAIRS)]

    for p in range(S5_PAIRS):
        s = _dot(u_p[p], wo_scr[p])
        s_scr[2 * p, 0:nc, :] = s[:, :LANES]
        s_scr[2 * p + 1, 0:nc, :] = s[:, LANES:]

    def step(k, carry):
        new = []
        for p in range(S5_PAIRS):
            cr, ci = carry[2 * p], carry[2 * p + 1]
            hin_scr[2 * p, pl.ds(k, ns, stride=cps), :] = cr
            hin_scr[2 * p + 1, pl.ds(k, ns, stride=cps), :] = ci
            ar = pw_ref[p, 0, t:t + 1, :]
            ai = pw_ref[p, 1, t:t + 1, :]
            tr = s_scr[2 * p, pl.ds(k, ns, stride=cps), :]
            ti = s_scr[2 * p + 1, pl.ds(k, ns, stride=cps), :]
            new += [ar * cr - ai * ci + tr, ar * ci + ai * cr + ti]
        return tuple(new)

    carry = lax.fori_loop(0, cps, step, tuple(h_scr[c] for c in range(ncol)), unroll=min(cps, S5_SCAN_UNROLL))
    for c in range(ncol):
        h_scr[c] = carry[c]
        hf_ref[c] = carry[c]

    y_p = []
    for p in range(S5_PAIRS):
        h = jnp.concatenate([hin_scr[2 * p, 0:nc, :], hin_scr[2 * p + 1, 0:nc, :]], axis=1).astype(BF16)
        y_p.append(_dot(u_p[p], m_scr[p]) + _dot_nt(h, wit_scr[p]))
    for tq in range(t // 4):
        w = _xpose4([y_p[p][:, LANES * tq:LANES * (tq + 1)] for p in range(S5_PAIRS)], lo64, lo32)
        for i in range(4):
            for b in range(nb):
                y_ref[b, pl.ds(4 * tq + i, rows_b, stride=t), :] = w[i][b * rows_b:(b + 1) * rows_b]


def _s5_call(u, tabs, h0, spb, cps, tokens):
    xb, xc, xd, pw = tabs
    nb, length, d = u.shape
    assert tokens == spb * cps * S5_T and length % tokens == 0
    ns = nb * spb
    ncol = 2 * S5_PAIRS
    nc_pad = -(-(nb * spb * cps) // SUBLANES) * SUBLANES
    width = S5_T * S5_PAIR
    blk = pl.BlockSpec((nb, tokens, LANES), lambda b, r: (0, r, b))
    state = pl.BlockSpec((ncol, ns, LANES), lambda b, r: (b, 0, 0))
    return pl.pallas_call(
        functools.partial(_s5_body, spb, cps),
        grid=(d // LANES, length // tokens),
        in_specs=[
            blk,
            pl.BlockSpec((S5_PAIRS, 2, S5_PAIR, LANES), lambda b, r: (b, 0, 0, 0)),
            pl.BlockSpec((S5_PAIRS, 2, S5_PAIR, LANES), lambda b, r: (b, 0, 0, 0)),
            pl.BlockSpec((S5_PAIRS, S5_PAIR, LANES), lambda b, r: (b, 0, 0)),
            pl.BlockSpec((S5_PAIRS, 2, S5_T + 1, LANES), lambda b, r: (b, 0, 0, 0)),
            state,
        ],
        out_specs=[blk, state],
        out_shape=[jax.ShapeDtypeStruct(u.shape, F32), jax.ShapeDtypeStruct(h0.shape, F32)],
        scratch_shapes=[pltpu.VMEM((ncol, nc_pad, LANES), F32), pltpu.VMEM((ncol, nc_pad, LANES), F32),
                        pltpu.VMEM((ncol, ns, LANES), F32), pltpu.VMEM((S5_PAIRS, width, width), BF16),
                        pltpu.VMEM((S5_PAIRS, width, 2 * LANES), BF16), pltpu.VMEM((S5_PAIRS, width, 2 * LANES), BF16)],
        compiler_params=_cparams("parallel", "arbitrary"),
        name="s5",
    )(u, xb, xc, xd, pw, h0)


def _s5_tables(a_re, a_im, log_dt, b_re, b_im, c_re, c_im, d_skip):
    t = S5_T
    g, p = a_re.shape
    c = b_re.shape[-1]
    dt = jnp.exp(log_dt)[:, None]
    mag = jnp.exp(a_re * dt)
    ang = a_im * dt
    ar = mag * jnp.cos(ang)
    ai = mag * jnp.sin(ang)
    num_re = ar - 1.0
    num_im = ai
    inv = 1.0 / (a_re * a_re + a_im * a_im)
    f_re = (num_re * a_re + num_im * a_im) * inv
    f_im = (num_im * a_re - num_re * a_im) * inv
    bb_re = f_re[..., None] * b_re - f_im[..., None] * b_im
    bb_im = f_re[..., None] * b_im + f_im[..., None] * b_re
    pr = jnp.ones((1, g, p), F32)
    pi = jnp.zeros((1, g, p), F32)
    sr, si = ar, ai
    while pr.shape[0] < t + 1:
        pr, pi = (jnp.concatenate([pr, pr * sr - pi * si]), jnp.concatenate([pi, pr * si + pi * sr]))
        sr, si = sr * sr - si * si, 2.0 * sr * si
    pr = pr[:t + 1]
    pi = pi[:t + 1]
    eye2 = jnp.eye(2, dtype=F32)

    def pad_pair(x):
        x = x.reshape(g // 2, 2, c, p)
        return (x[:, :, :, None, :] * eye2[None, :, None, :, None]).reshape(g // 2, 2 * c, 2 * p)

    xb = jnp.stack([pad_pair(jnp.transpose(bb_re, (0, 2, 1))), pad_pair(jnp.transpose(bb_im, (0, 2, 1)))], axis=1)
    xc = jnp.stack([pad_pair(c_re), pad_pair(c_im)], axis=1)
    xd = d_skip.reshape(g // 2, 2 * c, 1) * jnp.eye(2 * c, LANES, dtype=F32)[None]
    pw = jnp.stack([jnp.transpose(pr.reshape(t + 1, g // 2, 2 * p), (1, 0, 2)),
                    jnp.transpose(pi.reshape(t + 1, g // 2, 2 * p), (1, 0, 2))], axis=1)
    return xb, xc, xd, pw


def _state_to_cols(h_re, h_im):
    ns, g, p = h_re.shape
    f = lambda h: jnp.transpose(h.reshape(ns, g // 2, 2 * p), (1, 0, 2))
    return jnp.stack([f(h_re), f(h_im)], axis=1).reshape(g, ns, 2 * p)


def _cols_to_state(h):
    g, ns, w = h.shape
    h = jnp.transpose(h.reshape(g // 2, 2, ns, 2, w // 2), (1, 2, 0, 3, 4)).reshape(2, ns, g, w // 2)
    return h[0], h[1]


def _ffn_rows(x, g, win_ref, wout_ref):
    dff = wout_ref.shape[0]
    xn = _rms(x, g).astype(BF16)
    acc = x
    for j in range(dff // FFN_CHUNK):
        lo = j * FFN_CHUNK
        gate = _dot(xn, win_ref[:, lo:lo + FFN_CHUNK])
        up = _dot(xn, win_ref[:, dff + lo:dff + lo + FFN_CHUNK])
        a = (jax.nn.silu(gate) * up).astype(BF16)
        acc = acc + _dot(a, wout_ref[lo:lo + FFN_CHUNK, :])
    return acc


def _ffn_body(x_ref, g_ref, win_ref, wout_ref, o_ref):
    o_ref[...] = _ffn_rows(x_ref[...], g_ref[...], win_ref, wout_ref)


def _glu_ffn_body(x_ref, y_ref, wglu_ref, g_ref, win_ref, wout_ref, o_ref):
    d = x_ref.shape[1]
    r = _dot(jax.nn.gelu(y_ref[...]).astype(BF16), wglu_ref[...])
    x = x_ref[...] + r[:, :d] * jax.nn.sigmoid(r[:, d:])
    o_ref[...] = _ffn_rows(x, g_ref[...], win_ref, wout_ref)


def _ffn_call(x, g, w_in, w_out, y=None, w_glu=None):
    n, d = x.shape
    tm = _row_tile(n, ROW_TILE)
    row = pl.BlockSpec((tm, d), lambda i: (i, 0))
    weights = [_resident((1, d)), _resident(w_in.shape), _resident(w_out.shape)]
    if y is None:
        body, name, in_specs, args = _ffn_body, "ffn", [row] + weights, (x, g.reshape(1, d), w_in, w_out)
    else:
        body, name = _glu_ffn_body, "glu_ffn"
        in_specs = [row, row, _resident(w_glu.shape)] + weights
        args = (x, y, w_glu, g.reshape(1, d), w_in, w_out)
    return pl.pallas_call(
        body,
        grid=(n // tm,),
        in_specs=in_specs,
        out_specs=row,
        out_shape=jax.ShapeDtypeStruct((n, d), F32),
        compiler_params=_cparams("parallel"),
        name=name,
    )(*args)


def _kv_body(x_ref, g_ref, w_ref, kn_ref, e_ref, et_ref, k_ref, v_ref):
    kvw = k_ref.shape[1]
    xn = _rms(x_ref[...], g_ref[...]).astype(BF16)
    r = _dot(xn, w_ref[...])
    k_ref[...] = _head_rms(r[:, :kvw], e_ref[...], et_ref[...], kn_ref[...])
    v_ref[...] = r[:, kvw:]


def _kv_call(x, g, w, k_norm):
    n, d = x.shape
    kvw = w.shape[1] // 2
    tm = _row_tile(n, ROW_TILE)
    e, et = _segment_mats(kvw // HEAD_DIM)
    out = pl.BlockSpec((tm, kvw), lambda i: (i, 0))
    return pl.pallas_call(
        _kv_body,
        grid=(n // tm,),
        in_specs=[pl.BlockSpec((tm, d), lambda i: (i, 0)), _resident((1, d)), _resident(w.shape),
                  _resident((1, kvw)), _resident(e.shape), _resident(et.shape)],
        out_specs=[out, out],
        out_shape=[jax.ShapeDtypeStruct((n, kvw), F32)] * 2,
        compiler_params=_cparams("parallel"),
        name="kv",
    )(x, g.reshape(1, d), w, jnp.tile(k_norm, kvw // HEAD_DIM).reshape(1, kvw), e, et)


def _attn_body(chunk, shared, x_ref, g_ref, wq_ref, gain_ref, wo_ref, sink_ref, kown_ref, vown_ref, kleft_ref,
               vleft_ref, kmeta_ref, vmeta_ref, bband_ref, bmeta_ref, e_ref, et_ref, o_ref,
               q_scr, kd_scr, vt_scr, lg_scr, att_scr):
    rows, d = x_ref.shape
    nchunks = rows // chunk
    n_heads = d // HEAD_DIM
    q_per_kv = n_heads // N_KV_HEADS
    band = WINDOW + chunk
    qlanes = q_per_kv * chunk
    kvw = N_KV_HEADS * HEAD_DIM

    x = x_ref[...]
    q = _dot(_rms(x, g_ref[...]).astype(BF16), wq_ref[...])
    q_scr[...] = _head_rms(q, e_ref[...], et_ref[...], 1.0)

    gain = gain_ref[...]
    low = lax.broadcasted_iota(jnp.int32, (1, LANES), 1) < HEAD_DIM

    def dup(kf):
        out = []
        for kv in range(N_KV_HEADS):
            t = kf[:, (kv // 2) * LANES:(kv // 2 + 1) * LANES]
            r = pltpu.roll(t, HEAD_DIM, 1)
            out.append((jnp.where(low, t, r) if kv % 2 == 0 else jnp.where(low, r, t)).astype(BF16))
        return out

    def put_keys(row0, kf, vf):
        for kv, t in enumerate(dup(kf * gain)):
            kd_scr[kv, row0:row0 + LANES, :] = t
        vt_scr[:, row0:row0 + LANES] = vf.T.astype(BF16)

    km = dup(kmeta_ref[...] * gain)
    vmt = vmeta_ref[...].astype(BF16)
    if shared:
        put_keys(0, kleft_ref[...], vleft_ref[...])
        for t in range(rows // LANES):
            put_keys(WINDOW + t * LANES, kown_ref[t * LANES:(t + 1) * LANES, :], vown_ref[t * LANES:(t + 1) * LANES, :])
        first_chunk = pl.program_id(1) * nchunks
    else:
        zpad = jnp.zeros((LANES - chunk, kvw), F32)
        for j in range(nchunks):
            put_keys(j * 2 * LANES, kleft_ref[j], vleft_ref[j])
            put_keys(j * 2 * LANES + LANES, jnp.concatenate([kown_ref[j * chunk:(j + 1) * chunk, :], zpad], axis=0),
                     jnp.concatenate([vown_ref[j * chunk:(j + 1) * chunk, :], zpad], axis=0))

    for j in range(nchunks):
        r0 = j * chunk
        if shared:
            base = r0
            blocks = [(base + i * chunk, chunk, i * chunk, WINDOW // chunk - i) for i in range(band // chunk)]
        else:
            base = j * 2 * LANES
            blocks = [(base, WINDOW, 0, None), (base + WINDOW, chunk, WINDOW, None)]
        tile0 = (base // LANES) * LANES
        bm = bmeta_ref[j] if shared else bmeta_ref[0]
        for pair in range(N_KV_HEADS // 2):
            o_pair = []
            for kv in (2 * pair, 2 * pair + 1):
                qx = []
                for gq in range(q_per_kv):
                    h = kv * q_per_kv + gq
                    t = q_scr[r0:r0 + chunk, (h // 2) * LANES:(h // 2 + 1) * LANES]
                    qx.append(jnp.where(low if h % 2 == 0 else jnp.logical_not(low), t, 0.0))
                qx = jnp.concatenate(qx, axis=0).astype(BF16)
                s = sink_ref[kv]
                run = jnp.full((SUBLANES, qlanes), NEG, F32)
                for (k0, nk, b0, back) in blocks:
                    lg = _dot_nt(kd_scr[kv, k0:k0 + nk, :], qx) + bband_ref[kv, b0:b0 + nk, :]
                    if back:
                        lg = jnp.where(first_chunk + j >= back, lg, NEG)
                    lg_scr[kv, b0:b0 + nk, :] = lg
                    run = jnp.maximum(run, jnp.max(lg.reshape(nk // SUBLANES, SUBLANES, qlanes), axis=0))
                lgm = _dot_nt(km[kv], qx) + bm[kv]
                run = jnp.maximum(run, jnp.max(lgm.reshape(N_META // SUBLANES, SUBLANES, qlanes), axis=0))
                mx = jnp.maximum(jnp.max(run, axis=0, keepdims=True), s)
                pm = jnp.exp(lgm - mx)
                den = jnp.sum(pm, axis=0, keepdims=True) + jnp.exp(s - mx)
                ot = _dot(vmt[kv * HEAD_DIM:(kv + 1) * HEAD_DIM, :], pm.astype(BF16))
                parts = {}
                for (k0, nk, b0, back) in blocks:
                    p = jnp.exp(lg_scr[kv, b0:b0 + nk, :] - mx)
                    den = den + jnp.sum(p, axis=0, keepdims=True)
                    parts[k0] = p.astype(BF16)
                for t0 in (tile0, tile0 + LANES):
                    segs, pos = [], t0
                    while pos < t0 + LANES:
                        if pos in parts:
                            segs.append(parts[pos])
                            pos += parts[pos].shape[0]
                        else:
                            nxt = min([k0 for k0 in parts if k0 > pos] + [t0 + LANES])
                            segs.append(jnp.zeros((nxt - pos, qlanes), BF16))
                            pos = nxt
                    pt = segs[0] if len(segs) == 1 else jnp.concatenate(segs, axis=0)
                    ot = ot + _dot(vt_scr[kv * HEAD_DIM:(kv + 1) * HEAD_DIM, t0:t0 + LANES], pt)
                o_pair.append(ot / den)
            o2 = jnp.concatenate(o_pair, axis=0).T
            for gq in range(q_per_kv):
                tile = pair * q_per_kv + gq
                att_scr[r0:r0 + chunk, tile * LANES:(tile + 1) * LANES] = o2[gq * chunk:(gq + 1) * chunk, :]

    o_ref[...] = x + _dot(att_scr[...].astype(BF16), wo_ref[...])


def _attn_call(x, g, w_q, q_norm, w_o, sinks, k_own, v_own, k_left, v_left, k_meta, v_meta, b_band, b_meta,
               chunk, shared, nseq, tiles_per_seq, chunks_per_tile):
    n, d = x.shape
    kvw = k_own.shape[1]
    n_heads = d // HEAD_DIM
    q_per_kv = n_heads // N_KV_HEADS
    rows = chunks_per_tile * chunk
    band = WINDOW + chunk
    qlanes = q_per_kv * chunk
    assert rows % LANES == 0 and LANES % chunk == 0
    key_rows = WINDOW + rows if shared else chunks_per_tile * 2 * LANES
    e, et = _segment_mats(n_heads)
    gain = jnp.tile(q_norm * (HEAD_DIM ** -0.5), N_KV_HEADS).reshape(1, kvw)
    sink_row = jnp.broadcast_to(sinks.reshape(N_KV_HEADS, 1, q_per_kv, 1),
                                (N_KV_HEADS, 1, q_per_kv, chunk)).reshape(N_KV_HEADS, 1, qlanes)
    w_o = jnp.transpose(w_o.reshape(N_KV_HEADS // 2, 2, q_per_kv, HEAD_DIM, d), (0, 2, 1, 3, 4)).reshape(d, d)
    row_idx = lambda b, i: (b * tiles_per_seq + i, 0)
    if shared:
        halo_blocks = rows // WINDOW
        left_spec = pl.BlockSpec(
            (WINDOW, kvw), lambda b, i: (jnp.maximum(b * tiles_per_seq * halo_blocks + i * halo_blocks - 1, 0), 0))
        last_tile = b_meta.shape[0] // chunks_per_tile - 1
        bmeta_spec = pl.BlockSpec((chunks_per_tile,) + b_meta.shape[1:],
                                  lambda b, i: (jnp.minimum(i, last_tile), 0, 0, 0))
    else:
        left_spec = pl.BlockSpec((chunks_per_tile, WINDOW, kvw), lambda b, i: (b * tiles_per_seq + i, 0, 0))
        bmeta_spec = _resident(b_meta.shape)
    return pl.pallas_call(
        functools.partial(_attn_body, chunk, shared),
        grid=(nseq, tiles_per_seq),
        in_specs=[
            pl.BlockSpec((rows, d), row_idx), _resident((1, d)), _resident(w_q.shape), _resident((1, kvw)),
            _resident(w_o.shape), _resident(sink_row.shape),
            pl.BlockSpec((rows, kvw), row_idx), pl.BlockSpec((rows, kvw), row_idx), left_spec, left_spec,
            _resident(k_meta.shape), _resident((kvw, N_META)), _resident(b_band.shape), bmeta_spec,
            _resident(e.shape), _resident(et.shape),
        ],
        out_specs=pl.BlockSpec((rows, d), row_idx),
        out_shape=jax.ShapeDtypeStruct((n, d), F32),
        scratch_shapes=[pltpu.VMEM((rows, d), F32), pltpu.VMEM((N_KV_HEADS, key_rows, LANES), BF16),
                        pltpu.VMEM((kvw, key_rows), BF16), pltpu.VMEM((N_KV_HEADS, band, qlanes), F32),
                        pltpu.VMEM((rows, d), F32)],
        compiler_params=_cparams("parallel", "arbitrary"),
        name="attn_prompt" if shared else "attn_sample",
    )(x, g.reshape(1, d), w_q, gain, w_o, sink_row, k_own, v_own, k_left, v_left, k_meta, v_meta.T, b_band, b_meta,
      e, et)


def _rel_bucket(rel):
    half = N_BUCKETS // 2
    max_exact = half // 2
    n = jnp.abs(rel)
    nf = jnp.maximum(n, 1).astype(F32)
    large = max_exact + (jnp.log(nf / max_exact) / math.log(MAX_DISTANCE / max_exact)
                         * (half - max_exact)).astype(jnp.int32)
    large = jnp.minimum(large, half - 1)
    return jnp.where(rel > 0, half, 0) + jnp.where(n < max_exact, n, large)


def _bias_lookup(rel_bias, bucket):
    onehot = (bucket[..., None] == jnp.arange(N_BUCKETS)).astype(F32)
    return jnp.dot(onehot, rel_bias, precision=lax.Precision.HIGHEST)


def _band_bias(rel_bias, chunk):
    n_heads = rel_bias.shape[1]
    q_per_kv = n_heads // N_KV_HEADS
    band = WINDOW + chunk
    rel = jnp.arange(band)[None, :] - WINDOW - jnp.arange(chunk)[:, None]
    b = _bias_lookup(rel_bias, _rel_bucket(rel)).reshape(chunk, band, N_KV_HEADS, q_per_kv)
    return jnp.transpose(b, (2, 1, 3, 0)).reshape(N_KV_HEADS, band, q_per_kv * chunk)


def _meta_bias(t0, count, rel_bias, chunk):
    n_heads = rel_bias.shape[1]
    q_per_kv = n_heads // N_KV_HEADS
    dist = jnp.arange(t0 + 1, t0 + count + N_META + 1)
    tab = _bias_lookup(rel_bias, _rel_bucket(-dist)).T
    b = jnp.stack([tab[:, N_META - 1 - m:N_META - 1 - m + count] for m in range(N_META)])
    b = b.reshape(N_META, N_KV_HEADS, q_per_kv, count // chunk, chunk)
    return jnp.transpose(b, (3, 1, 0, 2, 4)).reshape(count // chunk, N_KV_HEADS, N_META, q_per_kv * chunk)


def _meta_bias_tiles(seq, rel_bias, chunk, tile_rows):
    tiles = min(seq // tile_rows, 2)
    assert tiles == seq // tile_rows or (tiles - 1) * tile_rows + 1 >= MAX_DISTANCE
    return _meta_bias(0, tiles * tile_rows, rel_bias, chunk)


def kernel(x_prompt, x_sample, state_ssm_re, state_ssm_im, cache_k, cache_v, meta_tokens, norm_mix, norm_ffn,
           ssm_a_re, ssm_a_im, ssm_log_dt, ssm_b_re, ssm_b_im, ssm_c_re, ssm_c_im, ssm_d, w_glu, w_ffn_in,
           w_ffn_out, norm_kv, w_kv, k_norm, w_q, q_norm, attn_sinks, w_o, rel_bias):
    bsz, seq, d = x_prompt.shape
    dbsz, dseq, _ = x_sample.shape
    n_a = ssm_a_re.shape[0]
    n_b = w_q.shape[0]
    n_groups, n_state = ssm_a_re.shape[1:]
    kvw = N_KV_HEADS * HEAD_DIM

    w_glu_b = w_glu.astype(BF16)
    w_in_b = w_ffn_in.astype(BF16)
    w_out_b = w_ffn_out.astype(BF16)
    w_kv_b = w_kv.astype(BF16)
    w_q_b = w_q.astype(BF16)
    w_o_b = w_o.astype(BF16)
    tabs = [_s5_tables(ssm_a_re[l], ssm_a_im[l], ssm_log_dt[l], ssm_b_re[l], ssm_b_im[l], ssm_c_re[l],
                       ssm_c_im[l], ssm_d[l]) for l in range(n_a)]

    n_s = dbsz * dseq
    n_small = n_s + dseq
    xs = jnp.concatenate([x_sample.reshape(n_s, d), jnp.zeros((dseq - N_META, d), F32), meta_tokens], axis=0)
    zero_state = jnp.zeros((1, n_groups, n_state), F32)
    s_re, s_im, meta_h = [], [], []
    for l in range(n_a):
        u = _rms_call(xs, norm_mix[l])
        h0 = _state_to_cols(jnp.concatenate([state_ssm_re[l], zero_state], axis=0),
                            jnp.concatenate([state_ssm_im[l], zero_state], axis=0))
        y, hf = _s5_call(u.reshape(1, n_small, d), tabs[l], h0, spb=dbsz + 1, cps=dseq // S5_T, tokens=n_small)
        xs = _ffn_call(xs, norm_ffn[l], w_in_b[l], w_out_b[l], y=y.reshape(n_small, d), w_glu=w_glu_b[l])
        hr, hi = _cols_to_state(hf)
        s_re.append(hr[:dbsz])
        s_im.append(hi[:dbsz])
        meta_h.append((hr[dbsz:], hi[dbsz:]))
    k_small, v_small = _kv_call(xs, norm_kv, w_kv_b, k_norm)
    k_s, v_s = k_small[:n_s], v_small[:n_s]
    k_meta, v_meta = k_small[n_small - N_META:], v_small[n_small - N_META:]

    bband_s = _band_bias(rel_bias, dseq)
    bmeta_s = _meta_bias(PAST_LEN, dseq, rel_bias, dseq)
    xq = xs[:n_s]
    ck = cache_k.reshape(dbsz, WINDOW, kvw)
    cv = cache_v.reshape(dbsz, WINDOW, kvw)
    for j in range(n_b):
        l = n_a + j
        xq = _attn_call(xq, norm_mix[l], w_q_b[j], q_norm[j], w_o_b[j], attn_sinks[j], k_s, v_s, ck, cv, k_meta,
                        v_meta, bband_s, bmeta_s, chunk=dseq, shared=False, nseq=1,
                        tiles_per_seq=dbsz // ATTN_SEQS_SAMPLE, chunks_per_tile=ATTN_SEQS_SAMPLE)
        xq = _ffn_call(xq, norm_ffn[l], w_in_b[l], w_out_b[l])
    y_sample = xq.reshape(dbsz, dseq, d)

    xp = x_prompt.reshape(bsz * seq, d)
    s5_tokens = min(S5_TOKENS, seq)
    p_re, p_im = [], []
    for l in range(n_a):
        u = _rms_call(xp, norm_mix[l])
        h0 = _state_to_cols(jnp.broadcast_to(meta_h[l][0], (bsz, n_groups, n_state)),
                            jnp.broadcast_to(meta_h[l][1], (bsz, n_groups, n_state)))
        y, hf = _s5_call(u.reshape(bsz, seq, d), tabs[l], h0, spb=1, cps=s5_tokens // S5_T, tokens=s5_tokens)
        xp = _ffn_call(xp, norm_ffn[l], w_in_b[l], w_out_b[l], y=y.reshape(bsz * seq, d), w_glu=w_glu_b[l])
        hr, hi = _cols_to_state(hf)
        p_re.append(hr)
        p_im.append(hi)
    k_p, v_p = _kv_call(xp, norm_kv, w_kv_b, k_norm)

    bband_p = _band_bias(rel_bias, CHUNK)
    tile_rows = ATTN_CHUNKS_PROMPT * CHUNK
    bmeta_p = _meta_bias_tiles(seq, rel_bias, CHUNK, tile_rows)
    for j in range(n_b):
        l = n_a + j
        xp = _attn_call(xp, norm_mix[l], w_q_b[j], q_norm[j], w_o_b[j], attn_sinks[j], k_p, v_p, k_p, v_p, k_meta,
                        v_meta, bband_p, bmeta_p, chunk=CHUNK, shared=True, nseq=bsz,
                        tiles_per_seq=seq // tile_rows, chunks_per_tile=ATTN_CHUNKS_PROMPT)
        xp = _ffn_call(xp, norm_ffn[l], w_in_b[l], w_out_b[l])
    y_prompt = xp.reshape(bsz, seq, d)

    k_p4 = k_p.reshape(bsz, seq, N_KV_HEADS, HEAD_DIM)
    v_p4 = v_p.reshape(bsz, seq, N_KV_HEADS, HEAD_DIM)
    return (y_prompt, y_sample, jnp.stack(p_re), jnp.stack(p_im), k_p4[:, -WINDOW:], v_p4[:, -WINDOW:],
            jnp.stack(s_re), jnp.stack(s_im), k_s.reshape(dbsz, dseq, N_KV_HEADS, HEAD_DIM),
            v_s.reshape(dbsz, dseq, N_KV_HEADS, HEAD_DIM))
```

```python
import functools
import math

import jax
import jax.numpy as jnp
from jax import lax
from jax.experimental import pallas as pl
from jax.experimental.pallas import tpu as pltpu

F32 = jnp.float32
BF16 = jnp.bfloat16

CHUNK = 64
N_META = 16
SSM_GROUP = 16
SSM_STATE = 64
HEAD_DIM = 64
N_KV_HEADS = 4
WINDOW = 128
N_BUCKETS = 32
MAX_DISTANCE = 128
PAST_LEN = 1024
EPS = 1e-6
NEG = -1e30

LANES = 128
SUBLANES = 8
VMEM_LIMIT_BYTES = 56 * 1024 * 1024

S5_T = 16
S5_PAIR = 2 * SSM_GROUP
S5_PAIRS = LANES // S5_PAIR
S5_TOKENS = 2048
S5_SCAN_UNROLL = 4
ROW_TILE = 512
FFN_CHUNK = 256
ATTN_CHUNKS_PROMPT = 8
ATTN_SEQS_SAMPLE = 8


def _row_tile(n, target):
    if n <= target:
        return n
    best = None
    for t in range(SUBLANES, target + 1, SUBLANES):
        if n % t == 0:
            best = t
    assert best is not None, n
    return best


def _cparams(*sem):
    return pltpu.CompilerParams(dimension_semantics=sem, vmem_limit_bytes=VMEM_LIMIT_BYTES)


def _resident(shape):
    nd = len(shape)
    return pl.BlockSpec(shape, lambda *_: (0,) * nd, pipeline_mode=pl.Buffered(1))


def _rms(x, g):
    ms = jnp.mean(x * x, axis=-1, keepdims=True)
    return (x * lax.rsqrt(ms + EPS)) * g


def _dot(a, b):
    return jnp.dot(a, b, preferred_element_type=F32)


def _dot_nt(a, b):
    return lax.dot_general(a, b, (((1,), (1,)), ((), ())), preferred_element_type=F32)


def _head_rms(x, e, et, gain):
    ss = _dot((x * x).astype(BF16), e)
    rinv = lax.rsqrt(ss * (1.0 / HEAD_DIM) + EPS)
    hi = rinv.astype(BF16)
    lo = (rinv - hi.astype(F32)).astype(BF16)
    rb = _dot(hi, et) + _dot(lo, et)
    return (x * rb) * gain


def _segment_mats(n_heads):
    idx = jnp.arange(n_heads * HEAD_DIM) // HEAD_DIM
    e = (idx[:, None] == jnp.arange(LANES)[None, :]).astype(BF16)
    return e, e.T


def _rms_body(x_ref, g_ref, o_ref):
    o_ref[...] = _rms(x_ref[...], g_ref[...])


def _rms_call(x, g):
    n, d = x.shape
    tm = _row_tile(n, 1024)
    return pl.pallas_call(
        _rms_body,
        grid=(n // tm,),
        in_specs=[pl.BlockSpec((tm, d), lambda i: (i, 0)), _resident((1, d))],
        out_specs=pl.BlockSpec((tm, d), lambda i: (i, 0)),
        out_shape=jax.ShapeDtypeStruct((n, d), F32),
        compiler_params=_cparams("parallel"),
        name="rms",
    )(x, g.reshape(1, d))


def _xpose4(a, lo64, lo32):
    a0, a1, a2, a3 = a
    r02 = pltpu.roll(jnp.where(lo64, a2, a0), 64, 1)
    r13 = pltpu.roll(jnp.where(lo64, a3, a1), 64, 1)
    b0 = jnp.where(lo64, a0, r02)
    b2 = jnp.where(lo64, r02, a2)
    b1 = jnp.where(lo64, a1, r13)
    b3 = jnp.where(lo64, r13, a3)
    return (jnp.where(lo32, b0, pltpu.roll(b1, 32, 1)), jnp.where(lo32, pltpu.roll(b0, 96, 1), b1),
            jnp.where(lo32, b2, pltpu.roll(b3, 32, 1)), jnp.where(lo32, pltpu.roll(b2, 96, 1), b3))


def _s5_body(spb, cps, u_ref, xb_ref, xc_ref, xd_ref, pw_ref, h0_ref, y_ref, hf_ref,
             s_scr, hin_scr, h_scr, m_scr, wo_scr, wit_scr):
    nb = u_ref.shape[0]
    t = S5_T
    rows_b = spb * cps
    nc = nb * rows_b
    ns = nb * spb
    ncol = 2 * S5_PAIRS
    width = t * S5_PAIR

    @pl.when(pl.program_id(1) == 0)
    def _():
        h_scr[...] = h0_ref[...]
        strip_lane = lax.broadcasted_iota(jnp.int32, (S5_PAIR, width), 1)
        for p in range(S5_PAIRS):
            br, bi = xb_ref[p, 0], xb_ref[p, 1]
            cr, ci = xc_ref[p, 0], xc_ref[p, 1]
            cw = []
            for s in range(t + 1):
                ar, ai = pw_ref[p, 0, s:s + 1, :], pw_ref[p, 1, s:s + 1, :]
                cw.append(jnp.concatenate([cr * ar - ci * ai, -(cr * ai + ci * ar)], axis=1))
            for s in range(t):
                rs = slice(S5_PAIR * s, S5_PAIR * (s + 1))
                ar, ai = pw_ref[p, 0, t - 1 - s:t - s, :], pw_ref[p, 1, t - 1 - s:t - s, :]
                wo_scr[p, rs, 0:LANES] = (br * ar - bi * ai).astype(BF16)
                wo_scr[p, rs, LANES:2 * LANES] = (br * ai + bi * ar).astype(BF16)
                wit_scr[p, rs, :] = cw[s + 1].astype(BF16)
            xbb = jnp.concatenate([br, bi], axis=1)
            cwt = jnp.concatenate(cw[:t], axis=0)
            x_hi, c_hi = xbb.astype(BF16), cwt.astype(BF16)
            x_lo, c_lo = (xbb - x_hi.astype(F32)).astype(BF16), (cwt - c_hi.astype(F32)).astype(BF16)
            strip = _dot_nt(x_hi, c_hi) + (_dot_nt(x_hi, c_lo) + _dot_nt(x_lo, c_hi))
            strip = strip + jnp.concatenate([xd_ref[p], jnp.zeros((S5_PAIR, width - LANES), F32)], axis=1)
            for s in range(t):
                blk = strip if s == 0 else jnp.where(strip_lane >= S5_PAIR * s, pltpu.roll(strip, S5_PAIR * s, 1), 0.0)
                m_scr[p, S5_PAIR * s:S5_PAIR * (s + 1), :] = blk.astype(BF16)

    lane = lax.broadcasted_iota(jnp.int32, (nc, LANES), 1)
    lo64 = lane < 64
    lo32 = (lane % 64) < 32

    v = []
    for tok in range(t):
        parts = [u_ref[b, pl.ds(tok, rows_b, stride=t), :] for b in range(nb)]
        v.append(parts[0] if nb == 1 else jnp.concatenate(parts, axis=0))
    quads = [_xpose4(v[4 * tq:4 * tq + 4], lo64, lo32) for tq in range(t // 4)]
    u_p = [jnp.concatenate([quads[tq][p] for tq in range(t // 4)], axis=1).astype(BF16) for p in range(S5_PAIRS)]

    for p in range(S5_PAIRS):
        s = _dot(u_p[p], wo_scr[p])
        s_scr[2 * p, 0:nc, :] = s[:, :LANES]
        s_scr[2 * p + 1, 0:nc, :] = s[:, LANES:]

    def step(k, carry):
        new = []
        for p in range(S5_PAIRS):
            cr, ci = carry[2 * p], carry[2 * p + 1]
            hin_scr[2 * p, pl.ds(k, ns, stride=cps), :] = cr
            hin_scr[2 * p + 1, pl.ds(k, ns, stride=cps), :] = ci
            ar = pw_ref[p, 0, t:t + 1, :]
            ai = pw_ref[p, 1, t:t + 1, :]
            tr = s_scr[2 * p, pl.ds(k, ns, stride=cps), :]
            ti = s_scr[2 * p + 1, pl.ds(k, ns, stride=cps), :]
            new += [ar * cr - ai * ci + tr, ar * ci + ai * cr + ti]
        return tuple(new)

    carry = lax.fori_loop(0, cps, step, tuple(h_scr[c] for c in range(ncol)), unroll=min(cps, S5_SCAN_UNROLL))
    for c in range(ncol):
        h_scr[c] = carry[c]
        hf_ref[c] = carry[c]

    y_p = []
    for p in range(S5_PAIRS):
        h = jnp.concatenate([hin_scr[2 * p, 0:nc, :], hin_scr[2 * p + 1, 0:nc, :]], axis=1).astype(BF16)
        y_p.append(_dot(u_p[p], m_scr[p]) + _dot_nt(h, wit_scr[p]))
    for tq in range(t // 4):
        w = _xpose4([y_p[p][:, LANES * tq:LANES * (tq + 1)] for p in range(S5_PAIRS)], lo64, lo32)
        for i in range(4):
            for b in range(nb):
                y_ref[b, pl.ds(4 * tq + i, rows_b, stride=t), :] = w[i][b * rows_b:(b + 1) * rows_b]


def _s5_call(u, tabs, h0, spb, cps, tokens):
    xb, xc, xd, pw = tabs
    nb, length, d = u.shape
    assert tokens == spb * cps * S5_T and length % tokens == 0
    ns = nb * spb
    ncol = 2 * S5_PAIRS
    nc_pad = -(-(nb * spb * cps) // SUBLANES) * SUBLANES
    width = S5_T * S5_PAIR
    blk = pl.BlockSpec((nb, tokens, LANES), lambda b, r: (0, r, b))
    state = pl.BlockSpec((ncol, ns, LANES), lambda b, r: (b, 0, 0))
    return pl.pallas_call(
        functools.partial(_s5_body, spb, cps),
        grid=(d // LANES, length // tokens),
        in_specs=[
            blk,
            pl.BlockSpec((S5_PAIRS, 2, S5_PAIR, LANES), lambda b, r: (b, 0, 0, 0)),
            pl.BlockSpec((S5_PAIRS, 2, S5_PAIR, LANES), lambda b, r: (b, 0, 0, 0)),
            pl.BlockSpec((S5_PAIRS, S5_PAIR, LANES), lambda b, r: (b, 0, 0)),
            pl.BlockSpec((S5_PAIRS, 2, S5_T + 1, LANES), lambda b, r: (b, 0, 0, 0)),
            state,
        ],
        out_specs=[blk, state],
        out_shape=[jax.ShapeDtypeStruct(u.shape, F32), jax.ShapeDtypeStruct(h0.shape, F32)],
        scratch_shapes=[pltpu.VMEM((ncol, nc_pad, LANES), F32), pltpu.VMEM((ncol, nc_pad, LANES), F32),
                        pltpu.VMEM((ncol, ns, LANES), F32), pltpu.VMEM((S5_PAIRS, width, width), BF16),
                        pltpu.VMEM((S5_PAIRS, width, 2 * LANES), BF16), pltpu.VMEM((S5_PAIRS, width, 2 * LANES), BF16)],
        compiler_params=_cparams("parallel", "arbitrary"),
        name="s5",
    )(u, xb, xc, xd, pw, h0)


def _s5_tables(a_re, a_im, log_dt, b_re, b_im, c_re, c_im, d_skip):
    t = S5_T
    g, p = a_re.shape
    c = b_re.shape[-1]
    dt = jnp.exp(log_dt)[:, None]
    mag = jnp.exp(a_re * dt)
    ang = a_im * dt
    ar = mag * jnp.cos(ang)
    ai = mag * jnp.sin(ang)
    num_re = ar - 1.0
    num_im = ai
    inv = 1.0 / (a_re * a_re + a_im * a_im)
    f_re = (num_re * a_re + num_im * a_im) * inv
    f_im = (num_im * a_re - num_re * a_im) * inv
    bb_re = f_re[..., None] * b_re - f_im[..., None] * b_im
    bb_im = f_re[..., None] * b_im + f_im[..., None] * b_re
    pr = jnp.ones((1, g, p), F32)
    pi = jnp.zeros((1, g, p), F32)
    sr, si = ar, ai
    while pr.shape[0] < t + 1:
        pr, pi = (jnp.concatenate([pr, pr * sr - pi * si]), jnp.concatenate([pi, pr * si + pi * sr]))
        sr, si = sr * sr - si * si, 2.0 * sr * si
    pr = pr[:t + 1]
    pi = pi[:t + 1]
    eye2 = jnp.eye(2, dtype=F32)

    def pad_pair(x):
        x = x.reshape(g // 2, 2, c, p)
        return (x[:, :, :, None, :] * eye2[None, :, None, :, None]).reshape(g // 2, 2 * c, 2 * p)

    xb = jnp.stack([pad_pair(jnp.transpose(bb_re, (0, 2, 1))), pad_pair(jnp.transpose(bb_im, (0, 2, 1)))], axis=1)
    xc = jnp.stack([pad_pair(c_re), pad_pair(c_im)], axis=1)
    xd = d_skip.reshape(g // 2, 2 * c, 1) * jnp.eye(2 * c, LANES, dtype=F32)[None]
    pw = jnp.stack([jnp.transpose(pr.reshape(t + 1, g // 2, 2 * p), (1, 0, 2)),
                    jnp.transpose(pi.reshape(t + 1, g // 2, 2 * p), (1, 0, 2))], axis=1)
    return xb, xc, xd, pw


def _state_to_cols(h_re, h_im):
    ns, g, p = h_re.shape
    f = lambda h: jnp.transpose(h.reshape(ns, g // 2, 2 * p), (1, 0, 2))
    return jnp.stack([f(h_re), f(h_im)], axis=1).reshape(g, ns, 2 * p)


def _cols_to_state(h):
    g, ns, w = h.shape
    h = jnp.transpose(h.reshape(g // 2, 2, ns, 2, w // 2), (1, 2, 0, 3, 4)).reshape(2, ns, g, w // 2)
    return h[0], h[1]


def _ffn_rows(x, g, win_ref, wout_ref):
    dff = wout_ref.shape[0]
    xn = _rms(x, g).astype(BF16)
    acc = x
    for j in range(dff // FFN_CHUNK):
        lo = j * FFN_CHUNK
        gate = _dot(xn, win_ref[:, lo:lo + FFN_CHUNK])
        up = _dot(xn, win_ref[:, dff + lo:dff + lo + FFN_CHUNK])
        a = (jax.nn.silu(gate) * up).astype(BF16)
        acc = acc + _dot(a, wout_ref[lo:lo + FFN_CHUNK, :])
    return acc


def _kv_rows(x, g_ref, w_ref, kn_ref, e_ref, et_ref, k_ref, v_ref):
    kvw = k_ref.shape[1]
    r = _dot(_rms(x, g_ref[...]).astype(BF16), w_ref[...])
    k_ref[...] = _head_rms(r[:, :kvw], e_ref[...], et_ref[...], kn_ref[...])
    v_ref[...] = r[:, kvw:]


def _ffn_body(x_ref, g_ref, win_ref, wout_ref, o_ref):
    o_ref[...] = _ffn_rows(x_ref[...], g_ref[...], win_ref, wout_ref)


def _glu_ffn_body(tail, x_ref, y_ref, wglu_ref, g_ref, win_ref, wout_ref, *rest):
    d = x_ref.shape[1]
    r = _dot(jax.nn.gelu(y_ref[...]).astype(BF16), wglu_ref[...])
    x = x_ref[...] + r[:, :d] * jax.nn.sigmoid(r[:, d:])
    out = _ffn_rows(x, g_ref[...], win_ref, wout_ref)
    if tail == "rms":
        gn_ref, o_ref, u_ref = rest
        u_ref[...] = _rms(out, gn_ref[...])
    else:
        gn_ref, wkv_ref, kn_ref, e_ref, et_ref, o_ref, k_ref, v_ref = rest
        _kv_rows(out, gn_ref, wkv_ref, kn_ref, e_ref, et_ref, k_ref, v_ref)
    o_ref[...] = out


def _ffn_call(x, g, w_in, w_out):
    n, d = x.shape
    tm = _row_tile(n, ROW_TILE)
    row = pl.BlockSpec((tm, d), lambda i: (i, 0))
    return pl.pallas_call(
        _ffn_body,
        grid=(n // tm,),
        in_specs=[row, _resident((1, d)), _resident(w_in.shape), _resident(w_out.shape)],
        out_specs=row,
        out_shape=jax.ShapeDtypeStruct((n, d), F32),
        compiler_params=_cparams("parallel"),
        name="ffn",
    )(x, g.reshape(1, d), w_in, w_out)


def _glu_ffn_call(x, y, w_glu, g, w_in, w_out, next_norm, kv=None):
    n, d = x.shape
    tm = _row_tile(n, ROW_TILE)
    row = pl.BlockSpec((tm, d), lambda i: (i, 0))
    in_specs = [row, row, _resident(w_glu.shape), _resident((1, d)), _resident(w_in.shape), _resident(w_out.shape),
                _resident((1, d))]
    args = [x, y, w_glu, g.reshape(1, d), w_in, w_out, next_norm.reshape(1, d)]
    if kv is None:
        tail, out_specs, out_shape = "rms", [row, row], [jax.ShapeDtypeStruct((n, d), F32)] * 2
    else:
        w_kv, k_norm = kv
        kvw = w_kv.shape[1] // 2
        e, et = _segment_mats(kvw // HEAD_DIM)
        kvrow = pl.BlockSpec((tm, kvw), lambda i: (i, 0))
        in_specs += [_resident(w_kv.shape), _resident((1, kvw)), _resident(e.shape), _resident(et.shape)]
        args += [w_kv, jnp.tile(k_norm, kvw // HEAD_DIM).reshape(1, kvw), e, et]
        tail, out_specs = "kv", [row, kvrow, kvrow]
        out_shape = [jax.ShapeDtypeStruct((n, d), F32)] + [jax.ShapeDtypeStruct((n, kvw), F32)] * 2
    return pl.pallas_call(
        functools.partial(_glu_ffn_body, tail),
        grid=(n // tm,),
        in_specs=in_specs,
        out_specs=out_specs,
        out_shape=out_shape,
        compiler_params=_cparams("parallel"),
        name="glu_ffn_" + tail,
    )(*args)


def _attn_body(chunk, shared, x_ref, g_ref, wq_ref, gain_ref, wo_ref, sink_ref, kown_ref, vown_ref, kleft_ref,
               vleft_ref, kmeta_ref, vmeta_ref, bband_ref, bmeta_ref, e_ref, et_ref, o_ref,
               q_scr, kd_scr, vt_scr, lg_scr, att_scr):
    rows, d = x_ref.shape
    nchunks = rows // chunk
    n_heads = d // HEAD_DIM
    q_per_kv = n_heads // N_KV_HEADS
    band = WINDOW + chunk
    qlanes = q_per_kv * chunk
    kvw = N_KV_HEADS * HEAD_DIM

    x = x_ref[...]
    q = _dot(_rms(x, g_ref[...]).astype(BF16), wq_ref[...])
    q_scr[...] = _head_rms(q, e_ref[...], et_ref[...], 1.0)

    gain = gain_ref[...]
    low = lax.broadcasted_iota(jnp.int32, (1, LANES), 1) < HEAD_DIM

    def dup(kf):
        out = []
        for kv in range(N_KV_HEADS):
            t = kf[:, (kv // 2) * LANES:(kv // 2 + 1) * LANES]
            r = pltpu.roll(t, HEAD_DIM, 1)
            out.append((jnp.where(low, t, r) if kv % 2 == 0 else jnp.where(low, r, t)).astype(BF16))
        return out

    def put_keys(row0, kf, vf):
        for kv, t in enumerate(dup(kf * gain)):
            kd_scr[kv, row0:row0 + LANES, :] = t
        vt_scr[:, row0:row0 + LANES] = vf.T.astype(BF16)

    km = dup(kmeta_ref[...] * gain)
    vmt = vmeta_ref[...].astype(BF16)
    if shared:
        put_keys(0, kleft_ref[...], vleft_ref[...])
        for t in range(rows // LANES):
            put_keys(WINDOW + t * LANES, kown_ref[t * LANES:(t + 1) * LANES, :], vown_ref[t * LANES:(t + 1) * LANES, :])
        first_chunk = pl.program_id(1) * nchunks
    else:
        zpad = jnp.zeros((LANES - chunk, kvw), F32)
        for j in range(nchunks):
            put_keys(j * 2 * LANES, kleft_ref[j], vleft_ref[j])
            put_keys(j * 2 * LANES + LANES, jnp.concatenate([kown_ref[j * chunk:(j + 1) * chunk, :], zpad], axis=0),
                     jnp.concatenate([vown_ref[j * chunk:(j + 1) * chunk, :], zpad], axis=0))

    for j in range(nchunks):
        r0 = j * chunk
        if shared:
            base = r0
            blocks = [(base + i * chunk, chunk, i * chunk, WINDOW // chunk - i) for i in range(band // chunk)]
        else:
            base = j * 2 * LANES
            blocks = [(base, WINDOW, 0, None), (base + WINDOW, chunk, WINDOW, None)]
        tile0 = (base // LANES) * LANES
        bm = bmeta_ref[j] if shared else bmeta_ref[0]
        for pair in range(N_KV_HEADS // 2):
            o_pair = []
            for kv in (2 * pair, 2 * pair + 1):
                qx = []
                for gq in range(q_per_kv):
                    h = kv * q_per_kv + gq
                    t = q_scr[r0:r0 + chunk, (h // 2) * LANES:(h // 2 + 1) * LANES]
                    qx.append(jnp.where(low if h % 2 == 0 else jnp.logical_not(low), t, 0.0))
                qx = jnp.concatenate(qx, axis=0).astype(BF16)
                s = sink_ref[kv]
                raw = _dot_nt(jnp.concatenate([kd_scr[kv, base:base + band, :], km[kv]], axis=0), qx)
                run = jnp.full((SUBLANES, qlanes), NEG, F32)
                for (k0, nk, b0, back) in blocks:
                    lg = raw[b0:b0 + nk] + bband_ref[kv, b0:b0 + nk, :]
                    if back:
                        lg = jnp.where(first_chunk + j >= back, lg, NEG)
                    lg_scr[kv, b0:b0 + nk, :] = lg
                    run = jnp.maximum(run, jnp.max(lg.reshape(nk // SUBLANES, SUBLANES, qlanes), axis=0))
                lgm = raw[band:band + N_META] + bm[kv]
                run = jnp.maximum(run, jnp.max(lgm.reshape(N_META // SUBLANES, SUBLANES, qlanes), axis=0))
                mx = jnp.maximum(jnp.max(run, axis=0, keepdims=True), s)
                pm = jnp.exp(lgm - mx)
                den = jnp.sum(pm, axis=0, keepdims=True) + jnp.exp(s - mx)
                ot = _dot(vmt[kv * HEAD_DIM:(kv + 1) * HEAD_DIM, :], pm.astype(BF16))
                parts = {}
                for (k0, nk, b0, back) in blocks:
                    p = jnp.exp(lg_scr[kv, b0:b0 + nk, :] - mx)
                    den = den + jnp.sum(p, axis=0, keepdims=True)
                    parts[k0] = p.astype(BF16)
                for t0 in (tile0, tile0 + LANES):
                    segs, pos = [], t0
                    while pos < t0 + LANES:
                        if pos in parts:
                            segs.append(parts[pos])
                            pos += parts[pos].shape[0]
                        else:
                            nxt = min([k0 for k0 in parts if k0 > pos] + [t0 + LANES])
                            segs.append(jnp.zeros((nxt - pos, qlanes), BF16))
                            pos = nxt
                    pt = segs[0] if len(segs) == 1 else jnp.concatenate(segs, axis=0)
                    ot = ot + _dot(vt_scr[kv * HEAD_DIM:(kv + 1) * HEAD_DIM, t0:t0 + LANES], pt)
                o_pair.append(ot / den)
            o2 = jnp.concatenate(o_pair, axis=0).T
            for gq in range(q_per_kv):
                tile = pair * q_per_kv + gq
                att_scr[r0:r0 + chunk, tile * LANES:(tile + 1) * LANES] = o2[gq * chunk:(gq + 1) * chunk, :]

    o_ref[...] = x + _dot(att_scr[...].astype(BF16), wo_ref[...])


def _attn_call(x, g, w_q, q_norm, w_o, sinks, k_own, v_own, k_left, v_left, k_meta, v_meta, b_band, b_meta,
               chunk, shared, nseq, tiles_per_seq, chunks_per_tile):
    n, d = x.shape
    kvw = k_own.shape[1]
    n_heads = d // HEAD_DIM
    q_per_kv = n_heads // N_KV_HEADS
    rows = chunks_per_tile * chunk
    band = WINDOW + chunk
    qlanes = q_per_kv * chunk
    assert rows % LANES == 0 and LANES % chunk == 0
    key_rows = WINDOW + rows if shared else chunks_per_tile * 2 * LANES
    e, et = _segment_mats(n_heads)
    gain = jnp.tile(q_norm * (HEAD_DIM ** -0.5), N_KV_HEADS).reshape(1, kvw)
    sink_row = jnp.broadcast_to(sinks.reshape(N_KV_HEADS, 1, q_per_kv, 1),
                                (N_KV_HEADS, 1, q_per_kv, chunk)).reshape(N_KV_HEADS, 1, qlanes)
    w_o = jnp.concatenate([w_o[((2 * pair + kv2) * q_per_kv + gq) * HEAD_DIM:((2 * pair + kv2) * q_per_kv + gq + 1) * HEAD_DIM]
                           for pair in range(N_KV_HEADS // 2) for gq in range(q_per_kv) for kv2 in range(2)], axis=0)
    row_idx = lambda b, i: (b * tiles_per_seq + i, 0)
    if shared:
        halo_blocks = rows // WINDOW
        left_spec = pl.BlockSpec(
            (WINDOW, kvw), lambda b, i: (jnp.maximum(b * tiles_per_seq * halo_blocks + i * halo_blocks - 1, 0), 0))
        last_tile = b_meta.shape[0] // chunks_per_tile - 1
        bmeta_spec = pl.BlockSpec((chunks_per_tile,) + b_meta.shape[1:],
                                  lambda b, i: (jnp.minimum(i, last_tile), 0, 0, 0))
    else:
        left_spec = pl.BlockSpec((chunks_per_tile, WINDOW, kvw), lambda b, i: (b * tiles_per_seq + i, 0, 0))
        bmeta_spec = _resident(b_meta.shape)
    return pl.pallas_call(
        functools.partial(_attn_body, chunk, shared),
        grid=(nseq, tiles_per_seq),
        in_specs=[
            pl.BlockSpec((rows, d), row_idx), _resident((1, d)), _resident(w_q.shape), _resident((1, kvw)),
            _resident(w_o.shape), _resident(sink_row.shape),
            pl.BlockSpec((rows, kvw), row_idx), pl.BlockSpec((rows, kvw), row_idx), left_spec, left_spec,
            _resident(k_meta.shape), _resident((kvw, N_META)), _resident(b_band.shape), bmeta_spec,
            _resident(e.shape), _resident(et.shape),
        ],
        out_specs=pl.BlockSpec((rows, d), row_idx),
        out_shape=jax.ShapeDtypeStruct((n, d), F32),
        scratch_shapes=[pltpu.VMEM((rows, d), F32), pltpu.VMEM((N_KV_HEADS, key_rows, LANES), BF16),
                        pltpu.VMEM((kvw, key_rows), BF16), pltpu.VMEM((N_KV_HEADS, band, qlanes), F32),
                        pltpu.VMEM((rows, d), F32)],
        compiler_params=_cparams("parallel", "arbitrary"),
        name="attn_prompt" if shared else "attn_sample",
    )(x, g.reshape(1, d), w_q, gain, w_o, sink_row, k_own, v_own, k_left, v_left, k_meta, v_meta.T, b_band, b_meta,
      e, et)


def _rel_bucket(rel):
    half = N_BUCKETS // 2
    max_exact = half // 2
    n = jnp.abs(rel)
    nf = jnp.maximum(n, 1).astype(F32)
    large = max_exact + (jnp.log(nf / max_exact) / math.log(MAX_DISTANCE / max_exact)
                         * (half - max_exact)).astype(jnp.int32)
    large = jnp.minimum(large, half - 1)
    return jnp.where(rel > 0, half, 0) + jnp.where(n < max_exact, n, large)


def _bias_lookup(rel_bias, bucket):
    onehot = (bucket[..., None] == jnp.arange(N_BUCKETS)).astype(F32)
    return jnp.dot(onehot, rel_bias, precision=lax.Precision.HIGHEST)


def _band_bias(rel_bias, chunk):
    n_heads = rel_bias.shape[1]
    q_per_kv = n_heads // N_KV_HEADS
    band = WINDOW + chunk
    rel = jnp.arange(band)[None, :] - WINDOW - jnp.arange(chunk)[:, None]
    b = _bias_lookup(rel_bias, _rel_bucket(rel)).reshape(chunk, band, N_KV_HEADS, q_per_kv)
    return jnp.transpose(b, (2, 1, 3, 0)).reshape(N_KV_HEADS, band, q_per_kv * chunk)


def _meta_bias(t0, count, rel_bias, chunk):
    n_heads = rel_bias.shape[1]
    q_per_kv = n_heads // N_KV_HEADS
    dist = jnp.arange(t0 + 1, t0 + count + N_META + 1)
    tab = _bias_lookup(rel_bias, _rel_bucket(-dist)).T
    b = jnp.stack([tab[:, N_META - 1 - m:N_META - 1 - m + count] for m in range(N_META)])
    b = b.reshape(N_META, N_KV_HEADS, q_per_kv, count // chunk, chunk)
    return jnp.transpose(b, (3, 1, 0, 2, 4)).reshape(count // chunk, N_KV_HEADS, N_META, q_per_kv * chunk)


def _meta_bias_tiles(seq, rel_bias, chunk, tile_rows):
    tiles = min(seq // tile_rows, 2)
    assert tiles == seq // tile_rows or (tiles - 1) * tile_rows + 1 >= MAX_DISTANCE
    return _meta_bias(0, tiles * tile_rows, rel_bias, chunk)


def kernel(x_prompt, x_sample, state_ssm_re, state_ssm_im, cache_k, cache_v, meta_tokens, norm_mix, norm_ffn,
           ssm_a_re, ssm_a_im, ssm_log_dt, ssm_b_re, ssm_b_im, ssm_c_re, ssm_c_im, ssm_d, w_glu, w_ffn_in,
           w_ffn_out, norm_kv, w_kv, k_norm, w_q, q_norm, attn_sinks, w_o, rel_bias):
    bsz, seq, d = x_prompt.shape
    dbsz, dseq, _ = x_sample.shape
    n_a = ssm_a_re.shape[0]
    n_b = w_q.shape[0]
    n_groups, n_state = ssm_a_re.shape[1:]
    kvw = N_KV_HEADS * HEAD_DIM

    cast = lambda w: [w[i].astype(BF16) for i in range(w.shape[0])]
    w_glu_b, w_in_b, w_out_b, w_q_b, w_o_b = cast(w_glu), cast(w_ffn_in), cast(w_ffn_out), cast(w_q), cast(w_o)
    w_kv_b = w_kv.astype(BF16)
    tabs = [_s5_tables(ssm_a_re[l], ssm_a_im[l], ssm_log_dt[l], ssm_b_re[l], ssm_b_im[l], ssm_c_re[l],
                       ssm_c_im[l], ssm_d[l]) for l in range(n_a)]

    def s5_layers(x, view, h0_of_layer, spb, cps, tokens):
        states = []
        u = _rms_call(x, norm_mix[0])
        for l in range(n_a):
            y, hf = _s5_call(u.reshape(view), tabs[l], h0_of_layer(l), spb=spb, cps=cps, tokens=tokens)
            states.append(_cols_to_state(hf))
            args = (x, y.reshape(x.shape), w_glu_b[l], norm_ffn[l], w_in_b[l], w_out_b[l])
            if l + 1 < n_a:
                x, u = _glu_ffn_call(*args, norm_mix[l + 1])
            else:
                x, k, v = _glu_ffn_call(*args, norm_kv, kv=(w_kv_b, k_norm))
        return x, k, v, states

    n_s = dbsz * dseq
    n_small = n_s + dseq
    xs = jnp.concatenate([x_sample.reshape(n_s, d), jnp.zeros((dseq - N_META, d), F32), meta_tokens], axis=0)
    zero_state = jnp.zeros((1, n_groups, n_state), F32)
    h0_small = lambda l: _state_to_cols(jnp.concatenate([state_ssm_re[l], zero_state], axis=0),
                                        jnp.concatenate([state_ssm_im[l], zero_state], axis=0))
    xs, k_small, v_small, st_small = s5_layers(xs, (1, n_small, d), h0_small, dbsz + 1, dseq // S5_T, n_small)
    s_re = [hr[:dbsz] for hr, _ in st_small]
    s_im = [hi[:dbsz] for _, hi in st_small]
    k_s, v_s = k_small[:n_s], v_small[:n_s]
    k_meta, v_meta = k_small[n_small - N_META:], v_small[n_small - N_META:]

    bband_s = _band_bias(rel_bias, dseq)
    bmeta_s = _meta_bias(PAST_LEN, dseq, rel_bias, dseq)
    xq = xs[:n_s]
    ck = cache_k.reshape(dbsz, WINDOW, kvw)
    cv = cache_v.reshape(dbsz, WINDOW, kvw)
    for j in range(n_b):
        l = n_a + j
        xq = _attn_call(xq, norm_mix[l], w_q_b[j], q_norm[j], w_o_b[j], attn_sinks[j], k_s, v_s, ck, cv, k_meta,
                        v_meta, bband_s, bmeta_s, chunk=dseq, shared=False, nseq=1,
                        tiles_per_seq=dbsz // ATTN_SEQS_SAMPLE, chunks_per_tile=ATTN_SEQS_SAMPLE)
        xq = _ffn_call(xq, norm_ffn[l], w_in_b[l], w_out_b[l])
    y_sample = xq.reshape(dbsz, dseq, d)

    s5_tokens = min(S5_TOKENS, seq)
    h0_prompt = lambda l: _state_to_cols(jnp.broadcast_to(st_small[l][0][dbsz:], (bsz, n_groups, n_state)),
                                         jnp.broadcast_to(st_small[l][1][dbsz:], (bsz, n_groups, n_state)))
    xp, k_p, v_p, st_prompt = s5_layers(x_prompt.reshape(bsz * seq, d), (bsz, seq, d), h0_prompt, 1,
                                        s5_tokens // S5_T, s5_tokens)
    p_re = [hr for hr, _ in st_prompt]
    p_im = [hi for _, hi in st_prompt]

    bband_p = _band_bias(rel_bias, CHUNK)
    tile_rows = ATTN_CHUNKS_PROMPT * CHUNK
    bmeta_p = _meta_bias_tiles(seq, rel_bias, CHUNK, tile_rows)
    for j in range(n_b):
        l = n_a + j
        xp = _attn_call(xp, norm_mix[l], w_q_b[j], q_norm[j], w_o_b[j], attn_sinks[j], k_p, v_p, k_p, v_p, k_meta,
                        v_meta, bband_p, bmeta_p, chunk=CHUNK, shared=True, nseq=bsz,
                        tiles_per_seq=seq // tile_rows, chunks_per_tile=ATTN_CHUNKS_PROMPT)
        xp = _ffn_call(xp, norm_ffn[l], w_in_b[l], w_out_b[l])
    y_prompt = xp.reshape(bsz, seq, d)

    k_p4 = k_p.reshape(bsz, seq, N_KV_HEADS, HEAD_DIM)
    v_p4 = v_p.reshape(bsz, seq, N_KV_HEADS, HEAD_DIM)
    return (y_prompt, y_sample, jnp.stack(p_re), jnp.stack(p_im), k_p4[:, -WINDOW:], v_p4[:, -WINDOW:],
            jnp.stack(s_re), jnp.stack(s_im), k_s.reshape(dbsz, dseq, N_KV_HEADS, HEAD_DIM),
            v_s.reshape(dbsz, dseq, N_KV_HEADS, HEAD_DIM))
```

```python
import functools
import math

import jax
import jax.numpy as jnp
from jax import lax
from jax.experimental import pallas as pl
from jax.experimental.pallas import tpu as pltpu

F32 = jnp.float32
BF16 = jnp.bfloat16

CHUNK = 64
N_META = 16
SSM_GROUP = 16
SSM_STATE = 64
HEAD_DIM = 64
N_KV_HEADS = 4
WINDOW = 128
N_BUCKETS = 32
MAX_DISTANCE = 128
PAST_LEN = 1024
EPS = 1e-6
NEG = -1e30

LANES = 128
SUBLANES = 8
VMEM_LIMIT_BYTES = 56 * 1024 * 1024

S5_T = 16
S5_PAIR = 2 * SSM_GROUP
S5_PAIRS = LANES // S5_PAIR
S5_TOKENS = 4096
S5_SCAN_UNROLL = 4
ROW_TILE = 512
FFN_CHUNK = 256
ATTN_CHUNKS_PROMPT = 8
ATTN_SEQS_SAMPLE = 8


def _row_tile(n, target):
    if n <= target:
        return n
    best = None
    for t in range(SUBLANES, target + 1, SUBLANES):
        if n % t == 0:
            best = t
    assert best is not None, n
    return best


def _cparams(*sem):
    return pltpu.CompilerParams(dimension_semantics=sem, vmem_limit_bytes=VMEM_LIMIT_BYTES)


def _resident(shape):
    nd = len(shape)
    return pl.BlockSpec(shape, lambda *_: (0,) * nd, pipeline_mode=pl.Buffered(1))


def _rms(x, g):
    ms = jnp.mean(x * x, axis=-1, keepdims=True)
    return (x * lax.rsqrt(ms + EPS)) * g


def _dot(a, b):
    return jnp.dot(a, b, preferred_element_type=F32)


def _dot_nt(a, b):
    return lax.dot_general(a, b, (((1,), (1,)), ((), ())), preferred_element_type=F32)


def _head_rms(x, e, et, gain):
    ss = _dot((x * x).astype(BF16), e)
    rinv = lax.rsqrt(ss * (1.0 / HEAD_DIM) + EPS)
    hi = rinv.astype(BF16)
    lo = (rinv - hi.astype(F32)).astype(BF16)
    rb = _dot(hi, et) + _dot(lo, et)
    return (x * rb) * gain


def _segment_mats(n_heads):
    idx = jnp.arange(n_heads * HEAD_DIM) // HEAD_DIM
    e = (idx[:, None] == jnp.arange(LANES)[None, :]).astype(BF16)
    return e, e.T


def _rms_body(x_ref, g_ref, o_ref):
    o_ref[...] = _rms(x_ref[...], g_ref[...])


def _rms_call(x, g):
    n, d = x.shape
    tm = _row_tile(n, 1024)
    return pl.pallas_call(
        _rms_body,
        grid=(n // tm,),
        in_specs=[pl.BlockSpec((tm, d), lambda i: (i, 0)), _resident((1, d))],
        out_specs=pl.BlockSpec((tm, d), lambda i: (i, 0)),
        out_shape=jax.ShapeDtypeStruct((n, d), F32),
        compiler_params=_cparams("parallel"),
        name="rms",
    )(x, g.reshape(1, d))


def _xpose4(a, lo64, lo32):
    a0, a1, a2, a3 = a
    r02 = pltpu.roll(jnp.where(lo64, a2, a0), 64, 1)
    r13 = pltpu.roll(jnp.where(lo64, a3, a1), 64, 1)
    b0 = jnp.where(lo64, a0, r02)
    b2 = jnp.where(lo64, r02, a2)
    b1 = jnp.where(lo64, a1, r13)
    b3 = jnp.where(lo64, r13, a3)
    return (jnp.where(lo32, b0, pltpu.roll(b1, 32, 1)), jnp.where(lo32, pltpu.roll(b0, 96, 1), b1),
            jnp.where(lo32, b2, pltpu.roll(b3, 32, 1)), jnp.where(lo32, pltpu.roll(b2, 96, 1), b3))


def _s5_body(spb, cps, u_ref, xb_ref, xc_ref, xd_ref, pw_ref, h0_ref, y_ref, hf_ref,
             s_scr, hin_scr, h_scr, m_scr, wo_scr, wit_scr):
    nb = u_ref.shape[0]
    t = S5_T
    rows_b = spb * cps
    nc = nb * rows_b
    ns = nb * spb
    ncol = 2 * S5_PAIRS
    width = t * S5_PAIR

    @pl.when(pl.program_id(1) == 0)
    def _():
        h_scr[...] = h0_ref[...]
        strip_lane = lax.broadcasted_iota(jnp.int32, (S5_PAIR, width), 1)
        for p in range(S5_PAIRS):
            br, bi = xb_ref[p, 0], xb_ref[p, 1]
            cr, ci = xc_ref[p, 0], xc_ref[p, 1]
            cw = []
            for s in range(t + 1):
                ar, ai = pw_ref[p, 0, s:s + 1, :], pw_ref[p, 1, s:s + 1, :]
                cw.append(jnp.concatenate([cr * ar - ci * ai, -(cr * ai + ci * ar)], axis=1))
            for s in range(t):
                rs = slice(S5_PAIR * s, S5_PAIR * (s + 1))
                ar, ai = pw_ref[p, 0, t - 1 - s:t - s, :], pw_ref[p, 1, t - 1 - s:t - s, :]
                wo_scr[p, rs, 0:LANES] = (br * ar - bi * ai).astype(BF16)
                wo_scr[p, rs, LANES:2 * LANES] = (br * ai + bi * ar).astype(BF16)
                wit_scr[p, rs, :] = cw[s + 1].astype(BF16)
            xbb = jnp.concatenate([br, bi], axis=1)
            cwt = jnp.concatenate(cw[:t], axis=0)
            x_hi, c_hi = xbb.astype(BF16), cwt.astype(BF16)
            x_lo, c_lo = (xbb - x_hi.astype(F32)).astype(BF16), (cwt - c_hi.astype(F32)).astype(BF16)
            strip = _dot_nt(x_hi, c_hi) + (_dot_nt(x_hi, c_lo) + _dot_nt(x_lo, c_hi))
            strip = strip + jnp.concatenate([xd_ref[p], jnp.zeros((S5_PAIR, width - LANES), F32)], axis=1)
            for s in range(t):
                blk = strip if s == 0 else jnp.where(strip_lane >= S5_PAIR * s, pltpu.roll(strip, S5_PAIR * s, 1), 0.0)
                m_scr[p, S5_PAIR * s:S5_PAIR * (s + 1), :] = blk.astype(BF16)

    lane = lax.broadcasted_iota(jnp.int32, (nc, LANES), 1)
    lo64 = lane < 64
    lo32 = (lane % 64) < 32

    v = []
    for tok in range(t):
        parts = [u_ref[b, pl.ds(tok, rows_b, stride=t), :] for b in range(nb)]
        v.append(parts[0] if nb == 1 else jnp.concatenate(parts, axis=0))
    quads = [_xpose4(v[4 * tq:4 * tq + 4], lo64, lo32) for tq in range(t // 4)]
    u_p = [jnp.concatenate([quads[tq][p] for tq in range(t // 4)], axis=1).astype(BF16) for p in range(S5_PAIRS)]

    if S5_PAIRS * ns == SUBLANES:
        for p in range(S5_PAIRS):
            s = _dot(u_p[p], wo_scr[p])
            for q in range(ns):
                rows_q = pl.ds(p * ns + q, cps, stride=SUBLANES)
                s_scr[0, rows_q, :] = s[q * cps:(q + 1) * cps, :LANES]
                s_scr[1, rows_q, :] = s[q * cps:(q + 1) * cps, LANES:]
        ar = jnp.concatenate([jnp.broadcast_to(pw_ref[p, 0, t:t + 1, :], (ns, LANES)) for p in range(S5_PAIRS)], axis=0)
        ai = jnp.concatenate([jnp.broadcast_to(pw_ref[p, 1, t:t + 1, :], (ns, LANES)) for p in range(S5_PAIRS)], axis=0)

        def step(k, carry):
            cr, ci = carry
            rows_k = pl.ds(pl.multiple_of(k * SUBLANES, SUBLANES), SUBLANES)
            hin_scr[0, rows_k, :] = cr
            hin_scr[1, rows_k, :] = ci
            return ar * cr - ai * ci + s_scr[0, rows_k, :], ar * ci + ai * cr + s_scr[1, rows_k, :]

        cr, ci = lax.fori_loop(0, cps, step, (jnp.concatenate([h_scr[2 * p] for p in range(S5_PAIRS)], axis=0),
                                              jnp.concatenate([h_scr[2 * p + 1] for p in range(S5_PAIRS)], axis=0)),
                               unroll=min(cps, S5_SCAN_UNROLL))
        carry = []
        for p in range(S5_PAIRS):
            carry += [cr[p * ns:(p + 1) * ns], ci[p * ns:(p + 1) * ns]]

        def chunk_states(p, plane):
            return jnp.concatenate([hin_scr[plane, pl.ds(p * ns + q, cps, stride=SUBLANES), :] for q in range(ns)],
                                   axis=0)
    else:
        for p in range(S5_PAIRS):
            s = _dot(u_p[p], wo_scr[p])
            s_scr[2 * p, 0:nc, :] = s[:, :LANES]
            s_scr[2 * p + 1, 0:nc, :] = s[:, LANES:]

        def step(k, carry):
            new = []
            for p in range(S5_PAIRS):
                cr, ci = carry[2 * p], carry[2 * p + 1]
                hin_scr[2 * p, pl.ds(k, ns, stride=cps), :] = cr
                hin_scr[2 * p + 1, pl.ds(k, ns, stride=cps), :] = ci
                ar = pw_ref[p, 0, t:t + 1, :]
                ai = pw_ref[p, 1, t:t + 1, :]
                tr = s_scr[2 * p, pl.ds(k, ns, stride=cps), :]
                ti = s_scr[2 * p + 1, pl.ds(k, ns, stride=cps), :]
                new += [ar * cr - ai * ci + tr, ar * ci + ai * cr + ti]
            return tuple(new)

        carry = lax.fori_loop(0, cps, step, tuple(h_scr[c] for c in range(ncol)), unroll=min(cps, S5_SCAN_UNROLL))

        def chunk_states(p, plane):
            return hin_scr[2 * p + plane, 0:nc, :]

    for c in range(ncol):
        h_scr[c] = carry[c]
        hf_ref[c] = carry[c]

    y_p = []
    for p in range(S5_PAIRS):
        h = jnp.concatenate([chunk_states(p, 0), chunk_states(p, 1)], axis=1).astype(BF16)
        y_p.append(_dot(u_p[p], m_scr[p]) + _dot_nt(h, wit_scr[p]))
    for tq in range(t // 4):
        w = _xpose4([y_p[p][:, LANES * tq:LANES * (tq + 1)] for p in range(S5_PAIRS)], lo64, lo32)
        for i in range(4):
            for b in range(nb):
                y_ref[b, pl.ds(4 * tq + i, rows_b, stride=t), :] = w[i][b * rows_b:(b + 1) * rows_b]


def _s5_call(u, tabs, h0, spb, cps, tokens):
    xb, xc, xd, pw = tabs
    nb, length, d = u.shape
    assert tokens == spb * cps * S5_T and length % tokens == 0
    ns = nb * spb
    ncol = 2 * S5_PAIRS
    nc_pad = -(-(nb * spb * cps) // SUBLANES) * SUBLANES
    scan_shape = (2, cps * SUBLANES, LANES) if S5_PAIRS * ns == SUBLANES else (ncol, nc_pad, LANES)
    width = S5_T * S5_PAIR
    blk = pl.BlockSpec((nb, tokens, LANES), lambda b, r: (0, r, b))
    state = pl.BlockSpec((ncol, ns, LANES), lambda b, r: (b, 0, 0))
    return pl.pallas_call(
        functools.partial(_s5_body, spb, cps),
        grid=(d // LANES, length // tokens),
        in_specs=[
            blk,
            pl.BlockSpec((S5_PAIRS, 2, S5_PAIR, LANES), lambda b, r: (b, 0, 0, 0)),
            pl.BlockSpec((S5_PAIRS, 2, S5_PAIR, LANES), lambda b, r: (b, 0, 0, 0)),
            pl.BlockSpec((S5_PAIRS, S5_PAIR, LANES), lambda b, r: (b, 0, 0)),
            pl.BlockSpec((S5_PAIRS, 2, S5_T + 1, LANES), lambda b, r: (b, 0, 0, 0)),
            state,
        ],
        out_specs=[blk, state],
        out_shape=[jax.ShapeDtypeStruct(u.shape, F32), jax.ShapeDtypeStruct(h0.shape, F32)],
        scratch_shapes=[pltpu.VMEM(scan_shape, F32), pltpu.VMEM(scan_shape, F32),
                        pltpu.VMEM((ncol, ns, LANES), F32), pltpu.VMEM((S5_PAIRS, width, width), BF16),
                        pltpu.VMEM((S5_PAIRS, width, 2 * LANES), BF16), pltpu.VMEM((S5_PAIRS, width, 2 * LANES), BF16)],
        compiler_params=_cparams("parallel", "arbitrary"),
        name="s5",
    )(u, xb, xc, xd, pw, h0)


def _s5_tables(a_re, a_im, log_dt, b_re, b_im, c_re, c_im, d_skip):
    t = S5_T
    g, p = a_re.shape
    c = b_re.shape[-1]
    dt = jnp.exp(log_dt)[:, None]
    mag = jnp.exp(a_re * dt)
    ang = a_im * dt
    ar = mag * jnp.cos(ang)
    ai = mag * jnp.sin(ang)
    num_re = ar - 1.0
    num_im = ai
    inv = 1.0 / (a_re * a_re + a_im * a_im)
    f_re = (num_re * a_re + num_im * a_im) * inv
    f_im = (num_im * a_re - num_re * a_im) * inv
    bb_re = f_re[..., None] * b_re - f_im[..., None] * b_im
    bb_im = f_re[..., None] * b_im + f_im[..., None] * b_re
    pr = jnp.ones((1, g, p), F32)
    pi = jnp.zeros((1, g, p), F32)
    sr, si = ar, ai
    while pr.shape[0] < t + 1:
        pr, pi = (jnp.concatenate([pr, pr * sr - pi * si]), jnp.concatenate([pi, pr * si + pi * sr]))
        sr, si = sr * sr - si * si, 2.0 * sr * si
    pr = pr[:t + 1]
    pi = pi[:t + 1]
    eye2 = jnp.eye(2, dtype=F32)

    def pad_pair(x):
        x = x.reshape(g // 2, 2, c, p)
        return (x[:, :, :, None, :] * eye2[None, :, None, :, None]).reshape(g // 2, 2 * c, 2 * p)

    xb = jnp.stack([pad_pair(jnp.transpose(bb_re, (0, 2, 1))), pad_pair(jnp.transpose(bb_im, (0, 2, 1)))], axis=1)
    xc = jnp.stack([pad_pair(c_re), pad_pair(c_im)], axis=1)
    xd = d_skip.reshape(g // 2, 2 * c, 1) * jnp.eye(2 * c, LANES, dtype=F32)[None]
    pw = jnp.stack([jnp.transpose(pr.reshape(t + 1, g // 2, 2 * p), (1, 0, 2)),
                    jnp.transpose(pi.reshape(t + 1, g // 2, 2 * p), (1, 0, 2))], axis=1)
    return xb, xc, xd, pw


def _state_to_cols(h_re, h_im):
    ns, g, p = h_re.shape
    f = lambda h: jnp.transpose(h.reshape(ns, g // 2, 2 * p), (1, 0, 2))
    return jnp.stack([f(h_re), f(h_im)], axis=1).reshape(g, ns, 2 * p)


def _cols_to_state(h):
    g, ns, w = h.shape
    h = jnp.transpose(h.reshape(g // 2, 2, ns, 2, w // 2), (1, 2, 0, 3, 4)).reshape(2, ns, g, w // 2)
    return h[0], h[1]


def _ffn_rows(x, g, win_ref, wout_ref):
    dff = wout_ref.shape[0]
    xn = _rms(x, g).astype(BF16)
    acc = x
    for j in range(dff // FFN_CHUNK):
        lo = j * FFN_CHUNK
        gate = _dot(xn, win_ref[:, lo:lo + FFN_CHUNK])
        up = _dot(xn, win_ref[:, dff + lo:dff + lo + FFN_CHUNK])
        a = (jax.nn.silu(gate) * up).astype(BF16)
        acc = acc + _dot(a, wout_ref[lo:lo + FFN_CHUNK, :])
    return acc


def _kv_rows(x, g_ref, w_ref, kn_ref, e_ref, et_ref, k_ref, v_ref):
    kvw = k_ref.shape[1]
    r = _dot(_rms(x, g_ref[...]).astype(BF16), w_ref[...])
    k_ref[...] = _head_rms(r[:, :kvw], e_ref[...], et_ref[...], kn_ref[...])
    v_ref[...] = r[:, kvw:]


def _ffn_body(x_ref, g_ref, win_ref, wout_ref, o_ref):
    o_ref[...] = _ffn_rows(x_ref[...], g_ref[...], win_ref, wout_ref)


def _glu_ffn_body(tail, x_ref, y_ref, wglu_ref, g_ref, win_ref, wout_ref, *rest):
    d = x_ref.shape[1]
    r = _dot(jax.nn.gelu(y_ref[...]).astype(BF16), wglu_ref[...])
    x = x_ref[...] + r[:, :d] * jax.nn.sigmoid(r[:, d:])
    out = _ffn_rows(x, g_ref[...], win_ref, wout_ref)
    if tail == "rms":
        gn_ref, o_ref, u_ref = rest
        u_ref[...] = _rms(out, gn_ref[...])
    else:
        gn_ref, wkv_ref, kn_ref, e_ref, et_ref, o_ref, k_ref, v_ref = rest
        _kv_rows(out, gn_ref, wkv_ref, kn_ref, e_ref, et_ref, k_ref, v_ref)
    o_ref[...] = out


def _ffn_call(x, g, w_in, w_out):
    n, d = x.shape
    tm = _row_tile(n, ROW_TILE)
    row = pl.BlockSpec((tm, d), lambda i: (i, 0))
    return pl.pallas_call(
        _ffn_body,
        grid=(n // tm,),
        in_specs=[row, _resident((1, d)), _resident(w_in.shape), _resident(w_out.shape)],
        out_specs=row,
        out_shape=jax.ShapeDtypeStruct((n, d), F32),
        compiler_params=_cparams("parallel"),
        name="ffn",
    )(x, g.reshape(1, d), w_in, w_out)


def _glu_ffn_call(x, y, w_glu, g, w_in, w_out, next_norm, kv=None):
    n, d = x.shape
    tm = _row_tile(n, ROW_TILE)
    row = pl.BlockSpec((tm, d), lambda i: (i, 0))
    in_specs = [row, row, _resident(w_glu.shape), _resident((1, d)), _resident(w_in.shape), _resident(w_out.shape),
                _resident((1, d))]
    args = [x, y, w_glu, g.reshape(1, d), w_in, w_out, next_norm.reshape(1, d)]
    if kv is None:
        tail, out_specs, out_shape = "rms", [row, row], [jax.ShapeDtypeStruct((n, d), F32)] * 2
    else:
        w_kv, k_norm = kv
        kvw = w_kv.shape[1] // 2
        e, et = _segment_mats(kvw // HEAD_DIM)
        kvrow = pl.BlockSpec((tm, kvw), lambda i: (i, 0))
        in_specs += [_resident(w_kv.shape), _resident((1, kvw)), _resident(e.shape), _resident(et.shape)]
        args += [w_kv, jnp.tile(k_norm, kvw // HEAD_DIM).reshape(1, kvw), e, et]
        tail, out_specs = "kv", [row, kvrow, kvrow]
        out_shape = [jax.ShapeDtypeStruct((n, d), F32)] + [jax.ShapeDtypeStruct((n, kvw), F32)] * 2
    return pl.pallas_call(
        functools.partial(_glu_ffn_body, tail),
        grid=(n // tm,),
        in_specs=in_specs,
        out_specs=out_specs,
        out_shape=out_shape,
        compiler_params=_cparams("parallel"),
        name="glu_ffn_" + tail,
    )(*args)


def _attn_body(chunk, shared, x_ref, g_ref, wq_ref, gain_ref, wo_ref, sink_ref, kown_ref, vown_ref, kleft_ref,
               vleft_ref, kmeta_ref, vmeta_ref, bband_ref, bmeta_ref, e_ref, et_ref, o_ref,
               q_scr, kd_scr, vt_scr, lg_scr, att_scr):
    rows, d = x_ref.shape
    nchunks = rows // chunk
    n_heads = d // HEAD_DIM
    q_per_kv = n_heads // N_KV_HEADS
    band = WINDOW + chunk
    qlanes = q_per_kv * chunk
    kvw = N_KV_HEADS * HEAD_DIM

    x = x_ref[...]
    q = _dot(_rms(x, g_ref[...]).astype(BF16), wq_ref[...])
    q_scr[...] = _head_rms(q, e_ref[...], et_ref[...], 1.0)

    gain = gain_ref[...]
    low = lax.broadcasted_iota(jnp.int32, (1, LANES), 1) < HEAD_DIM

    def dup(kf):
        out = []
        for kv in range(N_KV_HEADS):
            t = kf[:, (kv // 2) * LANES:(kv // 2 + 1) * LANES]
            r = pltpu.roll(t, HEAD_DIM, 1)
            out.append((jnp.where(low, t, r) if kv % 2 == 0 else jnp.where(low, r, t)).astype(BF16))
        return out

    def put_keys(row0, kf, vf):
        for kv, t in enumerate(dup(kf * gain)):
            kd_scr[kv, row0:row0 + LANES, :] = t
        vt_scr[:, row0:row0 + LANES] = vf.T.astype(BF16)

    km = dup(kmeta_ref[...] * gain)
    vmt = vmeta_ref[...].astype(BF16)
    if shared:
        put_keys(0, kleft_ref[...], vleft_ref[...])
        for t in range(rows // LANES):
            put_keys(WINDOW + t * LANES, kown_ref[t * LANES:(t + 1) * LANES, :], vown_ref[t * LANES:(t + 1) * LANES, :])
        first_chunk = pl.program_id(1) * nchunks
    else:
        zpad = jnp.zeros((LANES - chunk, kvw), F32)
        for j in range(nchunks):
            put_keys(j * 2 * LANES, kleft_ref[j], vleft_ref[j])
            put_keys(j * 2 * LANES + LANES, jnp.concatenate([kown_ref[j * chunk:(j + 1) * chunk, :], zpad], axis=0),
                     jnp.concatenate([vown_ref[j * chunk:(j + 1) * chunk, :], zpad], axis=0))

    for j in range(nchunks):
        r0 = j * chunk
        if shared:
            base = r0
            blocks = [(base + i * chunk, chunk, i * chunk, WINDOW // chunk - i) for i in range(band // chunk)]
        else:
            base = j * 2 * LANES
            blocks = [(base, WINDOW, 0, None), (base + WINDOW, chunk, WINDOW, None)]
        tile0 = (base // LANES) * LANES
        bm = bmeta_ref[j] if shared else bmeta_ref[0]
        for pair in range(N_KV_HEADS // 2):
            o_pair = []
            for kv in (2 * pair, 2 * pair + 1):
                qx = []
                for gq in range(q_per_kv):
                    h = kv * q_per_kv + gq
                    t = q_scr[r0:r0 + chunk, (h // 2) * LANES:(h // 2 + 1) * LANES]
                    qx.append(jnp.where(low if h % 2 == 0 else jnp.logical_not(low), t, 0.0))
                qx = jnp.concatenate(qx, axis=0).astype(BF16)
                s = sink_ref[kv]
                raw = _dot_nt(jnp.concatenate([kd_scr[kv, base:base + band, :], km[kv]], axis=0), qx)
                run = jnp.full((SUBLANES, qlanes), NEG, F32)
                for (k0, nk, b0, back) in blocks:
                    lg = raw[b0:b0 + nk] + bband_ref[kv, b0:b0 + nk, :]
                    if back:
                        lg = jnp.where(first_chunk + j >= back, lg, NEG)
                    lg_scr[kv, b0:b0 + nk, :] = lg
                    run = jnp.maximum(run, jnp.max(lg.reshape(nk // SUBLANES, SUBLANES, qlanes), axis=0))
                lgm = raw[band:band + N_META] + bm[kv]
                run = jnp.maximum(run, jnp.max(lgm.reshape(N_META // SUBLANES, SUBLANES, qlanes), axis=0))
                mx = jnp.maximum(jnp.max(run, axis=0, keepdims=True), s)
                pm = jnp.exp(lgm - mx)
                den = jnp.sum(pm, axis=0, keepdims=True) + jnp.exp(s - mx)
                ot = _dot(vmt[kv * HEAD_DIM:(kv + 1) * HEAD_DIM, :], pm.astype(BF16))
                parts = {}
                for (k0, nk, b0, back) in blocks:
                    p = jnp.exp(lg_scr[kv, b0:b0 + nk, :] - mx)
                    den = den + jnp.sum(p, axis=0, keepdims=True)
                    parts[k0] = p.astype(BF16)
                for t0 in (tile0, tile0 + LANES):
                    segs, pos = [], t0
                    while pos < t0 + LANES:
                        if pos in parts:
                            segs.append(parts[pos])
                            pos += parts[pos].shape[0]
                        else:
                            nxt = min([k0 for k0 in parts if k0 > pos] + [t0 + LANES])
                            segs.append(jnp.zeros((nxt - pos, qlanes), BF16))
                            pos = nxt
                    pt = segs[0] if len(segs) == 1 else jnp.concatenate(segs, axis=0)
                    ot = ot + _dot(vt_scr[kv * HEAD_DIM:(kv + 1) * HEAD_DIM, t0:t0 + LANES], pt)
                o_pair.append(ot / den)
            o2 = jnp.concatenate(o_pair, axis=0).T
            for gq in range(q_per_kv):
                tile = pair * q_per_kv + gq
                att_scr[r0:r0 + chunk, tile * LANES:(tile + 1) * LANES] = o2[gq * chunk:(gq + 1) * chunk, :]

    o_ref[...] = x + _dot(att_scr[...].astype(BF16), wo_ref[...])


def _attn_call(x, g, w_q, q_norm, w_o, sinks, k_own, v_own, k_left, v_left, k_meta, v_meta, b_band, b_meta,
               chunk, shared, nseq, tiles_per_seq, chunks_per_tile):
    n, d = x.shape
    kvw = k_own.shape[1]
    n_heads = d // HEAD_DIM
    q_per_kv = n_heads // N_KV_HEADS
    rows = chunks_per_tile * chunk
    band = WINDOW + chunk
    qlanes = q_per_kv * chunk
    assert rows % LANES == 0 and LANES % chunk == 0
    key_rows = WINDOW + rows if shared else chunks_per_tile * 2 * LANES
    e, et = _segment_mats(n_heads)
    gain = jnp.tile(q_norm * (HEAD_DIM ** -0.5), N_KV_HEADS).reshape(1, kvw)
    sink_row = jnp.broadcast_to(sinks.reshape(N_KV_HEADS, 1, q_per_kv, 1),
                                (N_KV_HEADS, 1, q_per_kv, chunk)).reshape(N_KV_HEADS, 1, qlanes)
    w_o = jnp.concatenate([w_o[((2 * pair + kv2) * q_per_kv + gq) * HEAD_DIM:((2 * pair + kv2) * q_per_kv + gq + 1) * HEAD_DIM]
                           for pair in range(N_KV_HEADS // 2) for gq in range(q_per_kv) for kv2 in range(2)], axis=0)
    row_idx = lambda b, i: (b * tiles_per_seq + i, 0)
    if shared:
        halo_blocks = rows // WINDOW
        left_spec = pl.BlockSpec(
            (WINDOW, kvw), lambda b, i: (jnp.maximum(b * tiles_per_seq * halo_blocks + i * halo_blocks - 1, 0), 0))
        last_tile = b_meta.shape[0] // chunks_per_tile - 1
        bmeta_spec = pl.BlockSpec((chunks_per_tile,) + b_meta.shape[1:],
                                  lambda b, i: (jnp.minimum(i, last_tile), 0, 0, 0))
    else:
        left_spec = pl.BlockSpec((chunks_per_tile, WINDOW, kvw), lambda b, i: (b * tiles_per_seq + i, 0, 0))
        bmeta_spec = _resident(b_meta.shape)
    return pl.pallas_call(
        functools.partial(_attn_body, chunk, shared),
        grid=(nseq, tiles_per_seq),
        in_specs=[
            pl.BlockSpec((rows, d), row_idx), _resident((1, d)), _resident(w_q.shape), _resident((1, kvw)),
            _resident(w_o.shape), _resident(sink_row.shape),
            pl.BlockSpec((rows, kvw), row_idx), pl.BlockSpec((rows, kvw), row_idx), left_spec, left_spec,
            _resident(k_meta.shape), _resident((kvw, N_META)), _resident(b_band.shape), bmeta_spec,
            _resident(e.shape), _resident(et.shape),
        ],
        out_specs=pl.BlockSpec((rows, d), row_idx),
        out_shape=jax.ShapeDtypeStruct((n, d), F32),
        scratch_shapes=[pltpu.VMEM((rows, d), F32), pltpu.VMEM((N_KV_HEADS, key_rows, LANES), BF16),
                        pltpu.VMEM((kvw, key_rows), BF16), pltpu.VMEM((N_KV_HEADS, band, qlanes), F32),
                        pltpu.VMEM((rows, d), F32)],
        compiler_params=_cparams("parallel", "arbitrary"),
        name="attn_prompt" if shared else "attn_sample",
    )(x, g.reshape(1, d), w_q, gain, w_o, sink_row, k_own, v_own, k_left, v_left, k_meta, v_meta.T, b_band, b_meta,
      e, et)


def _rel_bucket(rel):
    half = N_BUCKETS // 2
    max_exact = half // 2
    n = jnp.abs(rel)
    nf = jnp.maximum(n, 1).astype(F32)
    large = max_exact + (jnp.log(nf / max_exact) / math.log(MAX_DISTANCE / max_exact)
                         * (half - max_exact)).astype(jnp.int32)
    large = jnp.minimum(large, half - 1)
    return jnp.where(rel > 0, half, 0) + jnp.where(n < max_exact, n, large)


def _bias_lookup(rel_bias, bucket):
    onehot = (bucket[..., None] == jnp.arange(N_BUCKETS)).astype(F32)
    return jnp.dot(onehot, rel_bias, precision=lax.Precision.HIGHEST)


def _band_bias(rel_bias, chunk):
    n_heads = rel_bias.shape[1]
    q_per_kv = n_heads // N_KV_HEADS
    band = WINDOW + chunk
    rel = jnp.arange(band)[None, :] - WINDOW - jnp.arange(chunk)[:, None]
    b = _bias_lookup(rel_bias, _rel_bucket(rel)).reshape(chunk, band, N_KV_HEADS, q_per_kv)
    return jnp.transpose(b, (2, 1, 3, 0)).reshape(N_KV_HEADS, band, q_per_kv * chunk)


def _meta_bias(t0, count, rel_bias, chunk):
    n_heads = rel_bias.shape[1]
    q_per_kv = n_heads // N_KV_HEADS
    dist = jnp.arange(t0 + 1, t0 + count + N_META + 1)
    tab = _bias_lookup(rel_bias, _rel_bucket(-dist)).T
    b = jnp.stack([tab[:, N_META - 1 - m:N_META - 1 - m + count] for m in range(N_META)])
    b = b.reshape(N_META, N_KV_HEADS, q_per_kv, count // chunk, chunk)
    return jnp.transpose(b, (3, 1, 0, 2, 4)).reshape(count // chunk, N_KV_HEADS, N_META, q_per_kv * chunk)


def _meta_bias_tiles(seq, rel_bias, chunk, tile_rows):
    tiles = min(seq // tile_rows, 2)
    assert tiles == seq // tile_rows or (tiles - 1) * tile_rows + 1 >= MAX_DISTANCE
    return _meta_bias(0, tiles * tile_rows, rel_bias, chunk)


def kernel(x_prompt, x_sample, state_ssm_re, state_ssm_im, cache_k, cache_v, meta_tokens, norm_mix, norm_ffn,
           ssm_a_re, ssm_a_im, ssm_log_dt, ssm_b_re, ssm_b_im, ssm_c_re, ssm_c_im, ssm_d, w_glu, w_ffn_in,
           w_ffn_out, norm_kv, w_kv, k_norm, w_q, q_norm, attn_sinks, w_o, rel_bias):
    bsz, seq, d = x_prompt.shape
    dbsz, dseq, _ = x_sample.shape
    n_a = ssm_a_re.shape[0]
    n_b = w_q.shape[0]
    n_groups, n_state = ssm_a_re.shape[1:]
    kvw = N_KV_HEADS * HEAD_DIM

    cast = lambda w: [w[i].astype(BF16) for i in range(w.shape[0])]
    w_glu_b, w_in_b, w_out_b, w_q_b, w_o_b = cast(w_glu), cast(w_ffn_in), cast(w_ffn_out), cast(w_q), cast(w_o)
    w_kv_b = w_kv.astype(BF16)
    tabs = [_s5_tables(ssm_a_re[l], ssm_a_im[l], ssm_log_dt[l], ssm_b_re[l], ssm_b_im[l], ssm_c_re[l],
                       ssm_c_im[l], ssm_d[l]) for l in range(n_a)]

    def s5_layers(x, view, h0_of_layer, spb, cps, tokens):
        states = []
        u = _rms_call(x, norm_mix[0])
        for l in range(n_a):
            y, hf = _s5_call(u.reshape(view), tabs[l], h0_of_layer(l), spb=spb, cps=cps, tokens=tokens)
            states.append(_cols_to_state(hf))
            args = (x, y.reshape(x.shape), w_glu_b[l], norm_ffn[l], w_in_b[l], w_out_b[l])
            if l + 1 < n_a:
                x, u = _glu_ffn_call(*args, norm_mix[l + 1])
            else:
                x, k, v = _glu_ffn_call(*args, norm_kv, kv=(w_kv_b, k_norm))
        return x, k, v, states

    n_s = dbsz * dseq
    n_small = n_s + dseq
    xs = jnp.concatenate([x_sample.reshape(n_s, d), jnp.zeros((dseq - N_META, d), F32), meta_tokens], axis=0)
    zero_state = jnp.zeros((1, n_groups, n_state), F32)
    h0_small = lambda l: _state_to_cols(jnp.concatenate([state_ssm_re[l], zero_state], axis=0),
                                        jnp.concatenate([state_ssm_im[l], zero_state], axis=0))
    xs, k_small, v_small, st_small = s5_layers(xs, (1, n_small, d), h0_small, dbsz + 1, dseq // S5_T, n_small)
    s_re = [hr[:dbsz] for hr, _ in st_small]
    s_im = [hi[:dbsz] for _, hi in st_small]
    k_s, v_s = k_small[:n_s], v_small[:n_s]
    k_meta, v_meta = k_small[n_small - N_META:], v_small[n_small - N_META:]

    bband_s = _band_bias(rel_bias, dseq)
    bmeta_s = _meta_bias(PAST_LEN, dseq, rel_bias, dseq)
    xq = xs[:n_s]
    ck = cache_k.reshape(dbsz, WINDOW, kvw)
    cv = cache_v.reshape(dbsz, WINDOW, kvw)
    for j in range(n_b):
        l = n_a + j
        xq = _attn_call(xq, norm_mix[l], w_q_b[j], q_norm[j], w_o_b[j], attn_sinks[j], k_s, v_s, ck, cv, k_meta,
                        v_meta, bband_s, bmeta_s, chunk=dseq, shared=False, nseq=1,
                        tiles_per_seq=dbsz // ATTN_SEQS_SAMPLE, chunks_per_tile=ATTN_SEQS_SAMPLE)
        xq = _ffn_call(xq, norm_ffn[l], w_in_b[l], w_out_b[l])
    y_sample = xq.reshape(dbsz, dseq, d)

    s5_tokens = min(S5_TOKENS, seq)
    h0_prompt = lambda l: _state_to_cols(jnp.broadcast_to(st_small[l][0][dbsz:], (bsz, n_groups, n_state)),
                                         jnp.broadcast_to(st_small[l][1][dbsz:], (bsz, n_groups, n_state)))
    xp, k_p, v_p, st_prompt = s5_layers(x_prompt.reshape(bsz * seq, d), (bsz, seq, d), h0_prompt, 1,
                                        s5_tokens // S5_T, s5_tokens)
    p_re = [hr for hr, _ in st_prompt]
    p_im = [hi for _, hi in st_prompt]

    bband_p = _band_bias(rel_bias, CHUNK)
    tile_rows = ATTN_CHUNKS_PROMPT * CHUNK
    bmeta_p = _meta_bias_tiles(seq, rel_bias, CHUNK, tile_rows)
    for j in range(n_b):
        l = n_a + j
        xp = _attn_call(xp, norm_mix[l], w_q_b[j], q_norm[j], w_o_b[j], attn_sinks[j], k_p, v_p, k_p, v_p, k_meta,
                        v_meta, bband_p, bmeta_p, chunk=CHUNK, shared=True, nseq=bsz,
                        tiles_per_seq=seq // tile_rows, chunks_per_tile=ATTN_CHUNKS_PROMPT)
        xp = _ffn_call(xp, norm_ffn[l], w_in_b[l], w_out_b[l])
    y_prompt = xp.reshape(bsz, seq, d)

    k_p4 = k_p.reshape(bsz, seq, N_KV_HEADS, HEAD_DIM)
    v_p4 = v_p.reshape(bsz, seq, N_KV_HEADS, HEAD_DIM)
    return (y_prompt, y_sample, jnp.stack(p_re), jnp.stack(p_im), k_p4[:, -WINDOW:], v_p4[:, -WINDOW:],
            jnp.stack(s_re), jnp.stack(s_im), k_s.reshape(dbsz, dseq, N_KV_HEADS, HEAD_DIM),
            v_s.reshape(dbsz, dseq, N_KV_HEADS, HEAD_DIM))
```

```python
import functools
import math

import jax
import jax.numpy as jnp
from jax import lax
from jax.experimental import pallas as pl
from jax.experimental.pallas import tpu as pltpu

F32 = jnp.float32
BF16 = jnp.bfloat16

CHUNK = 64
N_META = 16
SSM_GROUP = 16
SSM_STATE = 64
HEAD_DIM = 64
N_KV_HEADS = 4
WINDOW = 128
N_BUCKETS = 32
MAX_DISTANCE = 128
PAST_LEN = 1024
EPS = 1e-6
NEG = -1e30

LANES = 128
SUBLANES = 8
VMEM_LIMIT_BYTES = 56 * 1024 * 1024

S5_T = 16
S5_PAIR = 2 * SSM_GROUP
S5_PAIRS = LANES // S5_PAIR
S5_TOKENS = 4096
S5_SCAN_UNROLL = 4
ROW_TILE = 512
FFN_CHUNK = 256
ATTN_CHUNKS_PROMPT = 8
ATTN_SEQS_SAMPLE = 8


def _row_tile(n, target):
    if n <= target:
        return n
    best = None
    for t in range(SUBLANES, target + 1, SUBLANES):
        if n % t == 0:
            best = t
    assert best is not None, n
    return best


def _cparams(*sem):
    return pltpu.CompilerParams(dimension_semantics=sem, vmem_limit_bytes=VMEM_LIMIT_BYTES)


def _resident(shape):
    nd = len(shape)
    return pl.BlockSpec(shape, lambda *_: (0,) * nd, pipeline_mode=pl.Buffered(1))


def _rms(x, g):
    ms = jnp.mean(x * x, axis=-1, keepdims=True)
    return (x * lax.rsqrt(ms + EPS)) * g


def _dot(a, b):
    return jnp.dot(a, b, preferred_element_type=F32)


def _dot_nt(a, b):
    return lax.dot_general(a, b, (((1,), (1,)), ((), ())), preferred_element_type=F32)


def _head_rms(x, e, et, gain):
    ss = _dot((x * x).astype(BF16), e)
    rinv = lax.rsqrt(ss * (1.0 / HEAD_DIM) + EPS)
    hi = rinv.astype(BF16)
    lo = (rinv - hi.astype(F32)).astype(BF16)
    rb = _dot(hi, et) + _dot(lo, et)
    return (x * rb) * gain


def _segment_mats(n_heads):
    idx = jnp.arange(n_heads * HEAD_DIM) // HEAD_DIM
    e = (idx[:, None] == jnp.arange(LANES)[None, :]).astype(BF16)
    return e, e.T


def _rms_body(x_ref, g_ref, o_ref):
    o_ref[...] = _rms(x_ref[...], g_ref[...])


def _rms_call(x, g):
    n, d = x.shape
    tm = _row_tile(n, 1024)
    return pl.pallas_call(
        _rms_body,
        grid=(n // tm,),
        in_specs=[pl.BlockSpec((tm, d), lambda i: (i, 0)), _resident((1, d))],
        out_specs=pl.BlockSpec((tm, d), lambda i: (i, 0)),
        out_shape=jax.ShapeDtypeStruct((n, d), F32),
        compiler_params=_cparams("parallel"),
        name="rms",
    )(x, g.reshape(1, d))


def _xpose4(a, lo64, lo32):
    a0, a1, a2, a3 = a
    r02 = pltpu.roll(jnp.where(lo64, a2, a0), 64, 1)
    r13 = pltpu.roll(jnp.where(lo64, a3, a1), 64, 1)
    b0 = jnp.where(lo64, a0, r02)
    b2 = jnp.where(lo64, r02, a2)
    b1 = jnp.where(lo64, a1, r13)
    b3 = jnp.where(lo64, r13, a3)
    return (jnp.where(lo32, b0, pltpu.roll(b1, 32, 1)), jnp.where(lo32, pltpu.roll(b0, 96, 1), b1),
            jnp.where(lo32, b2, pltpu.roll(b3, 32, 1)), jnp.where(lo32, pltpu.roll(b2, 96, 1), b3))


def _s5_body(spb, cps, u_ref, xb_ref, xc_ref, xd_ref, pw_ref, h0_ref, y_ref, hf_ref,
             s_scr, hin_scr, h_scr, m_scr, wo_scr, wit_scr):
    nb = u_ref.shape[0]
    t = S5_T
    rows_b = spb * cps
    nc = nb * rows_b
    ns = nb * spb
    ncol = 2 * S5_PAIRS
    width = t * S5_PAIR

    @pl.when(pl.program_id(1) == 0)
    def _():
        h_scr[...] = h0_ref[...]
        strip_lane = lax.broadcasted_iota(jnp.int32, (S5_PAIR, width), 1)
        for p in range(S5_PAIRS):
            br, bi = xb_ref[p, 0], xb_ref[p, 1]
            cr, ci = xc_ref[p, 0], xc_ref[p, 1]
            cw = []
            for s in range(t + 1):
                ar, ai = pw_ref[p, 0, s:s + 1, :], pw_ref[p, 1, s:s + 1, :]
                cw.append(jnp.concatenate([cr * ar - ci * ai, -(cr * ai + ci * ar)], axis=1))
            for s in range(t):
                rs = slice(S5_PAIR * s, S5_PAIR * (s + 1))
                ar, ai = pw_ref[p, 0, t - 1 - s:t - s, :], pw_ref[p, 1, t - 1 - s:t - s, :]
                wo_scr[p, rs, 0:LANES] = (br * ar - bi * ai).astype(BF16)
                wo_scr[p, rs, LANES:2 * LANES] = (br * ai + bi * ar).astype(BF16)
                wit_scr[p, rs, :] = cw[s + 1].astype(BF16)
            xbb = jnp.concatenate([br, bi], axis=1)
            cwt = jnp.concatenate(cw[:t], axis=0)
            x_hi, c_hi = xbb.astype(BF16), cwt.astype(BF16)
            x_lo, c_lo = (xbb - x_hi.astype(F32)).astype(BF16), (cwt - c_hi.astype(F32)).astype(BF16)
            strip = _dot_nt(x_hi, c_hi) + (_dot_nt(x_hi, c_lo) + _dot_nt(x_lo, c_hi))
            strip = strip + jnp.concatenate([xd_ref[p], jnp.zeros((S5_PAIR, width - LANES), F32)], axis=1)
            for s in range(t):
                blk = strip if s == 0 else jnp.where(strip_lane >= S5_PAIR * s, pltpu.roll(strip, S5_PAIR * s, 1), 0.0)
                m_scr[p, S5_PAIR * s:S5_PAIR * (s + 1), :] = blk.astype(BF16)

    lane = lax.broadcasted_iota(jnp.int32, (nc, LANES), 1)
    lo64 = lane < 64
    lo32 = (lane % 64) < 32

    v = []
    for tok in range(t):
        parts = [u_ref[b, pl.ds(tok, rows_b, stride=t), :] for b in range(nb)]
        v.append(parts[0] if nb == 1 else jnp.concatenate(parts, axis=0))
    quads = [_xpose4(v[4 * tq:4 * tq + 4], lo64, lo32) for tq in range(t // 4)]
    u_p = [jnp.concatenate([quads[tq][p] for tq in range(t // 4)], axis=1).astype(BF16) for p in range(S5_PAIRS)]

    if S5_PAIRS * ns == SUBLANES:
        for p in range(S5_PAIRS):
            s = _dot(u_p[p], wo_scr[p])
            for q in range(ns):
                rows_q = pl.ds(p * ns + q, cps, stride=SUBLANES)
                s_scr[0, rows_q, :] = s[q * cps:(q + 1) * cps, :LANES]
                s_scr[1, rows_q, :] = s[q * cps:(q + 1) * cps, LANES:]
        ar = jnp.concatenate([jnp.broadcast_to(pw_ref[p, 0, t:t + 1, :], (ns, LANES)) for p in range(S5_PAIRS)], axis=0)
        ai = jnp.concatenate([jnp.broadcast_to(pw_ref[p, 1, t:t + 1, :], (ns, LANES)) for p in range(S5_PAIRS)], axis=0)

        def step(k, carry):
            cr, ci = carry
            rows_k = pl.ds(pl.multiple_of(k * SUBLANES, SUBLANES), SUBLANES)
            hin_scr[0, rows_k, :] = cr
            hin_scr[1, rows_k, :] = ci
            return ar * cr - ai * ci + s_scr[0, rows_k, :], ar * ci + ai * cr + s_scr[1, rows_k, :]

        cr, ci = lax.fori_loop(0, cps, step, (jnp.concatenate([h_scr[2 * p] for p in range(S5_PAIRS)], axis=0),
                                              jnp.concatenate([h_scr[2 * p + 1] for p in range(S5_PAIRS)], axis=0)),
                               unroll=min(cps, S5_SCAN_UNROLL))
        carry = []
        for p in range(S5_PAIRS):
            carry += [cr[p * ns:(p + 1) * ns], ci[p * ns:(p + 1) * ns]]

        def chunk_states(p, plane):
            return jnp.concatenate([hin_scr[plane, pl.ds(p * ns + q, cps, stride=SUBLANES), :] for q in range(ns)],
                                   axis=0)
    else:
        for p in range(S5_PAIRS):
            s = _dot(u_p[p], wo_scr[p])
            s_scr[2 * p, 0:nc, :] = s[:, :LANES]
            s_scr[2 * p + 1, 0:nc, :] = s[:, LANES:]

        def step(k, carry):
            new = []
            for p in range(S5_PAIRS):
                cr, ci = carry[2 * p], carry[2 * p + 1]
                hin_scr[2 * p, pl.ds(k, ns, stride=cps), :] = cr
                hin_scr[2 * p + 1, pl.ds(k, ns, stride=cps), :] = ci
                ar = pw_ref[p, 0, t:t + 1, :]
                ai = pw_ref[p, 1, t:t + 1, :]
                tr = s_scr[2 * p, pl.ds(k, ns, stride=cps), :]
                ti = s_scr[2 * p + 1, pl.ds(k, ns, stride=cps), :]
                new += [ar * cr - ai * ci + tr, ar * ci + ai * cr + ti]
            return tuple(new)

        carry = lax.fori_loop(0, cps, step, tuple(h_scr[c] for c in range(ncol)), unroll=min(cps, S5_SCAN_UNROLL))

        def chunk_states(p, plane):
            return hin_scr[2 * p + plane, 0:nc, :]

    for c in range(ncol):
        h_scr[c] = carry[c]
        hf_ref[c] = carry[c]

    y_p = []
    for p in range(S5_PAIRS):
        h = jnp.concatenate([chunk_states(p, 0), chunk_states(p, 1)], axis=1).astype(BF16)
        y_p.append(_dot(u_p[p], m_scr[p]) + _dot_nt(h, wit_scr[p]))
    for tq in range(t // 4):
        w = _xpose4([y_p[p][:, LANES * tq:LANES * (tq + 1)] for p in range(S5_PAIRS)], lo64, lo32)
        for i in range(4):
            for b in range(nb):
                y_ref[b, pl.ds(4 * tq + i, rows_b, stride=t), :] = w[i][b * rows_b:(b + 1) * rows_b]


def _s5_call(u, tabs, h0, spb, cps, tokens):
    xb, xc, xd, pw = tabs
    nb, length, d = u.shape
    assert tokens == spb * cps * S5_T and length % tokens == 0
    ns = nb * spb
    ncol = 2 * S5_PAIRS
    nc_pad = -(-(nb * spb * cps) // SUBLANES) * SUBLANES
    scan_shape = (2, cps * SUBLANES, LANES) if S5_PAIRS * ns == SUBLANES else (ncol, nc_pad, LANES)
    width = S5_T * S5_PAIR
    blk = pl.BlockSpec((nb, tokens, LANES), lambda b, r: (0, r, b))
    state = pl.BlockSpec((ncol, ns, LANES), lambda b, r: (b, 0, 0))
    return pl.pallas_call(
        functools.partial(_s5_body, spb, cps),
        grid=(d // LANES, length // tokens),
        in_specs=[
            blk,
            pl.BlockSpec((S5_PAIRS, 2, S5_PAIR, LANES), lambda b, r: (b, 0, 0, 0)),
            pl.BlockSpec((S5_PAIRS, 2, S5_PAIR, LANES), lambda b, r: (b, 0, 0, 0)),
            pl.BlockSpec((S5_PAIRS, S5_PAIR, LANES), lambda b, r: (b, 0, 0)),
            pl.BlockSpec((S5_PAIRS, 2, S5_T + 1, LANES), lambda b, r: (b, 0, 0, 0)),
            state,
        ],
        out_specs=[blk, state],
        out_shape=[jax.ShapeDtypeStruct(u.shape, F32), jax.ShapeDtypeStruct(h0.shape, F32)],
        scratch_shapes=[pltpu.VMEM(scan_shape, F32), pltpu.VMEM(scan_shape, F32),
                        pltpu.VMEM((ncol, ns, LANES), F32), pltpu.VMEM((S5_PAIRS, width, width), BF16),
                        pltpu.VMEM((S5_PAIRS, width, 2 * LANES), BF16), pltpu.VMEM((S5_PAIRS, width, 2 * LANES), BF16)],
        compiler_params=_cparams("parallel", "arbitrary"),
        name="s5",
    )(u, xb, xc, xd, pw, h0)


def _s5_tables(a_re, a_im, log_dt, b_re, b_im, c_re, c_im, d_skip):
    t = S5_T
    g, p = a_re.shape
    c = b_re.shape[-1]
    dt = jnp.exp(log_dt)[:, None]
    mag = jnp.exp(a_re * dt)
    ang = a_im * dt
    ar = mag * jnp.cos(ang)
    ai = mag * jnp.sin(ang)
    num_re = ar - 1.0
    num_im = ai
    inv = 1.0 / (a_re * a_re + a_im * a_im)
    f_re = (num_re * a_re + num_im * a_im) * inv
    f_im = (num_im * a_re - num_re * a_im) * inv
    bb_re = f_re[..., None] * b_re - f_im[..., None] * b_im
    bb_im = f_re[..., None] * b_im + f_im[..., None] * b_re
    pr = jnp.ones((1, g, p), F32)
    pi = jnp.zeros((1, g, p), F32)
    sr, si = ar, ai
    while pr.shape[0] < t + 1:
        pr, pi = (jnp.concatenate([pr, pr * sr - pi * si]), jnp.concatenate([pi, pr * si + pi * sr]))
        sr, si = sr * sr - si * si, 2.0 * sr * si
    pr = pr[:t + 1]
    pi = pi[:t + 1]
    eye2 = jnp.eye(2, dtype=F32)

    def pad_pair(x):
        x = x.reshape(g // 2, 2, c, p)
        return (x[:, :, :, None, :] * eye2[None, :, None, :, None]).reshape(g // 2, 2 * c, 2 * p)

    xb = jnp.stack([pad_pair(jnp.transpose(bb_re, (0, 2, 1))), pad_pair(jnp.transpose(bb_im, (0, 2, 1)))], axis=1)
    xc = jnp.stack([pad_pair(c_re), pad_pair(c_im)], axis=1)
    xd = d_skip.reshape(g // 2, 2 * c, 1) * jnp.eye(2 * c, LANES, dtype=F32)[None]
    pw = jnp.stack([jnp.transpose(pr.reshape(t + 1, g // 2, 2 * p), (1, 0, 2)),
                    jnp.transpose(pi.reshape(t + 1, g // 2, 2 * p), (1, 0, 2))], axis=1)
    return xb, xc, xd, pw


def _state_to_cols(h_re, h_im):
    ns, g, p = h_re.shape
    f = lambda h: jnp.transpose(h.reshape(ns, g // 2, 2 * p), (1, 0, 2))
    return jnp.stack([f(h_re), f(h_im)], axis=1).reshape(g, ns, 2 * p)


def _cols_to_state(h):
    g, ns, w = h.shape
    h = jnp.transpose(h.reshape(g // 2, 2, ns, 2, w // 2), (1, 2, 0, 3, 4)).reshape(2, ns, g, w // 2)
    return h[0], h[1]


def _ffn_rows(x, g, win_ref, wout_ref):
    dff = wout_ref.shape[0]
    xn = _rms(x, g).astype(BF16)
    acc = x
    for j in range(dff // FFN_CHUNK):
        lo = j * FFN_CHUNK
        gate = _dot(xn, win_ref[:, lo:lo + FFN_CHUNK])
        up = _dot(xn, win_ref[:, dff + lo:dff + lo + FFN_CHUNK])
        a = (jax.nn.silu(gate) * up).astype(BF16)
        acc = acc + _dot(a, wout_ref[lo:lo + FFN_CHUNK, :])
    return acc


def _kv_rows(x, g_ref, w_ref, kn_ref, e_ref, et_ref, k_ref, v_ref):
    kvw = k_ref.shape[1]
    r = _dot(_rms(x, g_ref[...]).astype(BF16), w_ref[...])
    k_ref[...] = _head_rms(r[:, :kvw], e_ref[...], et_ref[...], kn_ref[...])
    v_ref[...] = r[:, kvw:]


def _ffn_body(x_ref, g_ref, win_ref, wout_ref, o_ref):
    o_ref[...] = _ffn_rows(x_ref[...], g_ref[...], win_ref, wout_ref)


def _glu_ffn_body(tail, x_ref, y_ref, wglu_ref, g_ref, win_ref, wout_ref, *rest):
    d = x_ref.shape[1]
    r = _dot(jax.nn.gelu(y_ref[...]).astype(BF16), wglu_ref[...])
    x = x_ref[...] + r[:, :d] * jax.nn.sigmoid(r[:, d:])
    out = _ffn_rows(x, g_ref[...], win_ref, wout_ref)
    if tail == "rms":
        gn_ref, o_ref, u_ref = rest
        u_ref[...] = _rms(out, gn_ref[...])
    else:
        gn_ref, wkv_ref, kn_ref, e_ref, et_ref, o_ref, k_ref, v_ref = rest
        _kv_rows(out, gn_ref, wkv_ref, kn_ref, e_ref, et_ref, k_ref, v_ref)
    o_ref[...] = out


def _ffn_call(x, g, w_in, w_out):
    n, d = x.shape
    tm = _row_tile(n, ROW_TILE)
    row = pl.BlockSpec((tm, d), lambda i: (i, 0))
    return pl.pallas_call(
        _ffn_body,
        grid=(n // tm,),
        in_specs=[row, _resident((1, d)), _resident(w_in.shape), _resident(w_out.shape)],
        out_specs=row,
        out_shape=jax.ShapeDtypeStruct((n, d), F32),
        compiler_params=_cparams("parallel"),
        name="ffn",
    )(x, g.reshape(1, d), w_in, w_out)


def _glu_ffn_call(x, y, w_glu, g, w_in, w_out, next_norm, kv=None):
    n, d = x.shape
    tm = _row_tile(n, ROW_TILE)
    row = pl.BlockSpec((tm, d), lambda i: (i, 0))
    in_specs = [row, row, _resident(w_glu.shape), _resident((1, d)), _resident(w_in.shape), _resident(w_out.shape),
                _resident((1, d))]
    args = [x, y, w_glu, g.reshape(1, d), w_in, w_out, next_norm.reshape(1, d)]
    if kv is None:
        tail, out_specs, out_shape = "rms", [row, row], [jax.ShapeDtypeStruct((n, d), F32)] * 2
    else:
        w_kv, k_norm = kv
        kvw = w_kv.shape[1] // 2
        e, et = _segment_mats(kvw // HEAD_DIM)
        kvrow = pl.BlockSpec((tm, kvw), lambda i: (i, 0))
        in_specs += [_resident(w_kv.shape), _resident((1, kvw)), _resident(e.shape), _resident(et.shape)]
        args += [w_kv, jnp.tile(k_norm, kvw // HEAD_DIM).reshape(1, kvw), e, et]
        tail, out_specs = "kv", [row, kvrow, kvrow]
        out_shape = [jax.ShapeDtypeStruct((n, d), F32)] + [jax.ShapeDtypeStruct((n, kvw), F32)] * 2
    return pl.pallas_call(
        functools.partial(_glu_ffn_body, tail),
        grid=(n // tm,),
        in_specs=in_specs,
        out_specs=out_specs,
        out_shape=out_shape,
        compiler_params=_cparams("parallel"),
        name="glu_ffn_" + tail,
    )(*args)


def _attn_body(chunk, shared, x_ref, g_ref, wq_ref, gain_ref, wo_ref, sink_ref, kown_ref, vown_ref, kleft_ref,
               vleft_ref, kmeta_ref, vmeta_ref, bband_ref, bmeta_ref, e_ref, o_ref,
               q_scr, rt_scr, kd_scr, vt_scr, lg_scr, att_scr):
    rows, d = x_ref.shape
    nchunks = rows // chunk
    n_heads = d // HEAD_DIM
    q_per_kv = n_heads // N_KV_HEADS
    band = WINDOW + chunk
    qlanes = q_per_kv * chunk
    kvw = N_KV_HEADS * HEAD_DIM

    x = x_ref[...]
    q = _dot(_rms(x, g_ref[...]).astype(BF16), wq_ref[...])
    q_scr[...] = q
    ss = _dot((q * q).astype(BF16), e_ref[...])
    rt_scr[...] = lax.rsqrt(ss * (1.0 / HEAD_DIM) + EPS).T
    lane_row = lax.broadcasted_iota(jnp.int32, (1, LANES), 1)

    gain = gain_ref[...]
    low = lax.broadcasted_iota(jnp.int32, (1, LANES), 1) < HEAD_DIM

    def dup(kf):
        out = []
        for kv in range(N_KV_HEADS):
            t = kf[:, (kv // 2) * LANES:(kv // 2 + 1) * LANES]
            r = pltpu.roll(t, HEAD_DIM, 1)
            out.append((jnp.where(low, t, r) if kv % 2 == 0 else jnp.where(low, r, t)).astype(BF16))
        return out

    def put_keys(row0, kf, vf):
        for kv, t in enumerate(dup(kf * gain)):
            kd_scr[kv, row0:row0 + LANES, :] = t
        vt_scr[:, row0:row0 + LANES] = vf.T.astype(BF16)

    km = dup(kmeta_ref[...] * gain)
    vmt = vmeta_ref[...].astype(BF16)
    if shared:
        put_keys(0, kleft_ref[...], vleft_ref[...])
        for t in range(rows // LANES):
            put_keys(WINDOW + t * LANES, kown_ref[t * LANES:(t + 1) * LANES, :], vown_ref[t * LANES:(t + 1) * LANES, :])
        first_chunk = pl.program_id(1) * nchunks
    else:
        zpad = jnp.zeros((LANES - chunk, kvw), F32)
        for j in range(nchunks):
            put_keys(j * 2 * LANES, kleft_ref[j], vleft_ref[j])
            put_keys(j * 2 * LANES + LANES, jnp.concatenate([kown_ref[j * chunk:(j + 1) * chunk, :], zpad], axis=0),
                     jnp.concatenate([vown_ref[j * chunk:(j + 1) * chunk, :], zpad], axis=0))

    for j in range(nchunks):
        r0 = j * chunk
        if shared:
            base = r0
            blocks = [(base + i * chunk, chunk, i * chunk, WINDOW // chunk - i) for i in range(band // chunk)]
        else:
            base = j * 2 * LANES
            blocks = [(base, WINDOW, 0, None), (base + WINDOW, chunk, WINDOW, None)]
        tile0 = (base // LANES) * LANES
        bm = bmeta_ref[j] if shared else bmeta_ref[0]
        for pair in range(N_KV_HEADS // 2):
            o_pair = []
            for kv in (2 * pair, 2 * pair + 1):
                qx = []
                for gq in range(q_per_kv):
                    h = kv * q_per_kv + gq
                    t = q_scr[r0:r0 + chunk, (h // 2) * LANES:(h // 2 + 1) * LANES]
                    qx.append(jnp.where(low if h % 2 == 0 else jnp.logical_not(low), t, 0.0))
                qx = jnp.concatenate(qx, axis=0).astype(BF16)
                s = sink_ref[kv]
                src = (r0 // LANES) * LANES
                tiles = [jnp.zeros((1, LANES), F32) for _ in range(qlanes // LANES)]
                for gq in range(q_per_kv):
                    h = kv * q_per_kv + gq
                    dst = gq * chunk
                    row = pltpu.roll(rt_scr[h:h + 1, src:src + LANES], (dst - r0) % LANES, 1)
                    pick = (lane_row >= dst % LANES) & (lane_row < dst % LANES + chunk)
                    tiles[dst // LANES] = jnp.where(pick, row, tiles[dst // LANES])
                rrow = tiles[0] if len(tiles) == 1 else jnp.concatenate(tiles, axis=1)
                raw = _dot_nt(jnp.concatenate([kd_scr[kv, base:base + band, :], km[kv]], axis=0), qx)
                run = jnp.full((SUBLANES, qlanes), NEG, F32)
                for (k0, nk, b0, back) in blocks:
                    lg = raw[b0:b0 + nk] * rrow + bband_ref[kv, b0:b0 + nk, :]
                    if back and j < back:
                        lg = jnp.where(first_chunk + j >= back, lg, NEG)
                    lg_scr[kv, b0:b0 + nk, :] = lg
                    run = jnp.maximum(run, jnp.max(lg.reshape(nk // SUBLANES, SUBLANES, qlanes), axis=0))
                lgm = raw[band:band + N_META] * rrow + bm[kv]
                run = jnp.maximum(run, jnp.max(lgm.reshape(N_META // SUBLANES, SUBLANES, qlanes), axis=0))
                mx = jnp.maximum(jnp.max(run, axis=0, keepdims=True), s)
                pm = jnp.exp(lgm - mx)
                den = jnp.sum(pm, axis=0, keepdims=True) + jnp.exp(s - mx)
                ot = _dot(vmt[kv * HEAD_DIM:(kv + 1) * HEAD_DIM, :], pm.astype(BF16))
                parts = {}
                for (k0, nk, b0, back) in blocks:
                    p = jnp.exp(lg_scr[kv, b0:b0 + nk, :] - mx)
                    den = den + jnp.sum(p, axis=0, keepdims=True)
                    parts[k0] = p.astype(BF16)
                for t0 in (tile0, tile0 + LANES):
                    segs, pos = [], t0
                    while pos < t0 + LANES:
                        if pos in parts:
                            segs.append(parts[pos])
                            pos += parts[pos].shape[0]
                        else:
                            nxt = min([k0 for k0 in parts if k0 > pos] + [t0 + LANES])
                            segs.append(jnp.zeros((nxt - pos, qlanes), BF16))
                            pos = nxt
                    pt = segs[0] if len(segs) == 1 else jnp.concatenate(segs, axis=0)
                    ot = ot + _dot(vt_scr[kv * HEAD_DIM:(kv + 1) * HEAD_DIM, t0:t0 + LANES], pt)
                o_pair.append(ot / den)
            o2 = jnp.concatenate(o_pair, axis=0).T
            for gq in range(q_per_kv):
                tile = pair * q_per_kv + gq
                att_scr[r0:r0 + chunk, tile * LANES:(tile + 1) * LANES] = o2[gq * chunk:(gq + 1) * chunk, :]

    o_ref[...] = x + _dot(att_scr[...].astype(BF16), wo_ref[...])


def _attn_call(x, g, w_q, q_norm, w_o, sinks, k_own, v_own, k_left, v_left, k_meta, v_meta, b_band, b_meta,
               chunk, shared, nseq, tiles_per_seq, chunks_per_tile):
    n, d = x.shape
    kvw = k_own.shape[1]
    n_heads = d // HEAD_DIM
    q_per_kv = n_heads // N_KV_HEADS
    rows = chunks_per_tile * chunk
    band = WINDOW + chunk
    qlanes = q_per_kv * chunk
    assert rows % LANES == 0 and LANES % chunk == 0
    key_rows = WINDOW + rows if shared else chunks_per_tile * 2 * LANES
    e, _ = _segment_mats(n_heads)
    gain = jnp.tile(q_norm * (HEAD_DIM ** -0.5), N_KV_HEADS).reshape(1, kvw)
    sink_row = jnp.broadcast_to(sinks.reshape(N_KV_HEADS, 1, q_per_kv, 1),
                                (N_KV_HEADS, 1, q_per_kv, chunk)).reshape(N_KV_HEADS, 1, qlanes)
    w_o = jnp.concatenate([w_o[((2 * pair + kv2) * q_per_kv + gq) * HEAD_DIM:((2 * pair + kv2) * q_per_kv + gq + 1) * HEAD_DIM]
                           for pair in range(N_KV_HEADS // 2) for gq in range(q_per_kv) for kv2 in range(2)], axis=0)
    row_idx = lambda b, i: (b * tiles_per_seq + i, 0)
    if shared:
        halo_blocks = rows // WINDOW
        left_spec = pl.BlockSpec(
            (WINDOW, kvw), lambda b, i: (jnp.maximum(b * tiles_per_seq * halo_blocks + i * halo_blocks - 1, 0), 0))
        last_tile = b_meta.shape[0] // chunks_per_tile - 1
        bmeta_spec = pl.BlockSpec((chunks_per_tile,) + b_meta.shape[1:],
                                  lambda b, i: (jnp.minimum(i, last_tile), 0, 0, 0))
    else:
        left_spec = pl.BlockSpec((chunks_per_tile, WINDOW, kvw), lambda b, i: (b * tiles_per_seq + i, 0, 0))
        bmeta_spec = _resident(b_meta.shape)
    return pl.pallas_call(
        functools.partial(_attn_body, chunk, shared),
        grid=(nseq, tiles_per_seq),
        in_specs=[
            pl.BlockSpec((rows, d), row_idx), _resident((1, d)), _resident(w_q.shape), _resident((1, kvw)),
            _resident(w_o.shape), _resident(sink_row.shape),
            pl.BlockSpec((rows, kvw), row_idx), pl.BlockSpec((rows, kvw), row_idx), left_spec, left_spec,
            _resident(k_meta.shape), _resident((kvw, N_META)), _resident(b_band.shape), bmeta_spec,
            _resident(e.shape),
        ],
        out_specs=pl.BlockSpec((rows, d), row_idx),
        out_shape=jax.ShapeDtypeStruct((n, d), F32),
        scratch_shapes=[pltpu.VMEM((rows, d), F32), pltpu.VMEM((LANES, rows), F32),
                        pltpu.VMEM((N_KV_HEADS, key_rows, LANES), BF16),
                        pltpu.VMEM((kvw, key_rows), BF16), pltpu.VMEM((N_KV_HEADS, band, qlanes), F32),
                        pltpu.VMEM((rows, d), F32)],
        compiler_params=_cparams("parallel", "arbitrary"),
        name="attn_prompt" if shared else "attn_sample",
    )(x, g.reshape(1, d), w_q, gain, w_o, sink_row, k_own, v_own, k_left, v_left, k_meta, v_meta.T, b_band, b_meta,
      e)


def _rel_bucket(rel):
    half = N_BUCKETS // 2
    max_exact = half // 2
    n = jnp.abs(rel)
    nf = jnp.maximum(n, 1).astype(F32)
    large = max_exact + (jnp.log(nf / max_exact) / math.log(MAX_DISTANCE / max_exact)
                         * (half - max_exact)).astype(jnp.int32)
    large = jnp.minimum(large, half - 1)
    return jnp.where(rel > 0, half, 0) + jnp.where(n < max_exact, n, large)


def _bias_lookup(rel_bias, bucket):
    onehot = (bucket[..., None] == jnp.arange(N_BUCKETS)).astype(F32)
    return jnp.dot(onehot, rel_bias, precision=lax.Precision.HIGHEST)


def _band_bias(rel_bias, chunk):
    n_heads = rel_bias.shape[1]
    q_per_kv = n_heads // N_KV_HEADS
    band = WINDOW + chunk
    rel = jnp.arange(band)[None, :] - WINDOW - jnp.arange(chunk)[:, None]
    b = _bias_lookup(rel_bias, _rel_bucket(rel)).reshape(chunk, band, N_KV_HEADS, q_per_kv)
    return jnp.transpose(b, (2, 1, 3, 0)).reshape(N_KV_HEADS, band, q_per_kv * chunk)


def _meta_bias(t0, count, rel_bias, chunk):
    n_heads = rel_bias.shape[1]
    q_per_kv = n_heads // N_KV_HEADS
    dist = jnp.arange(t0 + 1, t0 + count + N_META + 1)
    tab = _bias_lookup(rel_bias, _rel_bucket(-dist)).T
    b = jnp.stack([tab[:, N_META - 1 - m:N_META - 1 - m + count] for m in range(N_META)])
    b = b.reshape(N_META, N_KV_HEADS, q_per_kv, count // chunk, chunk)
    return jnp.transpose(b, (3, 1, 0, 2, 4)).reshape(count // chunk, N_KV_HEADS, N_META, q_per_kv * chunk)


def _meta_bias_tiles(seq, rel_bias, chunk, tile_rows):
    tiles = min(seq // tile_rows, 2)
    assert tiles == seq // tile_rows or (tiles - 1) * tile_rows + 1 >= MAX_DISTANCE
    return _meta_bias(0, tiles * tile_rows, rel_bias, chunk)


def kernel(x_prompt, x_sample, state_ssm_re, state_ssm_im, cache_k, cache_v, meta_tokens, norm_mix, norm_ffn,
           ssm_a_re, ssm_a_im, ssm_log_dt, ssm_b_re, ssm_b_im, ssm_c_re, ssm_c_im, ssm_d, w_glu, w_ffn_in,
           w_ffn_out, norm_kv, w_kv, k_norm, w_q, q_norm, attn_sinks, w_o, rel_bias):
    bsz, seq, d = x_prompt.shape
    dbsz, dseq, _ = x_sample.shape
    n_a = ssm_a_re.shape[0]
    n_b = w_q.shape[0]
    n_groups, n_state = ssm_a_re.shape[1:]
    kvw = N_KV_HEADS * HEAD_DIM

    cast = lambda w: [w[i].astype(BF16) for i in range(w.shape[0])]
    w_glu_b, w_in_b, w_out_b, w_q_b, w_o_b = cast(w_glu), cast(w_ffn_in), cast(w_ffn_out), cast(w_q), cast(w_o)
    w_kv_b = w_kv.astype(BF16)
    tabs = [_s5_tables(ssm_a_re[l], ssm_a_im[l], ssm_log_dt[l], ssm_b_re[l], ssm_b_im[l], ssm_c_re[l],
                       ssm_c_im[l], ssm_d[l]) for l in range(n_a)]

    def s5_layers(x, view, h0_of_layer, spb, cps, tokens):
        states = []
        u = _rms_call(x, norm_mix[0])
        for l in range(n_a):
            y, hf = _s5_call(u.reshape(view), tabs[l], h0_of_layer(l), spb=spb, cps=cps, tokens=tokens)
            states.append(_cols_to_state(hf))
            args = (x, y.reshape(x.shape), w_glu_b[l], norm_ffn[l], w_in_b[l], w_out_b[l])
            if l + 1 < n_a:
                x, u = _glu_ffn_call(*args, norm_mix[l + 1])
            else:
                x, k, v = _glu_ffn_call(*args, norm_kv, kv=(w_kv_b, k_norm))
        return x, k, v, states

    n_s = dbsz * dseq
    n_small = n_s + dseq
    xs = jnp.concatenate([x_sample.reshape(n_s, d), jnp.zeros((dseq - N_META, d), F32), meta_tokens], axis=0)
    zero_state = jnp.zeros((1, n_groups, n_state), F32)
    h0_small = lambda l: _state_to_cols(jnp.concatenate([state_ssm_re[l], zero_state], axis=0),
                                        jnp.concatenate([state_ssm_im[l], zero_state], axis=0))
    xs, k_small, v_small, st_small = s5_layers(xs, (1, n_small, d), h0_small, dbsz + 1, dseq // S5_T, n_small)
    s_re = [hr[:dbsz] for hr, _ in st_small]
    s_im = [hi[:dbsz] for _, hi in st_small]
    k_s, v_s = k_small[:n_s], v_small[:n_s]
    k_meta, v_meta = k_small[n_small - N_META:], v_small[n_small - N_META:]

    bband_s = _band_bias(rel_bias, dseq)
    bmeta_s = _meta_bias(PAST_LEN, dseq, rel_bias, dseq)
    xq = xs[:n_s]
    ck = cache_k.reshape(dbsz, WINDOW, kvw)
    cv = cache_v.reshape(dbsz, WINDOW, kvw)
    for j in range(n_b):
        l = n_a + j
        xq = _attn_call(xq, norm_mix[l], w_q_b[j], q_norm[j], w_o_b[j], attn_sinks[j], k_s, v_s, ck, cv, k_meta,
                        v_meta, bband_s, bmeta_s, chunk=dseq, shared=False, nseq=1,
                        tiles_per_seq=dbsz // ATTN_SEQS_SAMPLE, chunks_per_tile=ATTN_SEQS_SAMPLE)
        xq = _ffn_call(xq, norm_ffn[l], w_in_b[l], w_out_b[l])
    y_sample = xq.reshape(dbsz, dseq, d)

    s5_tokens = min(S5_TOKENS, seq)
    h0_prompt = lambda l: _state_to_cols(jnp.broadcast_to(st_small[l][0][dbsz:], (bsz, n_groups, n_state)),
                                         jnp.broadcast_to(st_small[l][1][dbsz:], (bsz, n_groups, n_state)))
    xp, k_p, v_p, st_prompt = s5_layers(x_prompt.reshape(bsz * seq, d), (bsz, seq, d), h0_prompt, 1,
                                        s5_tokens // S5_T, s5_tokens)
    p_re = [hr for hr, _ in st_prompt]
    p_im = [hi for _, hi in st_prompt]

    bband_p = _band_bias(rel_bias, CHUNK)
    tile_rows = ATTN_CHUNKS_PROMPT * CHUNK
    bmeta_p = _meta_bias_tiles(seq, rel_bias, CHUNK, tile_rows)
    for j in range(n_b):
        l = n_a + j
        xp = _attn_call(xp, norm_mix[l], w_q_b[j], q_norm[j], w_o_b[j], attn_sinks[j], k_p, v_p, k_p, v_p, k_meta,
                        v_meta, bband_p, bmeta_p, chunk=CHUNK, shared=True, nseq=bsz,
                        tiles_per_seq=seq // tile_rows, chunks_per_tile=ATTN_CHUNKS_PROMPT)
        xp = _ffn_call(xp, norm_ffn[l], w_in_b[l], w_out_b[l])
    y_prompt = xp.reshape(bsz, seq, d)

    k_p4 = k_p.reshape(bsz, seq, N_KV_HEADS, HEAD_DIM)
    v_p4 = v_p.reshape(bsz, seq, N_KV_HEADS, HEAD_DIM)
    return (y_prompt, y_sample, jnp.stack(p_re), jnp.stack(p_im), k_p4[:, -WINDOW:], v_p4[:, -WINDOW:],
            jnp.stack(s_re), jnp.stack(s_im), k_s.reshape(dbsz, dseq, N_KV_HEADS, HEAD_DIM),
            v_s.reshape(dbsz, dseq, N_KV_HEADS, HEAD_DIM))
```

```python
import functools
import math

import jax
import jax.numpy as jnp
from jax import lax
from jax.experimental import pallas as pl
from jax.experimental.pallas import tpu as pltpu

F32 = jnp.float32
BF16 = jnp.bfloat16

CHUNK = 64
N_META = 16
SSM_GROUP = 16
SSM_STATE = 64
HEAD_DIM = 64
N_KV_HEADS = 4
WINDOW = 128
N_BUCKETS = 32
MAX_DISTANCE = 128
PAST_LEN = 1024
EPS = 1e-6
NEG = -1e30

LANES = 128
SUBLANES = 8
VMEM_LIMIT_BYTES = 56 * 1024 * 1024

S5_T = 16
S5_PAIR = 2 * SSM_GROUP
S5_PAIRS = LANES // S5_PAIR
S5_TOKENS = 4096
S5_SCAN_UNROLL = 4
S5_FOLD = 4
ROW_TILE = 512
FFN_CHUNK = 256
ATTN_CHUNKS_PROMPT = 8
ATTN_SEQS_SAMPLE = 8


def _row_tile(n, target):
    if n <= target:
        return n
    best = None
    for t in range(SUBLANES, target + 1, SUBLANES):
        if n % t == 0:
            best = t
    assert best is not None, n
    return best


def _cparams(*sem):
    return pltpu.CompilerParams(dimension_semantics=sem, vmem_limit_bytes=VMEM_LIMIT_BYTES)


def _resident(shape):
    nd = len(shape)
    return pl.BlockSpec(shape, lambda *_: (0,) * nd, pipeline_mode=pl.Buffered(1))


def _rms(x, g):
    ms = jnp.mean(x * x, axis=-1, keepdims=True)
    return (x * lax.rsqrt(ms + EPS)) * g


def _dot(a, b):
    return jnp.dot(a, b, preferred_element_type=F32)


def _dot_nt(a, b):
    return lax.dot_general(a, b, (((1,), (1,)), ((), ())), preferred_element_type=F32)


def _head_rms(x, e, et, gain):
    ss = _dot((x * x).astype(BF16), e)
    rinv = lax.rsqrt(ss * (1.0 / HEAD_DIM) + EPS)
    hi = rinv.astype(BF16)
    lo = (rinv - hi.astype(F32)).astype(BF16)
    rb = _dot(hi, et) + _dot(lo, et)
    return (x * rb) * gain


def _segment_mats(n_heads):
    idx = jnp.arange(n_heads * HEAD_DIM) // HEAD_DIM
    e = (idx[:, None] == jnp.arange(LANES)[None, :]).astype(BF16)
    return e, e.T


def _rms_body(x_ref, g_ref, o_ref):
    o_ref[...] = _rms(x_ref[...], g_ref[...])


def _rms_call(x, g):
    n, d = x.shape
    tm = _row_tile(n, 1024)
    return pl.pallas_call(
        _rms_body,
        grid=(n // tm,),
        in_specs=[pl.BlockSpec((tm, d), lambda i: (i, 0)), _resident((1, d))],
        out_specs=pl.BlockSpec((tm, d), lambda i: (i, 0)),
        out_shape=jax.ShapeDtypeStruct((n, d), F32),
        compiler_params=_cparams("parallel"),
        name="rms",
    )(x, g.reshape(1, d))


def _xpose4(a, lo64, lo32):
    a0, a1, a2, a3 = a
    r02 = pltpu.roll(jnp.where(lo64, a2, a0), 64, 1)
    r13 = pltpu.roll(jnp.where(lo64, a3, a1), 64, 1)
    b0 = jnp.where(lo64, a0, r02)
    b2 = jnp.where(lo64, r02, a2)
    b1 = jnp.where(lo64, a1, r13)
    b3 = jnp.where(lo64, r13, a3)
    return (jnp.where(lo32, b0, pltpu.roll(b1, 32, 1)), jnp.where(lo32, pltpu.roll(b0, 96, 1), b1),
            jnp.where(lo32, b2, pltpu.roll(b3, 32, 1)), jnp.where(lo32, pltpu.roll(b2, 96, 1), b3))


def _s5_body(spb, cps, u0_ref, u1_ref, u2_ref, u3_ref, xb_ref, xc_ref, xd_ref, pw_ref, h0_ref, y_ref, hf_ref,
             s_scr, hin_scr, h_scr, m_scr, wo_scr, wit_scr):
    u_refs = (u0_ref, u1_ref, u2_ref, u3_ref)
    nb = u0_ref.shape[0]
    t = S5_T
    rows_b = spb * cps
    nc = nb * rows_b
    ns = nb * spb
    ncol = 2 * S5_PAIRS
    width = t * S5_PAIR

    @pl.when(pl.program_id(1) == 0)
    def _():
        h_scr[...] = h0_ref[...]
        strip_lane = lax.broadcasted_iota(jnp.int32, (S5_PAIR, width), 1)
        for p in range(S5_PAIRS):
            br, bi = xb_ref[p, 0], xb_ref[p, 1]
            cr, ci = xc_ref[p, 0], xc_ref[p, 1]
            cw = []
            for s in range(t + 1):
                ar, ai = pw_ref[p, 0, s:s + 1, :], pw_ref[p, 1, s:s + 1, :]
                cw.append(jnp.concatenate([cr * ar - ci * ai, -(cr * ai + ci * ar)], axis=1))
            for s in range(t):
                rs = slice(S5_PAIR * s, S5_PAIR * (s + 1))
                ar, ai = pw_ref[p, 0, t - 1 - s:t - s, :], pw_ref[p, 1, t - 1 - s:t - s, :]
                wo_scr[p, rs, 0:LANES] = (br * ar - bi * ai).astype(BF16)
                wo_scr[p, rs, LANES:2 * LANES] = (br * ai + bi * ar).astype(BF16)
                wit_scr[p, rs, :] = cw[s + 1].astype(BF16)
            xbb = jnp.concatenate([br, bi], axis=1)
            cwt = jnp.concatenate(cw[:t], axis=0)
            x_hi, c_hi = xbb.astype(BF16), cwt.astype(BF16)
            x_lo, c_lo = (xbb - x_hi.astype(F32)).astype(BF16), (cwt - c_hi.astype(F32)).astype(BF16)
            strip = _dot_nt(x_hi, c_hi) + (_dot_nt(x_hi, c_lo) + _dot_nt(x_lo, c_hi))
            strip = strip + jnp.concatenate([xd_ref[p], jnp.zeros((S5_PAIR, width - LANES), F32)], axis=1)
            for s in range(t):
                blk = strip if s == 0 else jnp.where(strip_lane >= S5_PAIR * s, pltpu.roll(strip, S5_PAIR * s, 1), 0.0)
                m_scr[p, S5_PAIR * s:S5_PAIR * (s + 1), :] = blk.astype(BF16)

    lane = lax.broadcasted_iota(jnp.int32, (nc, LANES), 1)
    lo64 = lane < 64
    lo32 = (lane % 64) < 32

    v = []
    for tok in range(t):
        parts = [u_refs[tok % S5_FOLD][b, pl.ds(tok // S5_FOLD, rows_b, stride=t // S5_FOLD), :] for b in range(nb)]
        v.append(parts[0] if nb == 1 else jnp.concatenate(parts, axis=0))
    quads = [_xpose4(v[4 * tq:4 * tq + 4], lo64, lo32) for tq in range(t // 4)]
    u_p = [jnp.concatenate([quads[tq][p] for tq in range(t // 4)], axis=1).astype(BF16) for p in range(S5_PAIRS)]

    if S5_PAIRS * ns == SUBLANES:
        for p in range(S5_PAIRS):
            s = _dot(u_p[p], wo_scr[p])
            for q in range(ns):
                rows_q = pl.ds(p * ns + q, cps, stride=SUBLANES)
                s_scr[0, rows_q, :] = s[q * cps:(q + 1) * cps, :LANES]
                s_scr[1, rows_q, :] = s[q * cps:(q + 1) * cps, LANES:]
        ar = jnp.concatenate([jnp.broadcast_to(pw_ref[p, 0, t:t + 1, :], (ns, LANES)) for p in range(S5_PAIRS)], axis=0)
        ai = jnp.concatenate([jnp.broadcast_to(pw_ref[p, 1, t:t + 1, :], (ns, LANES)) for p in range(S5_PAIRS)], axis=0)

        def step(k, carry):
            cr, ci = carry
            rows_k = pl.ds(pl.multiple_of(k * SUBLANES, SUBLANES), SUBLANES)
            hin_scr[0, rows_k, :] = cr
            hin_scr[1, rows_k, :] = ci
            return ar * cr - ai * ci + s_scr[0, rows_k, :], ar * ci + ai * cr + s_scr[1, rows_k, :]

        cr, ci = lax.fori_loop(0, cps, step, (jnp.concatenate([h_scr[2 * p] for p in range(S5_PAIRS)], axis=0),
                                              jnp.concatenate([h_scr[2 * p + 1] for p in range(S5_PAIRS)], axis=0)),
                               unroll=min(cps, S5_SCAN_UNROLL))
        carry = []
        for p in range(S5_PAIRS):
            carry += [cr[p * ns:(p + 1) * ns], ci[p * ns:(p + 1) * ns]]

        def chunk_states(p, plane):
            return jnp.concatenate([hin_scr[plane, pl.ds(p * ns + q, cps, stride=SUBLANES), :] for q in range(ns)],
                                   axis=0)
    else:
        for p in range(S5_PAIRS):
            s = _dot(u_p[p], wo_scr[p])
            s_scr[2 * p, 0:nc, :] = s[:, :LANES]
            s_scr[2 * p + 1, 0:nc, :] = s[:, LANES:]

        def step(k, carry):
            new = []
            for p in range(S5_PAIRS):
                cr, ci = carry[2 * p], carry[2 * p + 1]
                hin_scr[2 * p, pl.ds(k, ns, stride=cps), :] = cr
                hin_scr[2 * p + 1, pl.ds(k, ns, stride=cps), :] = ci
                ar = pw_ref[p, 0, t:t + 1, :]
                ai = pw_ref[p, 1, t:t + 1, :]
                tr = s_scr[2 * p, pl.ds(k, ns, stride=cps), :]
                ti = s_scr[2 * p + 1, pl.ds(k, ns, stride=cps), :]
                new += [ar * cr - ai * ci + tr, ar * ci + ai * cr + ti]
            return tuple(new)

        carry = lax.fori_loop(0, cps, step, tuple(h_scr[c] for c in range(ncol)), unroll=min(cps, S5_SCAN_UNROLL))

        def chunk_states(p, plane):
            return hin_scr[2 * p + plane, 0:nc, :]

    for c in range(ncol):
        h_scr[c] = carry[c]
        hf_ref[c] = carry[c]

    y_p = []
    for p in range(S5_PAIRS):
        h = jnp.concatenate([chunk_states(p, 0), chunk_states(p, 1)], axis=1).astype(BF16)
        y_p.append(_dot(u_p[p], m_scr[p]) + _dot_nt(h, wit_scr[p]))
    for tq in range(t // 4):
        w = _xpose4([y_p[p][:, LANES * tq:LANES * (tq + 1)] for p in range(S5_PAIRS)], lo64, lo32)
        for i in range(4):
            for b in range(nb):
                y_ref[i, b, pl.ds(tq, rows_b, stride=t // S5_FOLD), :] = w[i][b * rows_b:(b + 1) * rows_b]


def _s5_call(u, tabs, h0, spb, cps, tokens):
    xb, xc, xd, pw = tabs
    nb, length, d = u.shape
    assert tokens == spb * cps * S5_T and length % tokens == 0
    ns = nb * spb
    ncol = 2 * S5_PAIRS
    nc_pad = -(-(nb * spb * cps) // SUBLANES) * SUBLANES
    scan_shape = (2, cps * SUBLANES, LANES) if S5_PAIRS * ns == SUBLANES else (ncol, nc_pad, LANES)
    width = S5_T * S5_PAIR
    fold = S5_FOLD
    u = u.reshape(nb, length // fold, fold * d)
    u_specs = [pl.BlockSpec((nb, tokens // fold, LANES), lambda b, r, e=e: (0, r, e * (d // LANES) + b)) for e in range(fold)]
    y_spec = pl.BlockSpec((fold, nb, tokens // fold, LANES), lambda b, r: (0, 0, r, b))
    state = pl.BlockSpec((ncol, ns, LANES), lambda b, r: (b, 0, 0))
    return pl.pallas_call(
        functools.partial(_s5_body, spb, cps),
        grid=(d // LANES, length // tokens),
        in_specs=u_specs + [
            pl.BlockSpec((S5_PAIRS, 2, S5_PAIR, LANES), lambda b, r: (b, 0, 0, 0)),
            pl.BlockSpec((S5_PAIRS, 2, S5_PAIR, LANES), lambda b, r: (b, 0, 0, 0)),
            pl.BlockSpec((S5_PAIRS, S5_PAIR, LANES), lambda b, r: (b, 0, 0)),
            pl.BlockSpec((S5_PAIRS, 2, S5_T + 1, LANES), lambda b, r: (b, 0, 0, 0)),
            state,
        ],
        out_specs=[y_spec, state],
        out_shape=[jax.ShapeDtypeStruct((fold, nb, length // fold, d), F32), jax.ShapeDtypeStruct(h0.shape, F32)],
        scratch_shapes=[pltpu.VMEM(scan_shape, F32), pltpu.VMEM(scan_shape, F32),
                        pltpu.VMEM((ncol, ns, LANES), F32), pltpu.VMEM((S5_PAIRS, width, width), BF16),
                        pltpu.VMEM((S5_PAIRS, width, 2 * LANES), BF16), pltpu.VMEM((S5_PAIRS, width, 2 * LANES), BF16)],
        compiler_params=_cparams("parallel", "arbitrary"),
        name="s5",
    )(u, u, u, u, xb, xc, xd, pw, h0)


def _s5_tables(a_re, a_im, log_dt, b_re, b_im, c_re, c_im, d_skip):
    t = S5_T
    g, p = a_re.shape
    c = b_re.shape[-1]
    dt = jnp.exp(log_dt)[:, None]
    mag = jnp.exp(a_re * dt)
    ang = a_im * dt
    ar = mag * jnp.cos(ang)
    ai = mag * jnp.sin(ang)
    num_re = ar - 1.0
    num_im = ai
    inv = 1.0 / (a_re * a_re + a_im * a_im)
    f_re = (num_re * a_re + num_im * a_im) * inv
    f_im = (num_im * a_re - num_re * a_im) * inv
    bb_re = f_re[..., None] * b_re - f_im[..., None] * b_im
    bb_im = f_re[..., None] * b_im + f_im[..., None] * b_re
    pr = jnp.ones((1, g, p), F32)
    pi = jnp.zeros((1, g, p), F32)
    sr, si = ar, ai
    while pr.shape[0] < t + 1:
        pr, pi = (jnp.concatenate([pr, pr * sr - pi * si]), jnp.concatenate([pi, pr * si + pi * sr]))
        sr, si = sr * sr - si * si, 2.0 * sr * si
    pr = pr[:t + 1]
    pi = pi[:t + 1]
    eye2 = jnp.eye(2, dtype=F32)

    def pad_pair(x):
        x = x.reshape(g // 2, 2, c, p)
        return (x[:, :, :, None, :] * eye2[None, :, None, :, None]).reshape(g // 2, 2 * c, 2 * p)

    xb = jnp.stack([pad_pair(jnp.transpose(bb_re, (0, 2, 1))), pad_pair(jnp.transpose(bb_im, (0, 2, 1)))], axis=1)
    xc = jnp.stack([pad_pair(c_re), pad_pair(c_im)], axis=1)
    xd = d_skip.reshape(g // 2, 2 * c, 1) * jnp.eye(2 * c, LANES, dtype=F32)[None]
    pw = jnp.stack([jnp.transpose(pr.reshape(t + 1, g // 2, 2 * p), (1, 0, 2)),
                    jnp.transpose(pi.reshape(t + 1, g // 2, 2 * p), (1, 0, 2))], axis=1)
    return xb, xc, xd, pw


def _state_to_cols(h_re, h_im):
    ns, g, p = h_re.shape
    f = lambda h: jnp.transpose(h.reshape(ns, g // 2, 2 * p), (1, 0, 2))
    return jnp.stack([f(h_re), f(h_im)], axis=1).reshape(g, ns, 2 * p)


def _cols_to_state(h):
    g, ns, w = h.shape
    h = jnp.transpose(h.reshape(g // 2, 2, ns, 2, w // 2), (1, 2, 0, 3, 4)).reshape(2, ns, g, w // 2)
    return h[0], h[1]


def _ffn_rows(x, g, win_ref, wout_ref):
    dff = wout_ref.shape[0]
    xn = _rms(x, g).astype(BF16)
    acc = x
    for j in range(dff // FFN_CHUNK):
        lo = j * FFN_CHUNK
        gate = _dot(xn, win_ref[:, lo:lo + FFN_CHUNK])
        up = _dot(xn, win_ref[:, dff + lo:dff + lo + FFN_CHUNK])
        a = (jax.nn.silu(gate) * up).astype(BF16)
        acc = acc + _dot(a, wout_ref[lo:lo + FFN_CHUNK, :])
    return acc


def _kv_rows(x, g_ref, w_ref, kn_ref, e_ref, et_ref, k_ref, v_ref):
    kvw = k_ref.shape[1]
    r = _dot(_rms(x, g_ref[...]).astype(BF16), w_ref[...])
    k_ref[...] = _head_rms(r[:, :kvw], e_ref[...], et_ref[...], kn_ref[...])
    v_ref[...] = r[:, kvw:]


def _ffn_body(x_ref, g_ref, win_ref, wout_ref, o_ref):
    o_ref[...] = _ffn_rows(x_ref[...], g_ref[...], win_ref, wout_ref)


def _glu_ffn_body(tail, x_ref, y_ref, wglu_ref, g_ref, win_ref, wout_ref, *rest):
    d = x_ref.shape[1]
    r = _dot(jax.nn.gelu(y_ref[...]).astype(BF16), wglu_ref[...])
    x = x_ref[...] + r[:, :d] * jax.nn.sigmoid(r[:, d:])
    out = _ffn_rows(x, g_ref[...], win_ref, wout_ref)
    if tail == "rms":
        gn_ref, o_ref, u_ref = rest
        u_ref[...] = _rms(out, gn_ref[...])
    else:
        gn_ref, wkv_ref, kn_ref, e_ref, et_ref, o_ref, k_ref, v_ref = rest
        _kv_rows(out, gn_ref, wkv_ref, kn_ref, e_ref, et_ref, k_ref, v_ref)
    o_ref[...] = out


def _ffn_call(x, g, w_in, w_out):
    n, d = x.shape
    tm = _row_tile(n, ROW_TILE)
    row = pl.BlockSpec((tm, d), lambda i: (i, 0))
    return pl.pallas_call(
        _ffn_body,
        grid=(n // tm,),
        in_specs=[row, _resident((1, d)), _resident(w_in.shape), _resident(w_out.shape)],
        out_specs=row,
        out_shape=jax.ShapeDtypeStruct((n, d), F32),
        compiler_params=_cparams("parallel"),
        name="ffn",
    )(x, g.reshape(1, d), w_in, w_out)


def _glu_ffn_call(x, y4, w_glu, g, w_in, w_out, next_norm, kv=None):
    n, d = x.shape
    fold = y4.shape[0]
    nr = n // fold
    tm = _row_tile(nr, ROW_TILE)
    row = pl.BlockSpec((tm, d), lambda e, i: (i, e))
    in_specs = [row, pl.BlockSpec((None, tm, d), lambda e, i: (e, i, 0)), _resident(w_glu.shape), _resident((1, d)),
                _resident(w_in.shape), _resident(w_out.shape), _resident((1, d))]
    args = [x.reshape(nr, fold * d), y4, w_glu, g.reshape(1, d), w_in, w_out, next_norm.reshape(1, d)]
    wide = jax.ShapeDtypeStruct((nr, fold * d), F32)
    if kv is None:
        tail, out_specs, out_shape = "rms", [row, row], [wide, wide]
    else:
        w_kv, k_norm = kv
        kvw = w_kv.shape[1] // 2
        e, et = _segment_mats(kvw // HEAD_DIM)
        kvrow = pl.BlockSpec((tm, kvw), lambda e, i: (i, e))
        in_specs += [_resident(w_kv.shape), _resident((1, kvw)), _resident(e.shape), _resident(et.shape)]
        args += [w_kv, jnp.tile(k_norm, kvw // HEAD_DIM).reshape(1, kvw), e, et]
        tail, out_specs = "kv", [row, kvrow, kvrow]
        out_shape = [wide] + [jax.ShapeDtypeStruct((nr, fold * kvw), F32)] * 2
    outs = pl.pallas_call(
        functools.partial(_glu_ffn_body, tail),
        grid=(fold, nr // tm),
        in_specs=in_specs,
        out_specs=out_specs,
        out_shape=out_shape,
        compiler_params=_cparams("parallel", "parallel"),
        name="glu_ffn_" + tail,
    )(*args)
    return [o.reshape(n, -1) for o in outs]


def _attn_body(chunk, shared, x_ref, g_ref, wq_ref, gain_ref, wo_ref, sink_ref, kown_ref, vown_ref, kleft_ref,
               vleft_ref, kmeta_ref, vmeta_ref, bband_ref, bmeta_ref, e_ref, o_ref,
               q_scr, rt_scr, kd_scr, vt_scr, lg_scr, att_scr):
    rows, d = x_ref.shape
    nchunks = rows // chunk
    n_heads = d // HEAD_DIM
    q_per_kv = n_heads // N_KV_HEADS
    band = WINDOW + chunk
    qlanes = q_per_kv * chunk
    kvw = N_KV_HEADS * HEAD_DIM

    x = x_ref[...]
    q = _dot(_rms(x, g_ref[...]).astype(BF16), wq_ref[...])
    q_scr[...] = q
    ss = _dot_nt(e_ref[...], (q * q).astype(BF16))
    rt_scr[...] = lax.rsqrt(ss * (1.0 / HEAD_DIM) + EPS)
    lane_row = lax.broadcasted_iota(jnp.int32, (1, LANES), 1)

    gain = gain_ref[...]
    low = lax.broadcasted_iota(jnp.int32, (1, LANES), 1) < HEAD_DIM

    def dup(kf):
        out = []
        for kv in range(N_KV_HEADS):
            t = kf[:, (kv // 2) * LANES:(kv // 2 + 1) * LANES]
            r = pltpu.roll(t, HEAD_DIM, 1)
            out.append((jnp.where(low, t, r) if kv % 2 == 0 else jnp.where(low, r, t)).astype(BF16))
        return out

    def put_keys(row0, kf, vf):
        for kv, t in enumerate(dup(kf * gain)):
            kd_scr[kv, row0:row0 + LANES, :] = t
        vt_scr[:, row0:row0 + LANES] = vf.T.astype(BF16)

    km = dup(kmeta_ref[...] * gain)
    vmt = vmeta_ref[...].astype(BF16)
    if shared:
        put_keys(0, kleft_ref[...], vleft_ref[...])
        for t in range(rows // LANES):
            put_keys(WINDOW + t * LANES, kown_ref[t * LANES:(t + 1) * LANES, :], vown_ref[t * LANES:(t + 1) * LANES, :])
        first_chunk = pl.program_id(1) * nchunks
    else:
        zpad = jnp.zeros((LANES - chunk, kvw), F32)
        for j in range(nchunks):
            put_keys(j * 2 * LANES, kleft_ref[j], vleft_ref[j])
            put_keys(j * 2 * LANES + LANES, jnp.concatenate([kown_ref[j * chunk:(j + 1) * chunk, :], zpad], axis=0),
                     jnp.concatenate([vown_ref[j * chunk:(j + 1) * chunk, :], zpad], axis=0))

    for j in range(nchunks):
        r0 = j * chunk
        if shared:
            base = r0
            blocks = [(base + i * chunk, chunk, i * chunk, WINDOW // chunk - i) for i in range(band // chunk)]
        else:
            base = j * 2 * LANES
            blocks = [(base, WINDOW, 0, None), (base + WINDOW, chunk, WINDOW, None)]
        tile0 = (base // LANES) * LANES
        bm = bmeta_ref[j] if shared else bmeta_ref[0]
        for pair in range(N_KV_HEADS // 2):
            o_pair = []
            for kv in (2 * pair, 2 * pair + 1):
                qx = []
                for gq in range(q_per_kv):
                    h = kv * q_per_kv + gq
                    t = q_scr[r0:r0 + chunk, (h // 2) * LANES:(h // 2 + 1) * LANES]
                    qx.append(jnp.where(low if h % 2 == 0 else jnp.logical_not(low), t, 0.0))
                qx = jnp.concatenate(qx, axis=0).astype(BF16)
                s = sink_ref[kv]
                src = (r0 // LANES) * LANES
                tiles = [jnp.zeros((1, LANES), F32) for _ in range(qlanes // LANES)]
                for gq in range(q_per_kv):
                    h = kv * q_per_kv + gq
                    dst = gq * chunk
                    row = pltpu.roll(rt_scr[h:h + 1, src:src + LANES], (dst - r0) % LANES, 1)
                    pick = (lane_row >= dst % LANES) & (lane_row < dst % LANES + chunk)
                    tiles[dst // LANES] = jnp.where(pick, row, tiles[dst // LANES])
                rrow = tiles[0] if len(tiles) == 1 else jnp.concatenate(tiles, axis=1)
                raw = _dot_nt(jnp.concatenate([kd_scr[kv, base:base + band, :], km[kv]], axis=0), qx)
                run = jnp.full((SUBLANES, qlanes), NEG, F32)
                for (k0, nk, b0, back) in blocks:
                    lg = raw[b0:b0 + nk] * rrow + bband_ref[kv, b0:b0 + nk, :]
                    if back and j < back:
                        lg = jnp.where(first_chunk + j >= back, lg, NEG)
                    lg_scr[kv, b0:b0 + nk, :] = lg
                    run = jnp.maximum(run, jnp.max(lg.reshape(nk // SUBLANES, SUBLANES, qlanes), axis=0))
                lgm = raw[band:band + N_META] * rrow + bm[kv]
                run = jnp.maximum(run, jnp.max(lgm.reshape(N_META // SUBLANES, SUBLANES, qlanes), axis=0))
                mx = jnp.maximum(jnp.max(run, axis=0, keepdims=True), s)
                pm = jnp.exp(lgm - mx)
                den = jnp.sum(pm, axis=0, keepdims=True) + jnp.exp(s - mx)
                ot = _dot(vmt[kv * HEAD_DIM:(kv + 1) * HEAD_DIM, :], pm.astype(BF16))
                parts = {}
                for (k0, nk, b0, back) in blocks:
                    p = jnp.exp(lg_scr[kv, b0:b0 + nk, :] - mx)
                    den = den + jnp.sum(p, axis=0, keepdims=True)
                    parts[k0] = p.astype(BF16)
                for t0 in (tile0, tile0 + LANES):
                    segs, pos = [], t0
                    while pos < t0 + LANES:
                        if pos in parts:
                            segs.append(parts[pos])
                            pos += parts[pos].shape[0]
                        else:
                            nxt = min([k0 for k0 in parts if k0 > pos] + [t0 + LANES])
                            segs.append(jnp.zeros((nxt - pos, qlanes), BF16))
                            pos = nxt
                    pt = segs[0] if len(segs) == 1 else jnp.concatenate(segs, axis=0)
                    ot = ot + _dot(vt_scr[kv * HEAD_DIM:(kv + 1) * HEAD_DIM, t0:t0 + LANES], pt)
                o_pair.append(ot / den)
            o2 = jnp.concatenate(o_pair, axis=0).T
            for gq in range(q_per_kv):
                tile = pair * q_per_kv + gq
                att_scr[r0:r0 + chunk, tile * LANES:(tile + 1) * LANES] = o2[gq * chunk:(gq + 1) * chunk, :]

    o_ref[...] = x + _dot(att_scr[...].astype(BF16), wo_ref[...])


def _attn_call(x, g, w_q, q_norm, w_o, sinks, k_own, v_own, k_left, v_left, k_meta, v_meta, b_band, b_meta,
               chunk, shared, nseq, tiles_per_seq, chunks_per_tile):
    n, d = x.shape
    kvw = k_own.shape[1]
    n_heads = d // HEAD_DIM
    q_per_kv = n_heads // N_KV_HEADS
    rows = chunks_per_tile * chunk
    band = WINDOW + chunk
    qlanes = q_per_kv * chunk
    assert rows % LANES == 0 and LANES % chunk == 0
    key_rows = WINDOW + rows if shared else chunks_per_tile * 2 * LANES
    _, e = _segment_mats(n_heads)
    gain = jnp.tile(q_norm * (HEAD_DIM ** -0.5), N_KV_HEADS).reshape(1, kvw)
    sink_row = jnp.broadcast_to(sinks.reshape(N_KV_HEADS, 1, q_per_kv, 1),
                                (N_KV_HEADS, 1, q_per_kv, chunk)).reshape(N_KV_HEADS, 1, qlanes)
    w_o = jnp.concatenate([w_o[((2 * pair + kv2) * q_per_kv + gq) * HEAD_DIM:((2 * pair + kv2) * q_per_kv + gq + 1) * HEAD_DIM]
                           for pair in range(N_KV_HEADS // 2) for gq in range(q_per_kv) for kv2 in range(2)], axis=0)
    row_idx = lambda b, i: (b * tiles_per_seq + i, 0)
    if shared:
        halo_blocks = rows // WINDOW
        left_spec = pl.BlockSpec(
            (WINDOW, kvw), lambda b, i: (jnp.maximum(b * tiles_per_seq * halo_blocks + i * halo_blocks - 1, 0), 0))
        last_tile = b_meta.shape[0] // chunks_per_tile - 1
        bmeta_spec = pl.BlockSpec((chunks_per_tile,) + b_meta.shape[1:],
                                  lambda b, i: (jnp.minimum(i, last_tile), 0, 0, 0))
    else:
        left_spec = pl.BlockSpec((chunks_per_tile, WINDOW, kvw), lambda b, i: (b * tiles_per_seq + i, 0, 0))
        bmeta_spec = _resident(b_meta.shape)
    return pl.pallas_call(
        functools.partial(_attn_body, chunk, shared),
        grid=(nseq, tiles_per_seq),
        in_specs=[
            pl.BlockSpec((rows, d), row_idx), _resident((1, d)), _resident(w_q.shape), _resident((1, kvw)),
            _resident(w_o.shape), _resident(sink_row.shape),
            pl.BlockSpec((rows, kvw), row_idx), pl.BlockSpec((rows, kvw), row_idx), left_spec, left_spec,
            _resident(k_meta.shape), _resident((kvw, N_META)), _resident(b_band.shape), bmeta_spec,
            _resident(e.shape),
        ],
        out_specs=pl.BlockSpec((rows, d), row_idx),
        out_shape=jax.ShapeDtypeStruct((n, d), F32),
        scratch_shapes=[pltpu.VMEM((rows, d), F32), pltpu.VMEM((LANES, rows), F32),
                        pltpu.VMEM((N_KV_HEADS, key_rows, LANES), BF16),
                        pltpu.VMEM((kvw, key_rows), BF16), pltpu.VMEM((N_KV_HEADS, band, qlanes), F32),
                        pltpu.VMEM((rows, d), F32)],
        compiler_params=_cparams("parallel", "arbitrary"),
        name="attn_prompt" if shared else "attn_sample",
    )(x, g.reshape(1, d), w_q, gain, w_o, sink_row, k_own, v_own, k_left, v_left, k_meta, v_meta.T, b_band, b_meta,
      e)


def _rel_bucket(rel):
    half = N_BUCKETS // 2
    max_exact = half // 2
    n = jnp.abs(rel)
    nf = jnp.maximum(n, 1).astype(F32)
    large = max_exact + (jnp.log(nf / max_exact) / math.log(MAX_DISTANCE / max_exact)
                         * (half - max_exact)).astype(jnp.int32)
    large = jnp.minimum(large, half - 1)
    return jnp.where(rel > 0, half, 0) + jnp.where(n < max_exact, n, large)


def _bias_lookup(rel_bias, bucket):
    onehot = (bucket[..., None] == jnp.arange(N_BUCKETS)).astype(F32)
    return jnp.dot(onehot, rel_bias, precision=lax.Precision.HIGHEST)


def _band_bias(rel_bias, chunk):
    n_heads = rel_bias.shape[1]
    q_per_kv = n_heads // N_KV_HEADS
    band = WINDOW + chunk
    rel = jnp.arange(band)[None, :] - WINDOW - jnp.arange(chunk)[:, None]
    b = _bias_lookup(rel_bias, _rel_bucket(rel)).reshape(chunk, band, N_KV_HEADS, q_per_kv)
    return jnp.transpose(b, (2, 1, 3, 0)).reshape(N_KV_HEADS, band, q_per_kv * chunk)


def _meta_bias(t0, count, rel_bias, chunk):
    n_heads = rel_bias.shape[1]
    q_per_kv = n_heads // N_KV_HEADS
    dist = jnp.arange(t0 + 1, t0 + count + N_META + 1)
    tab = _bias_lookup(rel_bias, _rel_bucket(-dist)).T
    b = jnp.stack([tab[:, N_META - 1 - m:N_META - 1 - m + count] for m in range(N_META)])
    b = b.reshape(N_META, N_KV_HEADS, q_per_kv, count // chunk, chunk)
    return jnp.transpose(b, (3, 1, 0, 2, 4)).reshape(count // chunk, N_KV_HEADS, N_META, q_per_kv * chunk)


def _meta_bias_tiles(seq, rel_bias, chunk, tile_rows):
    tiles = min(seq // tile_rows, 2)
    assert tiles == seq // tile_rows or (tiles - 1) * tile_rows + 1 >= MAX_DISTANCE
    return _meta_bias(0, tiles * tile_rows, rel_bias, chunk)


def kernel(x_prompt, x_sample, state_ssm_re, state_ssm_im, cache_k, cache_v, meta_tokens, norm_mix, norm_ffn,
           ssm_a_re, ssm_a_im, ssm_log_dt, ssm_b_re, ssm_b_im, ssm_c_re, ssm_c_im, ssm_d, w_glu, w_ffn_in,
           w_ffn_out, norm_kv, w_kv, k_norm, w_q, q_norm, attn_sinks, w_o, rel_bias):
    bsz, seq, d = x_prompt.shape
    dbsz, dseq, _ = x_sample.shape
    n_a = ssm_a_re.shape[0]
    n_b = w_q.shape[0]
    n_groups, n_state = ssm_a_re.shape[1:]
    kvw = N_KV_HEADS * HEAD_DIM

    cast = lambda w: [w[i].astype(BF16) for i in range(w.shape[0])]
    w_glu_b, w_in_b, w_out_b, w_q_b, w_o_b = cast(w_glu), cast(w_ffn_in), cast(w_ffn_out), cast(w_q), cast(w_o)
    w_kv_b = w_kv.astype(BF16)
    tabs = [_s5_tables(ssm_a_re[l], ssm_a_im[l], ssm_log_dt[l], ssm_b_re[l], ssm_b_im[l], ssm_c_re[l],
                       ssm_c_im[l], ssm_d[l]) for l in range(n_a)]

    def s5_layers(x, view, h0_of_layer, spb, cps, tokens):
        states = []
        u = _rms_call(x, norm_mix[0])
        for l in range(n_a):
            y, hf = _s5_call(u.reshape(view), tabs[l], h0_of_layer(l), spb=spb, cps=cps, tokens=tokens)
            states.append(_cols_to_state(hf))
            args = (x, y.reshape(S5_FOLD, -1, d), w_glu_b[l], norm_ffn[l], w_in_b[l], w_out_b[l])
            if l + 1 < n_a:
                x, u = _glu_ffn_call(*args, norm_mix[l + 1])
            else:
                x, k, v = _glu_ffn_call(*args, norm_kv, kv=(w_kv_b, k_norm))
        return x, k, v, states

    n_s = dbsz * dseq
    n_small = n_s + dseq
    xs = jnp.concatenate([x_sample.reshape(n_s, d), jnp.zeros((dseq - N_META, d), F32), meta_tokens], axis=0)
    zero_state = jnp.zeros((1, n_groups, n_state), F32)
    h0_small = lambda l: _state_to_cols(jnp.concatenate([state_ssm_re[l], zero_state], axis=0),
                                        jnp.concatenate([state_ssm_im[l], zero_state], axis=0))
    xs, k_small, v_small, st_small = s5_layers(xs, (1, n_small, d), h0_small, dbsz + 1, dseq // S5_T, n_small)
    s_re = [hr[:dbsz] for hr, _ in st_small]
    s_im = [hi[:dbsz] for _, hi in st_small]
    k_s, v_s = k_small[:n_s], v_small[:n_s]
    k_meta, v_meta = k_small[n_small - N_META:], v_small[n_small - N_META:]

    bband_s = _band_bias(rel_bias, dseq)
    bmeta_s = _meta_bias(PAST_LEN, dseq, rel_bias, dseq)
    xq = xs[:n_s]
    ck = cache_k.reshape(dbsz, WINDOW, kvw)
    cv = cache_v.reshape(dbsz, WINDOW, kvw)
    for j in range(n_b):
        l = n_a + j
        xq = _attn_call(xq, norm_mix[l], w_q_b[j], q_norm[j], w_o_b[j], attn_sinks[j], k_s, v_s, ck, cv, k_meta,
                        v_meta, bband_s, bmeta_s, chunk=dseq, shared=False, nseq=1,
                        tiles_per_seq=dbsz // ATTN_SEQS_SAMPLE, chunks_per_tile=ATTN_SEQS_SAMPLE)
        xq = _ffn_call(xq, norm_ffn[l], w_in_b[l], w_out_b[l])
    y_sample = xq.reshape(dbsz, dseq, d)

    s5_tokens = min(S5_TOKENS, seq)
    h0_prompt = lambda l: _state_to_cols(jnp.broadcast_to(st_small[l][0][dbsz:], (bsz, n_groups, n_state)),
                                         jnp.broadcast_to(st_small[l][1][dbsz:], (bsz, n_groups, n_state)))
    xp, k_p, v_p, st_prompt = s5_layers(x_prompt.reshape(bsz * seq, d), (bsz, seq, d), h0_prompt, 1,
                                        s5_tokens // S5_T, s5_tokens)
    p_re = [hr for hr, _ in st_prompt]
    p_im = [hi for _, hi in st_prompt]

    bband_p = _band_bias(rel_bias, CHUNK)
    tile_rows = ATTN_CHUNKS_PROMPT * CHUNK
    bmeta_p = _meta_bias_tiles(seq, rel_bias, CHUNK, tile_rows)
    for j in range(n_b):
        l = n_a + j
        xp = _attn_call(xp, norm_mix[l], w_q_b[j], q_norm[j], w_o_b[j], attn_sinks[j], k_p, v_p, k_p, v_p, k_meta,
                        v_meta, bband_p, bmeta_p, chunk=CHUNK, shared=True, nseq=bsz,
                        tiles_per_seq=seq // tile_rows, chunks_per_tile=ATTN_CHUNKS_PROMPT)
        xp = _ffn_call(xp, norm_ffn[l], w_in_b[l], w_out_b[l])
    y_prompt = xp.reshape(bsz, seq, d)

    k_p4 = k_p.reshape(bsz, seq, N_KV_HEADS, HEAD_DIM)
    v_p4 = v_p.reshape(bsz, seq, N_KV_HEADS, HEAD_DIM)
    return (y_prompt, y_sample, jnp.stack(p_re), jnp.stack(p_im), k_p4[:, -WINDOW:], v_p4[:, -WINDOW:],
            jnp.stack(s_re), jnp.stack(s_im), k_s.reshape(dbsz, dseq, N_KV_HEADS, HEAD_DIM),
            v_s.reshape(dbsz, dseq, N_KV_HEADS, HEAD_DIM))
```

```python
import functools
import math

import jax
import jax.numpy as jnp
from jax import lax
from jax.experimental import pallas as pl
from jax.experimental.pallas import tpu as pltpu

F32 = jnp.float32
BF16 = jnp.bfloat16

CHUNK = 64
N_META = 16
SSM_GROUP = 16
SSM_STATE = 64
HEAD_DIM = 64
N_KV_HEADS = 4
WINDOW = 128
N_BUCKETS = 32
MAX_DISTANCE = 128
PAST_LEN = 1024
EPS = 1e-6
NEG = -1e30

LANES = 128
SUBLANES = 8
VMEM_LIMIT_BYTES = 56 * 1024 * 1024

S5_T = 16
S5_PAIR = 2 * SSM_GROUP
S5_PAIRS = LANES // S5_PAIR
S5_TOKENS = 4096
S5_SCAN_UNROLL = 4
ROW_TILE = 512
FFN_CHUNK = 256
ATTN_CHUNKS_PROMPT = 8
ATTN_SEQS_SAMPLE = 8


def _row_tile(n, target):
    if n <= target:
        return n
    best = None
    for t in range(SUBLANES, target + 1, SUBLANES):
        if n % t == 0:
            best = t
    assert best is not None, n
    return best


def _cparams(*sem):
    return pltpu.CompilerParams(dimension_semantics=sem, vmem_limit_bytes=VMEM_LIMIT_BYTES)


def _resident(shape):
    nd = len(shape)
    return pl.BlockSpec(shape, lambda *_: (0,) * nd, pipeline_mode=pl.Buffered(1))


def _rms(x, g):
    ms = jnp.mean(x * x, axis=-1, keepdims=True)
    return (x * lax.rsqrt(ms + EPS)) * g


def _dot(a, b):
    return jnp.dot(a, b, preferred_element_type=F32)


def _dot_nt(a, b):
    return lax.dot_general(a, b, (((1,), (1,)), ((), ())), preferred_element_type=F32)


def _head_rms(x, e, et, gain):
    ss = _dot((x * x).astype(BF16), e)
    rinv = lax.rsqrt(ss * (1.0 / HEAD_DIM) + EPS)
    hi = rinv.astype(BF16)
    lo = (rinv - hi.astype(F32)).astype(BF16)
    rb = _dot(hi, et) + _dot(lo, et)
    return (x * rb) * gain


def _segment_mats(n_heads):
    idx = jnp.arange(n_heads * HEAD_DIM) // HEAD_DIM
    e = (idx[:, None] == jnp.arange(LANES)[None, :]).astype(BF16)
    return e, e.T


def _rms_body(x_ref, g_ref, o_ref):
    o_ref[...] = _rms(x_ref[...], g_ref[...])


def _rms_call(x, g):
    n, d = x.shape
    tm = _row_tile(n, 1024)
    return pl.pallas_call(
        _rms_body,
        grid=(n // tm,),
        in_specs=[pl.BlockSpec((tm, d), lambda i: (i, 0)), _resident((1, d))],
        out_specs=pl.BlockSpec((tm, d), lambda i: (i, 0)),
        out_shape=jax.ShapeDtypeStruct((n, d), F32),
        compiler_params=_cparams("parallel"),
        name="rms",
    )(x, g.reshape(1, d))


def _xpose4(a, lo64, lo32):
    a0, a1, a2, a3 = a
    r02 = pltpu.roll(jnp.where(lo64, a2, a0), 64, 1)
    r13 = pltpu.roll(jnp.where(lo64, a3, a1), 64, 1)
    b0 = jnp.where(lo64, a0, r02)
    b2 = jnp.where(lo64, r02, a2)
    b1 = jnp.where(lo64, a1, r13)
    b3 = jnp.where(lo64, r13, a3)
    return (jnp.where(lo32, b0, pltpu.roll(b1, 32, 1)), jnp.where(lo32, pltpu.roll(b0, 96, 1), b1),
            jnp.where(lo32, b2, pltpu.roll(b3, 32, 1)), jnp.where(lo32, pltpu.roll(b2, 96, 1), b3))


def _s5_body(spb, cps, u_ref, xb_ref, xc_ref, xd_ref, pw_ref, h0_ref, y_ref, hf_ref,
             s_scr, hin_scr, h_scr, m_scr, wo_scr, wit_scr):
    nb = u_ref.shape[0]
    t = S5_T
    rows_b = spb * cps
    nc = nb * rows_b
    ns = nb * spb
    ncol = 2 * S5_PAIRS
    width = t * S5_PAIR

    @pl.when(pl.program_id(1) == 0)
    def _():
        h_scr[...] = h0_ref[...]
        strip_lane = lax.broadcasted_iota(jnp.int32, (S5_PAIR, width), 1)
        for p in range(S5_PAIRS):
            br, bi = xb_ref[p, 0], xb_ref[p, 1]
            cr, ci = xc_ref[p, 0], xc_ref[p, 1]
            cw = []
            for s in range(t + 1):
                ar, ai = pw_ref[p, 0, s:s + 1, :], pw_ref[p, 1, s:s + 1, :]
                cw.append(jnp.concatenate([cr * ar - ci * ai, -(cr * ai + ci * ar)], axis=1))
            for s in range(t):
                rs = slice(S5_PAIR * s, S5_PAIR * (s + 1))
                ar, ai = pw_ref[p, 0, t - 1 - s:t - s, :], pw_ref[p, 1, t - 1 - s:t - s, :]
                wo_scr[p, rs, 0:LANES] = (br * ar - bi * ai).astype(BF16)
                wo_scr[p, rs, LANES:2 * LANES] = (br * ai + bi * ar).astype(BF16)
                wit_scr[p, rs, :] = cw[s + 1].astype(BF16)
            xbb = jnp.concatenate([br, bi], axis=1)
            cwt = jnp.concatenate(cw[:t], axis=0)
            x_hi, c_hi = xbb.astype(BF16), cwt.astype(BF16)
            x_lo, c_lo = (xbb - x_hi.astype(F32)).astype(BF16), (cwt - c_hi.astype(F32)).astype(BF16)
            strip = _dot_nt(x_hi, c_hi) + (_dot_nt(x_hi, c_lo) + _dot_nt(x_lo, c_hi))
            strip = strip + jnp.concatenate([xd_ref[p], jnp.zeros((S5_PAIR, width - LANES), F32)], axis=1)
            for s in range(t):
                blk = strip if s == 0 else jnp.where(strip_lane >= S5_PAIR * s, pltpu.roll(strip, S5_PAIR * s, 1), 0.0)
                m_scr[p, S5_PAIR * s:S5_PAIR * (s + 1), :] = blk.astype(BF16)

    lane = lax.broadcasted_iota(jnp.int32, (nc, LANES), 1)
    lo64 = lane < 64
    lo32 = (lane % 64) < 32

    v = []
    for tok in range(t):
        parts = [u_ref[b, pl.ds(tok, rows_b, stride=t), :] for b in range(nb)]
        v.append(parts[0] if nb == 1 else jnp.concatenate(parts, axis=0))
    quads = [_xpose4(v[4 * tq:4 * tq + 4], lo64, lo32) for tq in range(t // 4)]
    u_p = [jnp.concatenate([quads[tq][p] for tq in range(t // 4)], axis=1).astype(BF16) for p in range(S5_PAIRS)]

    if S5_PAIRS * ns == SUBLANES:
        for p in range(S5_PAIRS):
            s = _dot(u_p[p], wo_scr[p])
            for q in range(ns):
                rows_q = pl.ds(p * ns + q, cps, stride=SUBLANES)
                s_scr[0, rows_q, :] = s[q * cps:(q + 1) * cps, :LANES]
                s_scr[1, rows_q, :] = s[q * cps:(q + 1) * cps, LANES:]
        ar = jnp.concatenate([jnp.broadcast_to(pw_ref[p, 0, t:t + 1, :], (ns, LANES)) for p in range(S5_PAIRS)], axis=0)
        ai = jnp.concatenate([jnp.broadcast_to(pw_ref[p, 1, t:t + 1, :], (ns, LANES)) for p in range(S5_PAIRS)], axis=0)

        def step(k, carry):
            cr, ci = carry
            rows_k = pl.ds(pl.multiple_of(k * SUBLANES, SUBLANES), SUBLANES)
            hin_scr[0, rows_k, :] = cr
            hin_scr[1, rows_k, :] = ci
            return ar * cr - ai * ci + s_scr[0, rows_k, :], ar * ci + ai * cr + s_scr[1, rows_k, :]

        cr, ci = lax.fori_loop(0, cps, step, (jnp.concatenate([h_scr[2 * p] for p in range(S5_PAIRS)], axis=0),
                                              jnp.concatenate([h_scr[2 * p + 1] for p in range(S5_PAIRS)], axis=0)),
                               unroll=min(cps, S5_SCAN_UNROLL))
        carry = []
        for p in range(S5_PAIRS):
            carry += [cr[p * ns:(p + 1) * ns], ci[p * ns:(p + 1) * ns]]

        def chunk_states(p, plane):
            return jnp.concatenate([hin_scr[plane, pl.ds(p * ns + q, cps, stride=SUBLANES), :] for q in range(ns)],
                                   axis=0)
    else:
        for p in range(S5_PAIRS):
            s = _dot(u_p[p], wo_scr[p])
            s_scr[2 * p, 0:nc, :] = s[:, :LANES]
            s_scr[2 * p + 1, 0:nc, :] = s[:, LANES:]

        def step(k, carry):
            new = []
            for p in range(S5_PAIRS):
                cr, ci = carry[2 * p], carry[2 * p + 1]
                hin_scr[2 * p, pl.ds(k, ns, stride=cps), :] = cr
                hin_scr[2 * p + 1, pl.ds(k, ns, stride=cps), :] = ci
                ar = pw_ref[p, 0, t:t + 1, :]
                ai = pw_ref[p, 1, t:t + 1, :]
                tr = s_scr[2 * p, pl.ds(k, ns, stride=cps), :]
                ti = s_scr[2 * p + 1, pl.ds(k, ns, stride=cps), :]
                new += [ar * cr - ai * ci + tr, ar * ci + ai * cr + ti]
            return tuple(new)

        carry = lax.fori_loop(0, cps, step, tuple(h_scr[c] for c in range(ncol)), unroll=min(cps, S5_SCAN_UNROLL))

        def chunk_states(p, plane):
            return hin_scr[2 * p + plane, 0:nc, :]

    for c in range(ncol):
        h_scr[c] = carry[c]
        hf_ref[c] = carry[c]

    y_p = []
    for p in range(S5_PAIRS):
        h = jnp.concatenate([chunk_states(p, 0), chunk_states(p, 1)], axis=1).astype(BF16)
        y_p.append(_dot(u_p[p], m_scr[p]) + _dot_nt(h, wit_scr[p]))
    for tq in range(t // 4):
        w = _xpose4([y_p[p][:, LANES * tq:LANES * (tq + 1)] for p in range(S5_PAIRS)], lo64, lo32)
        for i in range(4):
            for b in range(nb):
                y_ref[b, pl.ds(4 * tq + i, rows_b, stride=t), :] = w[i][b * rows_b:(b + 1) * rows_b]


def _s5_call(u, tabs, h0, spb, cps, tokens):
    xb, xc, xd, pw = tabs
    nb, length, d = u.shape
    assert tokens == spb * cps * S5_T and length % tokens == 0
    ns = nb * spb
    ncol = 2 * S5_PAIRS
    nc_pad = -(-(nb * spb * cps) // SUBLANES) * SUBLANES
    scan_shape = (2, cps * SUBLANES, LANES) if S5_PAIRS * ns == SUBLANES else (ncol, nc_pad, LANES)
    width = S5_T * S5_PAIR
    blk = pl.BlockSpec((nb, tokens, LANES), lambda b, r: (0, r, b))
    state = pl.BlockSpec((ncol, ns, LANES), lambda b, r: (b, 0, 0))
    return pl.pallas_call(
        functools.partial(_s5_body, spb, cps),
        grid=(d // LANES, length // tokens),
        in_specs=[
            blk,
            pl.BlockSpec((S5_PAIRS, 2, S5_PAIR, LANES), lambda b, r: (b, 0, 0, 0)),
            pl.BlockSpec((S5_PAIRS, 2, S5_PAIR, LANES), lambda b, r: (b, 0, 0, 0)),
            pl.BlockSpec((S5_PAIRS, S5_PAIR, LANES), lambda b, r: (b, 0, 0)),
            pl.BlockSpec((S5_PAIRS, 2, S5_T + 1, LANES), lambda b, r: (b, 0, 0, 0)),
            state,
        ],
        out_specs=[blk, state],
        out_shape=[jax.ShapeDtypeStruct(u.shape, F32), jax.ShapeDtypeStruct(h0.shape, F32)],
        scratch_shapes=[pltpu.VMEM(scan_shape, F32), pltpu.VMEM(scan_shape, F32),
                        pltpu.VMEM((ncol, ns, LANES), F32), pltpu.VMEM((S5_PAIRS, width, width), BF16),
                        pltpu.VMEM((S5_PAIRS, width, 2 * LANES), BF16), pltpu.VMEM((S5_PAIRS, width, 2 * LANES), BF16)],
        compiler_params=_cparams("parallel", "arbitrary"),
        name="s5",
    )(u, xb, xc, xd, pw, h0)


def _s5_tables(a_re, a_im, log_dt, b_re, b_im, c_re, c_im, d_skip):
    t = S5_T
    g, p = a_re.shape
    c = b_re.shape[-1]
    dt = jnp.exp(log_dt)[:, None]
    mag = jnp.exp(a_re * dt)
    ang = a_im * dt
    ar = mag * jnp.cos(ang)
    ai = mag * jnp.sin(ang)
    num_re = ar - 1.0
    num_im = ai
    inv = 1.0 / (a_re * a_re + a_im * a_im)
    f_re = (num_re * a_re + num_im * a_im) * inv
    f_im = (num_im * a_re - num_re * a_im) * inv
    bb_re = f_re[..., None] * b_re - f_im[..., None] * b_im
    bb_im = f_re[..., None] * b_im + f_im[..., None] * b_re
    pr = jnp.ones((1, g, p), F32)
    pi = jnp.zeros((1, g, p), F32)
    sr, si = ar, ai
    while pr.shape[0] < t + 1:
        pr, pi = (jnp.concatenate([pr, pr * sr - pi * si]), jnp.concatenate([pi, pr * si + pi * sr]))
        sr, si = sr * sr - si * si, 2.0 * sr * si
    pr = pr[:t + 1]
    pi = pi[:t + 1]
    eye2 = jnp.eye(2, dtype=F32)

    def pad_pair(x):
        x = x.reshape(g // 2, 2, c, p)
        return (x[:, :, :, None, :] * eye2[None, :, None, :, None]).reshape(g // 2, 2 * c, 2 * p)

    xb = jnp.stack([pad_pair(jnp.transpose(bb_re, (0, 2, 1))), pad_pair(jnp.transpose(bb_im, (0, 2, 1)))], axis=1)
    xc = jnp.stack([pad_pair(c_re), pad_pair(c_im)], axis=1)
    xd = d_skip.reshape(g // 2, 2 * c, 1) * jnp.eye(2 * c, LANES, dtype=F32)[None]
    pw = jnp.stack([jnp.transpose(pr.reshape(t + 1, g // 2, 2 * p), (1, 0, 2)),
                    jnp.transpose(pi.reshape(t + 1, g // 2, 2 * p), (1, 0, 2))], axis=1)
    return xb, xc, xd, pw


def _state_to_cols(h_re, h_im):
    ns, g, p = h_re.shape
    f = lambda h: jnp.transpose(h.reshape(ns, g // 2, 2 * p), (1, 0, 2))
    return jnp.stack([f(h_re), f(h_im)], axis=1).reshape(g, ns, 2 * p)


def _cols_to_state(h):
    g, ns, w = h.shape
    h = jnp.transpose(h.reshape(g // 2, 2, ns, 2, w // 2), (1, 2, 0, 3, 4)).reshape(2, ns, g, w // 2)
    return h[0], h[1]


def _ffn_rows(x, g, win_ref, wout_ref):
    dff = wout_ref.shape[0]
    xn = _rms(x, g).astype(BF16)
    acc = x
    for j in range(dff // FFN_CHUNK):
        lo = j * FFN_CHUNK
        gate = _dot(xn, win_ref[:, lo:lo + FFN_CHUNK])
        up = _dot(xn, win_ref[:, dff + lo:dff + lo + FFN_CHUNK])
        a = (jax.nn.silu(gate) * up).astype(BF16)
        acc = acc + _dot(a, wout_ref[lo:lo + FFN_CHUNK, :])
    return acc


def _kv_rows(x, g_ref, w_ref, kn_ref, e_ref, et_ref, k_ref, v_ref):
    kvw = k_ref.shape[1]
    r = _dot(_rms(x, g_ref[...]).astype(BF16), w_ref[...])
    k_ref[...] = _head_rms(r[:, :kvw], e_ref[...], et_ref[...], kn_ref[...])
    v_ref[...] = r[:, kvw:]


def _ffn_body(x_ref, g_ref, win_ref, wout_ref, o_ref):
    o_ref[...] = _ffn_rows(x_ref[...], g_ref[...], win_ref, wout_ref)


def _glu_ffn_body(tail, x_ref, y_ref, wglu_ref, g_ref, win_ref, wout_ref, *rest):
    d = x_ref.shape[1]
    r = _dot(jax.nn.gelu(y_ref[...]).astype(BF16), wglu_ref[...])
    x = x_ref[...] + r[:, :d] * jax.nn.sigmoid(r[:, d:])
    out = _ffn_rows(x, g_ref[...], win_ref, wout_ref)
    if tail == "rms":
        gn_ref, o_ref, u_ref = rest
        u_ref[...] = _rms(out, gn_ref[...])
    else:
        gn_ref, wkv_ref, kn_ref, e_ref, et_ref, o_ref, k_ref, v_ref = rest
        _kv_rows(out, gn_ref, wkv_ref, kn_ref, e_ref, et_ref, k_ref, v_ref)
    o_ref[...] = out


def _ffn_call(x, g, w_in, w_out):
    n, d = x.shape
    tm = _row_tile(n, ROW_TILE)
    row = pl.BlockSpec((tm, d), lambda i: (i, 0))
    return pl.pallas_call(
        _ffn_body,
        grid=(n // tm,),
        in_specs=[row, _resident((1, d)), _resident(w_in.shape), _resident(w_out.shape)],
        out_specs=row,
        out_shape=jax.ShapeDtypeStruct((n, d), F32),
        compiler_params=_cparams("parallel"),
        name="ffn",
    )(x, g.reshape(1, d), w_in, w_out)


def _glu_ffn_call(x, y, w_glu, g, w_in, w_out, next_norm, kv=None):
    n, d = x.shape
    tm = _row_tile(n, ROW_TILE)
    row = pl.BlockSpec((tm, d), lambda i: (i, 0))
    in_specs = [row, row, _resident(w_glu.shape), _resident((1, d)), _resident(w_in.shape), _resident(w_out.shape),
                _resident((1, d))]
    args = [x, y, w_glu, g.reshape(1, d), w_in, w_out, next_norm.reshape(1, d)]
    if kv is None:
        tail, out_specs, out_shape = "rms", [row, row], [jax.ShapeDtypeStruct((n, d), F32)] * 2
    else:
        w_kv, k_norm = kv
        kvw = w_kv.shape[1] // 2
        e, et = _segment_mats(kvw // HEAD_DIM)
        kvrow = pl.BlockSpec((tm, kvw), lambda i: (i, 0))
        in_specs += [_resident(w_kv.shape), _resident((1, kvw)), _resident(e.shape), _resident(et.shape)]
        args += [w_kv, jnp.tile(k_norm, kvw // HEAD_DIM).reshape(1, kvw), e, et]
        tail, out_specs = "kv", [row, kvrow, kvrow]
        out_shape = [jax.ShapeDtypeStruct((n, d), F32)] + [jax.ShapeDtypeStruct((n, kvw), F32)] * 2
    return pl.pallas_call(
        functools.partial(_glu_ffn_body, tail),
        grid=(n // tm,),
        in_specs=in_specs,
        out_specs=out_specs,
        out_shape=out_shape,
        compiler_params=_cparams("parallel"),
        name="glu_ffn_" + tail,
    )(*args)


def _attn_body(chunk, shared, x_ref, g_ref, wq_ref, gain_ref, wo_ref, sink_ref, kown_ref, vown_ref, kleft_ref,
               vleft_ref, kmeta_ref, vmeta_ref, bband_ref, bmeta_ref, e_ref, o_ref,
               q_scr, rt_scr, kd_scr, vt_scr, lg_scr, att_scr):
    rows, d = x_ref.shape
    nchunks = rows // chunk
    n_heads = d // HEAD_DIM
    q_per_kv = n_heads // N_KV_HEADS
    band = WINDOW + chunk
    qlanes = q_per_kv * chunk
    kvw = N_KV_HEADS * HEAD_DIM

    x = x_ref[...]
    q = _dot(_rms(x, g_ref[...]).astype(BF16), wq_ref[...])
    q_scr[...] = q
    ss = _dot_nt(e_ref[...], (q * q).astype(BF16))
    rt_scr[...] = lax.rsqrt(ss * (1.0 / HEAD_DIM) + EPS)
    lane_row = lax.broadcasted_iota(jnp.int32, (1, LANES), 1)

    gain = gain_ref[...]
    low = lax.broadcasted_iota(jnp.int32, (1, LANES), 1) < HEAD_DIM

    def dup(kf):
        out = []
        for kv in range(N_KV_HEADS):
            t = kf[:, (kv // 2) * LANES:(kv // 2 + 1) * LANES]
            r = pltpu.roll(t, HEAD_DIM, 1)
            out.append((jnp.where(low, t, r) if kv % 2 == 0 else jnp.where(low, r, t)).astype(BF16))
        return out

    def put_keys(row0, kf, vf):
        for kv, t in enumerate(dup(kf * gain)):
            kd_scr[kv, row0:row0 + LANES, :] = t
        vt_scr[:, row0:row0 + LANES] = vf.T.astype(BF16)

    km = dup(kmeta_ref[...] * gain)
    vmt = vmeta_ref[...].astype(BF16)
    if shared:
        put_keys(0, kleft_ref[...], vleft_ref[...])
        for t in range(rows // LANES):
            put_keys(WINDOW + t * LANES, kown_ref[t * LANES:(t + 1) * LANES, :], vown_ref[t * LANES:(t + 1) * LANES, :])
        first_chunk = pl.program_id(1) * nchunks
    else:
        zpad = jnp.zeros((LANES - chunk, kvw), F32)
        for j in range(nchunks):
            left = lambda ref: jnp.concatenate([ref[j, :, kv, :] for kv in range(N_KV_HEADS)], axis=1)
            put_keys(j * 2 * LANES, left(kleft_ref), left(vleft_ref))
            put_keys(j * 2 * LANES + LANES, jnp.concatenate([kown_ref[j * chunk:(j + 1) * chunk, :], zpad], axis=0),
                     jnp.concatenate([vown_ref[j * chunk:(j + 1) * chunk, :], zpad], axis=0))

    for j in range(nchunks):
        r0 = j * chunk
        if shared:
            base = r0
            blocks = [(base + i * chunk, chunk, i * chunk, WINDOW // chunk - i) for i in range(band // chunk)]
        else:
            base = j * 2 * LANES
            blocks = [(base, WINDOW, 0, None), (base + WINDOW, chunk, WINDOW, None)]
        tile0 = (base // LANES) * LANES
        bm = bmeta_ref[j] if shared else bmeta_ref[0]
        for pair in range(N_KV_HEADS // 2):
            o_pair = []
            for kv in (2 * pair, 2 * pair + 1):
                qx = []
                for gq in range(q_per_kv):
                    h = kv * q_per_kv + gq
                    t = q_scr[r0:r0 + chunk, (h // 2) * LANES:(h // 2 + 1) * LANES]
                    qx.append(jnp.where(low if h % 2 == 0 else jnp.logical_not(low), t, 0.0))
                qx = jnp.concatenate(qx, axis=0).astype(BF16)
                s = sink_ref[kv]
                src = (r0 // LANES) * LANES
                tiles = [jnp.zeros((1, LANES), F32) for _ in range(qlanes // LANES)]
                for gq in range(q_per_kv):
                    h = kv * q_per_kv + gq
                    dst = gq * chunk
                    row = pltpu.roll(rt_scr[h:h + 1, src:src + LANES], (dst - r0) % LANES, 1)
                    pick = (lane_row >= dst % LANES) & (lane_row < dst % LANES + chunk)
                    tiles[dst // LANES] = jnp.where(pick, row, tiles[dst // LANES])
                rrow = tiles[0] if len(tiles) == 1 else jnp.concatenate(tiles, axis=1)
                raw = _dot_nt(jnp.concatenate([kd_scr[kv, base:base + band, :], km[kv]], axis=0), qx)
                run = jnp.full((SUBLANES, qlanes), NEG, F32)
                for (k0, nk, b0, back) in blocks:
                    lg = raw[b0:b0 + nk] * rrow + bband_ref[kv, b0:b0 + nk, :]
                    if back and j < back:
                        lg = jnp.where(first_chunk + j >= back, lg, NEG)
                    lg_scr[kv, b0:b0 + nk, :] = lg
                    run = jnp.maximum(run, jnp.max(lg.reshape(nk // SUBLANES, SUBLANES, qlanes), axis=0))
                lgm = raw[band:band + N_META] * rrow + bm[kv]
                run = jnp.maximum(run, jnp.max(lgm.reshape(N_META // SUBLANES, SUBLANES, qlanes), axis=0))
                mx = jnp.maximum(jnp.max(run, axis=0, keepdims=True), s)
                pm = jnp.exp(lgm - mx)
                den = jnp.sum(pm, axis=0, keepdims=True) + jnp.exp(s - mx)
                ot = _dot(vmt[kv * HEAD_DIM:(kv + 1) * HEAD_DIM, :], pm.astype(BF16))
                parts = {}
                for (k0, nk, b0, back) in blocks:
                    p = jnp.exp(lg_scr[kv, b0:b0 + nk, :] - mx)
                    den = den + jnp.sum(p, axis=0, keepdims=True)
                    parts[k0] = p.astype(BF16)
                for t0 in (tile0, tile0 + LANES):
                    segs, pos = [], t0
                    while pos < t0 + LANES:
                        if pos in parts:
                            segs.append(parts[pos])
                            pos += parts[pos].shape[0]
                        else:
                            nxt = min([k0 for k0 in parts if k0 > pos] + [t0 + LANES])
                            segs.append(jnp.zeros((nxt - pos, qlanes), BF16))
                            pos = nxt
                    pt = segs[0] if len(segs) == 1 else jnp.concatenate(segs, axis=0)
                    ot = ot + _dot(vt_scr[kv * HEAD_DIM:(kv + 1) * HEAD_DIM, t0:t0 + LANES], pt)
                o_pair.append(ot / den)
            o2 = jnp.concatenate(o_pair, axis=0).T
            for gq in range(q_per_kv):
                tile = pair * q_per_kv + gq
                att_scr[r0:r0 + chunk, tile * LANES:(tile + 1) * LANES] = o2[gq * chunk:(gq + 1) * chunk, :]

    o_ref[...] = x + _dot(att_scr[...].astype(BF16), wo_ref[...])


def _attn_call(x, g, w_q, q_norm, w_o, sinks, k_own, v_own, k_left, v_left, k_meta, v_meta, b_band, b_meta,
               chunk, shared, nseq, tiles_per_seq, chunks_per_tile):
    n, d = x.shape
    kvw = k_own.shape[1]
    n_heads = d // HEAD_DIM
    q_per_kv = n_heads // N_KV_HEADS
    rows = chunks_per_tile * chunk
    band = WINDOW + chunk
    qlanes = q_per_kv * chunk
    assert rows % LANES == 0 and LANES % chunk == 0
    key_rows = WINDOW + rows if shared else chunks_per_tile * 2 * LANES
    _, e = _segment_mats(n_heads)
    gain = jnp.tile(q_norm * (HEAD_DIM ** -0.5), N_KV_HEADS).reshape(1, kvw)
    sink_row = jnp.broadcast_to(sinks.reshape(N_KV_HEADS, 1, q_per_kv, 1),
                                (N_KV_HEADS, 1, q_per_kv, chunk)).reshape(N_KV_HEADS, 1, qlanes)
    w_o = jnp.concatenate([w_o[((2 * pair + kv2) * q_per_kv + gq) * HEAD_DIM:((2 * pair + kv2) * q_per_kv + gq + 1) * HEAD_DIM]
                           for pair in range(N_KV_HEADS // 2) for gq in range(q_per_kv) for kv2 in range(2)], axis=0)
    row_idx = lambda b, i: (b * tiles_per_seq + i, 0)
    if shared:
        halo_blocks = rows // WINDOW
        left_spec = pl.BlockSpec(
            (WINDOW, kvw), lambda b, i: (jnp.maximum(b * tiles_per_seq * halo_blocks + i * halo_blocks - 1, 0), 0))
        last_tile = b_meta.shape[0] // chunks_per_tile - 1
        bmeta_spec = pl.BlockSpec((chunks_per_tile,) + b_meta.shape[1:],
                                  lambda b, i: (jnp.minimum(i, last_tile), 0, 0, 0))
    else:
        left_spec = pl.BlockSpec((chunks_per_tile, WINDOW, N_KV_HEADS, HEAD_DIM),
                                 lambda b, i: (b * tiles_per_seq + i, 0, 0, 0))
        bmeta_spec = _resident(b_meta.shape)
    return pl.pallas_call(
        functools.partial(_attn_body, chunk, shared),
        grid=(nseq, tiles_per_seq),
        in_specs=[
            pl.BlockSpec((rows, d), row_idx), _resident((1, d)), _resident(w_q.shape), _resident((1, kvw)),
            _resident(w_o.shape), _resident(sink_row.shape),
            pl.BlockSpec((rows, kvw), row_idx), pl.BlockSpec((rows, kvw), row_idx), left_spec, left_spec,
            _resident(k_meta.shape), _resident((kvw, N_META)), _resident(b_band.shape), bmeta_spec,
            _resident(e.shape),
        ],
        out_specs=pl.BlockSpec((rows, d), row_idx),
        out_shape=jax.ShapeDtypeStruct((n, d), F32),
        scratch_shapes=[pltpu.VMEM((rows, d), F32), pltpu.VMEM((LANES, rows), F32),
                        pltpu.VMEM((N_KV_HEADS, key_rows, LANES), BF16),
                        pltpu.VMEM((kvw, key_rows), BF16), pltpu.VMEM((N_KV_HEADS, band, qlanes), F32),
                        pltpu.VMEM((rows, d), F32)],
        compiler_params=_cparams("parallel", "arbitrary"),
        name="attn_prompt" if shared else "attn_sample",
    )(x, g.reshape(1, d), w_q, gain, w_o, sink_row, k_own, v_own, k_left, v_left, k_meta, v_meta.T, b_band, b_meta,
      e)


def _rel_bucket(rel):
    half = N_BUCKETS // 2
    max_exact = half // 2
    n = jnp.abs(rel)
    nf = jnp.maximum(n, 1).astype(F32)
    large = max_exact + (jnp.log(nf / max_exact) / math.log(MAX_DISTANCE / max_exact)
                         * (half - max_exact)).astype(jnp.int32)
    large = jnp.minimum(large, half - 1)
    return jnp.where(rel > 0, half, 0) + jnp.where(n < max_exact, n, large)


def _bias_lookup(rel_bias, bucket):
    onehot = (bucket[..., None] == jnp.arange(N_BUCKETS)).astype(F32)
    return jnp.dot(onehot, rel_bias, precision=lax.Precision.HIGHEST)


def _band_bias(rel_bias, chunk):
    n_heads = rel_bias.shape[1]
    q_per_kv = n_heads // N_KV_HEADS
    band = WINDOW + chunk
    rel = jnp.arange(band)[None, :] - WINDOW - jnp.arange(chunk)[:, None]
    b = _bias_lookup(rel_bias, _rel_bucket(rel)).reshape(chunk, band, N_KV_HEADS, q_per_kv)
    return jnp.transpose(b, (2, 1, 3, 0)).reshape(N_KV_HEADS, band, q_per_kv * chunk)


def _meta_bias(t0, count, rel_bias, chunk):
    n_heads = rel_bias.shape[1]
    q_per_kv = n_heads // N_KV_HEADS
    dist = jnp.arange(t0 + 1, t0 + count + N_META + 1)
    tab = _bias_lookup(rel_bias, _rel_bucket(-dist)).T
    b = jnp.stack([tab[:, N_META - 1 - m:N_META - 1 - m + count] for m in range(N_META)])
    b = b.reshape(N_META, N_KV_HEADS, q_per_kv, count // chunk, chunk)
    return jnp.transpose(b, (3, 1, 0, 2, 4)).reshape(count // chunk, N_KV_HEADS, N_META, q_per_kv * chunk)


def _meta_bias_tiles(seq, rel_bias, chunk, tile_rows):
    tiles = min(seq // tile_rows, 2)
    assert tiles == seq // tile_rows or (tiles - 1) * tile_rows + 1 >= MAX_DISTANCE
    return _meta_bias(0, tiles * tile_rows, rel_bias, chunk)


def kernel(x_prompt, x_sample, state_ssm_re, state_ssm_im, cache_k, cache_v, meta_tokens, norm_mix, norm_ffn,
           ssm_a_re, ssm_a_im, ssm_log_dt, ssm_b_re, ssm_b_im, ssm_c_re, ssm_c_im, ssm_d, w_glu, w_ffn_in,
           w_ffn_out, norm_kv, w_kv, k_norm, w_q, q_norm, attn_sinks, w_o, rel_bias):
    bsz, seq, d = x_prompt.shape
    dbsz, dseq, _ = x_sample.shape
    n_a = ssm_a_re.shape[0]
    n_b = w_q.shape[0]
    n_groups, n_state = ssm_a_re.shape[1:]
    kvw = N_KV_HEADS * HEAD_DIM

    cast = lambda w: [w[i].astype(BF16) for i in range(w.shape[0])]
    w_glu_b, w_in_b, w_out_b, w_q_b, w_o_b = cast(w_glu), cast(w_ffn_in), cast(w_ffn_out), cast(w_q), cast(w_o)
    w_kv_b = w_kv.astype(BF16)
    tabs = [_s5_tables(ssm_a_re[l], ssm_a_im[l], ssm_log_dt[l], ssm_b_re[l], ssm_b_im[l], ssm_c_re[l],
                       ssm_c_im[l], ssm_d[l]) for l in range(n_a)]

    def s5_layers(x, view, h0_of_layer, spb, cps, tokens):
        states = []
        u = _rms_call(x, norm_mix[0])
        for l in range(n_a):
            y, hf = _s5_call(u.reshape(view), tabs[l], h0_of_layer(l), spb=spb, cps=cps, tokens=tokens)
            states.append(_cols_to_state(hf))
            args = (x, y.reshape(x.shape), w_glu_b[l], norm_ffn[l], w_in_b[l], w_out_b[l])
            if l + 1 < n_a:
                x, u = _glu_ffn_call(*args, norm_mix[l + 1])
            else:
                x, k, v = _glu_ffn_call(*args, norm_kv, kv=(w_kv_b, k_norm))
        return x, k, v, states

    n_s = dbsz * dseq
    n_small = n_s + dseq
    xs = jnp.concatenate([x_sample.reshape(n_s, d), jnp.zeros((dseq - N_META, d), F32), meta_tokens], axis=0)
    zero_state = jnp.zeros((1, n_groups, n_state), F32)
    h0_small = lambda l: _state_to_cols(jnp.concatenate([state_ssm_re[l], zero_state], axis=0),
                                        jnp.concatenate([state_ssm_im[l], zero_state], axis=0))
    xs, k_small, v_small, st_small = s5_layers(xs, (1, n_small, d), h0_small, dbsz + 1, dseq // S5_T, n_small)
    s_re = [hr[:dbsz] for hr, _ in st_small]
    s_im = [hi[:dbsz] for _, hi in st_small]
    k_s, v_s = k_small[:n_s], v_small[:n_s]
    k_meta, v_meta = k_small[n_small - N_META:], v_small[n_small - N_META:]

    bband_s = _band_bias(rel_bias, dseq)
    bmeta_s = _meta_bias(PAST_LEN, dseq, rel_bias, dseq)
    xq = xs[:n_s]
    for j in range(n_b):
        l = n_a + j
        xq = _attn_call(xq, norm_mix[l], w_q_b[j], q_norm[j], w_o_b[j], attn_sinks[j], k_s, v_s, cache_k, cache_v, k_meta,
                        v_meta, bband_s, bmeta_s, chunk=dseq, shared=False, nseq=1,
                        tiles_per_seq=dbsz // ATTN_SEQS_SAMPLE, chunks_per_tile=ATTN_SEQS_SAMPLE)
        xq = _ffn_call(xq, norm_ffn[l], w_in_b[l], w_out_b[l])
    y_sample = xq.reshape(dbsz, dseq, d)

    s5_tokens = min(S5_TOKENS, seq)
    h0_prompt = lambda l: _state_to_cols(jnp.broadcast_to(st_small[l][0][dbsz:], (bsz, n_groups, n_state)),
                                         jnp.broadcast_to(st_small[l][1][dbsz:], (bsz, n_groups, n_state)))
    xp, k_p, v_p, st_prompt = s5_layers(x_prompt.reshape(bsz * seq, d), (bsz, seq, d), h0_prompt, 1,
                                        s5_tokens // S5_T, s5_tokens)
    p_re = [hr for hr, _ in st_prompt]
    p_im = [hi for _, hi in st_prompt]

    bband_p = _band_bias(rel_bias, CHUNK)
    tile_rows = ATTN_CHUNKS_PROMPT * CHUNK
    bmeta_p = _meta_bias_tiles(seq, rel_bias, CHUNK, tile_rows)
    for j in range(n_b):
        l = n_a + j
        xp = _attn_call(xp, norm_mix[l], w_q_b[j], q_norm[j], w_o_b[j], attn_sinks[j], k_p, v_p, k_p, v_p, k_meta,
                        v_meta, bband_p, bmeta_p, chunk=CHUNK, shared=True, nseq=bsz,
                        tiles_per_seq=seq // tile_rows, chunks_per_tile=ATTN_CHUNKS_PROMPT)
        xp = _ffn_call(xp, norm_ffn[l], w_in_b[l], w_out_b[l])
    y_prompt = xp.reshape(bsz, seq, d)

    k_p4 = k_p.reshape(bsz, seq, N_KV_HEADS, HEAD_DIM)
    v_p4 = v_p.reshape(bsz, seq, N_KV_HEADS, HEAD_DIM)
    return (y_prompt, y_sample, jnp.stack(p_re), jnp.stack(p_im), k_p4[:, -WINDOW:], v_p4[:, -WINDOW:],
            jnp.stack(s_re), jnp.stack(s_im), k_s.reshape(dbsz, dseq, N_KV_HEADS, HEAD_DIM),
            v_s.reshape(dbsz, dseq, N_KV_HEADS, HEAD_DIM))
```

```python
import functools
import math

import jax
import jax.numpy as jnp
from jax import lax
from jax.experimental import pallas as pl
from jax.experimental.pallas import tpu as pltpu

F32 = jnp.float32
BF16 = jnp.bfloat16

CHUNK = 64
N_META = 16
SSM_GROUP = 16
SSM_STATE = 64
HEAD_DIM = 64
N_KV_HEADS = 4
WINDOW = 128
N_BUCKETS = 32
MAX_DISTANCE = 128
PAST_LEN = 1024
EPS = 1e-6
NEG = -1e30

LANES = 128
SUBLANES = 8
VMEM_LIMIT_BYTES = 56 * 1024 * 1024

S5_T = 16
S5_PAIR = 2 * SSM_GROUP
S5_PAIRS = LANES // S5_PAIR
S5_TOKENS = 4096
S5_SCAN_UNROLL = 4
ROW_TILE = 512
FFN_CHUNK = 256
ATTN_CHUNKS_PROMPT = 8
ATTN_SEQS_SAMPLE = 8


def _row_tile(n, target):
    if n <= target:
        return n
    best = None
    for t in range(SUBLANES, target + 1, SUBLANES):
        if n % t == 0:
            best = t
    assert best is not None, n
    return best


def _cparams(*sem):
    return pltpu.CompilerParams(dimension_semantics=sem, vmem_limit_bytes=VMEM_LIMIT_BYTES)


def _resident(shape):
    nd = len(shape)
    return pl.BlockSpec(shape, lambda *_: (0,) * nd, pipeline_mode=pl.Buffered(1))


def _rms(x, g):
    ms = jnp.mean(x * x, axis=-1, keepdims=True)
    return (x * lax.rsqrt(ms + EPS)) * g


def _dot(a, b):
    return jnp.dot(a, b, preferred_element_type=F32)


def _dot_nt(a, b):
    return lax.dot_general(a, b, (((1,), (1,)), ((), ())), preferred_element_type=F32)


def _head_rms(x, e, et, gain):
    ss = _dot((x * x).astype(BF16), e)
    rinv = lax.rsqrt(ss * (1.0 / HEAD_DIM) + EPS)
    hi = rinv.astype(BF16)
    lo = (rinv - hi.astype(F32)).astype(BF16)
    rb = _dot(hi, et) + _dot(lo, et)
    return (x * rb) * gain


def _segment_mats(n_heads):
    idx = jnp.arange(n_heads * HEAD_DIM) // HEAD_DIM
    e = (idx[:, None] == jnp.arange(LANES)[None, :]).astype(BF16)
    return e, e.T


def _rms_body(x_ref, g_ref, o_ref):
    o_ref[...] = _rms(x_ref[...], g_ref[...])


def _rms_call(x, g):
    n, d = x.shape
    tm = _row_tile(n, 1024)
    return pl.pallas_call(
        _rms_body,
        grid=(n // tm,),
        in_specs=[pl.BlockSpec((tm, d), lambda i: (i, 0)), _resident((1, d))],
        out_specs=pl.BlockSpec((tm, d), lambda i: (i, 0)),
        out_shape=jax.ShapeDtypeStruct((n, d), F32),
        compiler_params=_cparams("parallel"),
        name="rms",
    )(x, g.reshape(1, d))


def _xpose4(a, lo64, lo32):
    a0, a1, a2, a3 = a
    r02 = pltpu.roll(jnp.where(lo64, a2, a0), 64, 1)
    r13 = pltpu.roll(jnp.where(lo64, a3, a1), 64, 1)
    b0 = jnp.where(lo64, a0, r02)
    b2 = jnp.where(lo64, r02, a2)
    b1 = jnp.where(lo64, a1, r13)
    b3 = jnp.where(lo64, r13, a3)
    return (jnp.where(lo32, b0, pltpu.roll(b1, 32, 1)), jnp.where(lo32, pltpu.roll(b0, 96, 1), b1),
            jnp.where(lo32, b2, pltpu.roll(b3, 32, 1)), jnp.where(lo32, pltpu.roll(b2, 96, 1), b3))


def _s5_body(spb, cps, u_ref, xb_ref, xc_ref, xd_ref, pw_ref, h0_ref, y_ref, hf_ref,
             s_scr, hin_scr, h_scr, m_scr, wo_scr, wit_scr):
    nb = u_ref.shape[0]
    t = S5_T
    rows_b = spb * cps
    nc = nb * rows_b
    ns = nb * spb
    ncol = 2 * S5_PAIRS
    width = t * S5_PAIR

    @pl.when(pl.program_id(1) == 0)
    def _():
        h_scr[...] = h0_ref[...]
        strip_lane = lax.broadcasted_iota(jnp.int32, (S5_PAIR, width), 1)
        for p in range(S5_PAIRS):
            br, bi = xb_ref[p, 0], xb_ref[p, 1]
            cr, ci = xc_ref[p, 0], xc_ref[p, 1]
            cw = []
            for s in range(t + 1):
                ar, ai = pw_ref[p, 0, s:s + 1, :], pw_ref[p, 1, s:s + 1, :]
                cw.append(jnp.concatenate([cr * ar - ci * ai, -(cr * ai + ci * ar)], axis=1))
            for s in range(t):
                rs = slice(S5_PAIR * s, S5_PAIR * (s + 1))
                ar, ai = pw_ref[p, 0, t - 1 - s:t - s, :], pw_ref[p, 1, t - 1 - s:t - s, :]
                wo_scr[p, rs, 0:LANES] = (br * ar - bi * ai).astype(BF16)
                wo_scr[p, rs, LANES:2 * LANES] = (br * ai + bi * ar).astype(BF16)
                wit_scr[p, rs, :] = cw[s + 1].astype(BF16)
            xbb = jnp.concatenate([br, bi], axis=1)
            cwt = jnp.concatenate(cw[:t], axis=0)
            x_hi, c_hi = xbb.astype(BF16), cwt.astype(BF16)
            x_lo, c_lo = (xbb - x_hi.astype(F32)).astype(BF16), (cwt - c_hi.astype(F32)).astype(BF16)
            strip = _dot_nt(x_hi, c_hi) + (_dot_nt(x_hi, c_lo) + _dot_nt(x_lo, c_hi))
            strip = strip + jnp.concatenate([xd_ref[p], jnp.zeros((S5_PAIR, width - LANES), F32)], axis=1)
            for s in range(t):
                blk = strip if s == 0 else jnp.where(strip_lane >= S5_PAIR * s, pltpu.roll(strip, S5_PAIR * s, 1), 0.0)
                m_scr[p, S5_PAIR * s:S5_PAIR * (s + 1), :] = blk.astype(BF16)

    lane = lax.broadcasted_iota(jnp.int32, (nc, LANES), 1)
    lo64 = lane < 64
    lo32 = (lane % 64) < 32

    v = []
    for tok in range(t):
        parts = [u_ref[b, pl.ds(tok, rows_b, stride=t), :] for b in range(nb)]
        v.append(parts[0] if nb == 1 else jnp.concatenate(parts, axis=0))
    quads = [_xpose4(v[4 * tq:4 * tq + 4], lo64, lo32) for tq in range(t // 4)]
    u_p = [jnp.concatenate([quads[tq][p] for tq in range(t // 4)], axis=1).astype(BF16) for p in range(S5_PAIRS)]

    if S5_PAIRS * ns == SUBLANES:
        for p in range(S5_PAIRS):
            s = _dot(u_p[p], wo_scr[p])
            for q in range(ns):
                rows_q = pl.ds(p * ns + q, cps, stride=SUBLANES)
                s_scr[0, rows_q, :] = s[q * cps:(q + 1) * cps, :LANES]
                s_scr[1, rows_q, :] = s[q * cps:(q + 1) * cps, LANES:]
        ar = jnp.concatenate([jnp.broadcast_to(pw_ref[p, 0, t:t + 1, :], (ns, LANES)) for p in range(S5_PAIRS)], axis=0)
        ai = jnp.concatenate([jnp.broadcast_to(pw_ref[p, 1, t:t + 1, :], (ns, LANES)) for p in range(S5_PAIRS)], axis=0)

        def step(k, carry):
            cr, ci = carry
            rows_k = pl.ds(pl.multiple_of(k * SUBLANES, SUBLANES), SUBLANES)
            hin_scr[0, rows_k, :] = cr
            hin_scr[1, rows_k, :] = ci
            return ar * cr - ai * ci + s_scr[0, rows_k, :], ar * ci + ai * cr + s_scr[1, rows_k, :]

        cr, ci = lax.fori_loop(0, cps, step, (jnp.concatenate([h_scr[2 * p] for p in range(S5_PAIRS)], axis=0),
                                              jnp.concatenate([h_scr[2 * p + 1] for p in range(S5_PAIRS)], axis=0)),
                               unroll=min(cps, S5_SCAN_UNROLL))
        carry = []
        for p in range(S5_PAIRS):
            carry += [cr[p * ns:(p + 1) * ns], ci[p * ns:(p + 1) * ns]]

        def chunk_states(p, plane):
            return jnp.concatenate([hin_scr[plane, pl.ds(p * ns + q, cps, stride=SUBLANES), :] for q in range(ns)],
                                   axis=0)
    else:
        for p in range(S5_PAIRS):
            s = _dot(u_p[p], wo_scr[p])
            s_scr[2 * p, 0:nc, :] = s[:, :LANES]
            s_scr[2 * p + 1, 0:nc, :] = s[:, LANES:]

        def step(k, carry):
            new = []
            for p in range(S5_PAIRS):
                cr, ci = carry[2 * p], carry[2 * p + 1]
                hin_scr[2 * p, pl.ds(k, ns, stride=cps), :] = cr
                hin_scr[2 * p + 1, pl.ds(k, ns, stride=cps), :] = ci
                ar = pw_ref[p, 0, t:t + 1, :]
                ai = pw_ref[p, 1, t:t + 1, :]
                tr = s_scr[2 * p, pl.ds(k, ns, stride=cps), :]
                ti = s_scr[2 * p + 1, pl.ds(k, ns, stride=cps), :]
                new += [ar * cr - ai * ci + tr, ar * ci + ai * cr + ti]
            return tuple(new)

        carry = lax.fori_loop(0, cps, step, tuple(h_scr[c] for c in range(ncol)), unroll=min(cps, S5_SCAN_UNROLL))

        def chunk_states(p, plane):
            return hin_scr[2 * p + plane, 0:nc, :]

    for c in range(ncol):
        h_scr[c] = carry[c]
        hf_ref[c] = carry[c]

    y_p = []
    for p in range(S5_PAIRS):
        h = jnp.concatenate([chunk_states(p, 0), chunk_states(p, 1)], axis=1).astype(BF16)
        y_p.append(_dot(u_p[p], m_scr[p]) + _dot_nt(h, wit_scr[p]))
    for tq in range(t // 4):
        w = _xpose4([y_p[p][:, LANES * tq:LANES * (tq + 1)] for p in range(S5_PAIRS)], lo64, lo32)
        for i in range(4):
            for b in range(nb):
                y_ref[b, pl.ds(4 * tq + i, rows_b, stride=t), :] = w[i][b * rows_b:(b + 1) * rows_b]


def _s5_call(u, tabs, h0, spb, cps, tokens):
    xb, xc, xd, pw = tabs
    nb, length, d = u.shape
    assert tokens == spb * cps * S5_T and length % tokens == 0
    ns = nb * spb
    ncol = 2 * S5_PAIRS
    nc_pad = -(-(nb * spb * cps) // SUBLANES) * SUBLANES
    scan_shape = (2, cps * SUBLANES, LANES) if S5_PAIRS * ns == SUBLANES else (ncol, nc_pad, LANES)
    width = S5_T * S5_PAIR
    blk = pl.BlockSpec((nb, tokens, LANES), lambda b, r: (0, r, b))
    state = pl.BlockSpec((ncol, ns, LANES), lambda b, r: (b, 0, 0))
    return pl.pallas_call(
        functools.partial(_s5_body, spb, cps),
        grid=(d // LANES, length // tokens),
        in_specs=[
            blk,
            pl.BlockSpec((S5_PAIRS, 2, S5_PAIR, LANES), lambda b, r: (b, 0, 0, 0)),
            pl.BlockSpec((S5_PAIRS, 2, S5_PAIR, LANES), lambda b, r: (b, 0, 0, 0)),
            pl.BlockSpec((S5_PAIRS, S5_PAIR, LANES), lambda b, r: (b, 0, 0)),
            pl.BlockSpec((S5_PAIRS, 2, S5_T + 1, LANES), lambda b, r: (b, 0, 0, 0)),
            state,
        ],
        out_specs=[blk, state],
        out_shape=[jax.ShapeDtypeStruct(u.shape, F32), jax.ShapeDtypeStruct(h0.shape, F32)],
        scratch_shapes=[pltpu.VMEM(scan_shape, F32), pltpu.VMEM(scan_shape, F32),
                        pltpu.VMEM((ncol, ns, LANES), F32), pltpu.VMEM((S5_PAIRS, width, width), BF16),
                        pltpu.VMEM((S5_PAIRS, width, 2 * LANES), BF16), pltpu.VMEM((S5_PAIRS, width, 2 * LANES), BF16)],
        compiler_params=_cparams("parallel", "arbitrary"),
        name="s5",
    )(u, xb, xc, xd, pw, h0)


def _s5_tables(a_re, a_im, log_dt, b_re, b_im, c_re, c_im, d_skip):
    t = S5_T
    g, p = a_re.shape
    c = b_re.shape[-1]
    dt = jnp.exp(log_dt)[:, None]
    mag = jnp.exp(a_re * dt)
    ang = a_im * dt
    ar = mag * jnp.cos(ang)
    ai = mag * jnp.sin(ang)
    num_re = ar - 1.0
    num_im = ai
    inv = 1.0 / (a_re * a_re + a_im * a_im)
    f_re = (num_re * a_re + num_im * a_im) * inv
    f_im = (num_im * a_re - num_re * a_im) * inv
    bb_re = f_re[..., None] * b_re - f_im[..., None] * b_im
    bb_im = f_re[..., None] * b_im + f_im[..., None] * b_re
    pr = jnp.ones((1, g, p), F32)
    pi = jnp.zeros((1, g, p), F32)
    sr, si = ar, ai
    while pr.shape[0] < t + 1:
        pr, pi = (jnp.concatenate([pr, pr * sr - pi * si]), jnp.concatenate([pi, pr * si + pi * sr]))
        sr, si = sr * sr - si * si, 2.0 * sr * si
    pr = pr[:t + 1]
    pi = pi[:t + 1]
    eye2 = jnp.eye(2, dtype=F32)

    def pad_pair(x):
        x = x.reshape(g // 2, 2, c, p)
        return (x[:, :, :, None, :] * eye2[None, :, None, :, None]).reshape(g // 2, 2 * c, 2 * p)

    xb = jnp.stack([pad_pair(jnp.transpose(bb_re, (0, 2, 1))), pad_pair(jnp.transpose(bb_im, (0, 2, 1)))], axis=1)
    xc = jnp.stack([pad_pair(c_re), pad_pair(c_im)], axis=1)
    xd = d_skip.reshape(g // 2, 2 * c, 1) * jnp.eye(2 * c, LANES, dtype=F32)[None]
    pw = jnp.stack([jnp.transpose(pr.reshape(t + 1, g // 2, 2 * p), (1, 0, 2)),
                    jnp.transpose(pi.reshape(t + 1, g // 2, 2 * p), (1, 0, 2))], axis=1)
    return xb, xc, xd, pw


def _state_to_cols(h_re, h_im):
    ns, g, p = h_re.shape
    f = lambda h: jnp.transpose(h.reshape(ns, g // 2, 2 * p), (1, 0, 2))
    return jnp.stack([f(h_re), f(h_im)], axis=1).reshape(g, ns, 2 * p)


def _cols_to_state(h):
    g, ns, w = h.shape
    h = jnp.transpose(h.reshape(g // 2, 2, ns, 2, w // 2), (1, 2, 0, 3, 4)).reshape(2, ns, g, w // 2)
    return h[0], h[1]


def _ffn_rows(x, g, win_ref, wout_ref):
    dff = wout_ref.shape[0]
    xn = _rms(x, g).astype(BF16)
    acc = x
    for j in range(dff // FFN_CHUNK):
        lo = j * FFN_CHUNK
        gate = _dot(xn, win_ref[:, lo:lo + FFN_CHUNK])
        up = _dot(xn, win_ref[:, dff + lo:dff + lo + FFN_CHUNK])
        a = (jax.nn.silu(gate) * up).astype(BF16)
        acc = acc + _dot(a, wout_ref[lo:lo + FFN_CHUNK, :])
    return acc


def _kv_rows(x, g_ref, w_ref, kn_ref, e_ref, et_ref, k_ref, v_ref):
    kvw = k_ref.shape[1]
    r = _dot(_rms(x, g_ref[...]).astype(BF16), w_ref[...])
    k_ref[...] = _head_rms(r[:, :kvw], e_ref[...], et_ref[...], kn_ref[...])
    v_ref[...] = r[:, kvw:]


def _ffn_body(x_ref, g_ref, win_ref, wout_ref, o_ref):
    o_ref[...] = _ffn_rows(x_ref[...], g_ref[...], win_ref, wout_ref)


def _glu_ffn_body(tail, x_ref, y_ref, wglu_ref, g_ref, win_ref, wout_ref, *rest):
    d = x_ref.shape[1]
    r = _dot(jax.nn.gelu(y_ref[...]).astype(BF16), wglu_ref[...])
    x = x_ref[...] + r[:, :d] * jax.nn.sigmoid(r[:, d:])
    out = _ffn_rows(x, g_ref[...], win_ref, wout_ref)
    if tail == "rms":
        gn_ref, o_ref, u_ref = rest
        u_ref[...] = _rms(out, gn_ref[...])
    else:
        gn_ref, wkv_ref, kn_ref, e_ref, et_ref, o_ref, k_ref, v_ref = rest
        _kv_rows(out, gn_ref, wkv_ref, kn_ref, e_ref, et_ref, k_ref, v_ref)
    o_ref[...] = out


def _ffn_call(x, g, w_in, w_out):
    n, d = x.shape
    tm = _row_tile(n, ROW_TILE)
    row = pl.BlockSpec((tm, d), lambda i: (i, 0))
    return pl.pallas_call(
        _ffn_body,
        grid=(n // tm,),
        in_specs=[row, _resident((1, d)), _resident(w_in.shape), _resident(w_out.shape)],
        out_specs=row,
        out_shape=jax.ShapeDtypeStruct((n, d), F32),
        compiler_params=_cparams("parallel"),
        name="ffn",
    )(x, g.reshape(1, d), w_in, w_out)


def _glu_ffn_call(x, y, w_glu, g, w_in, w_out, next_norm, kv=None):
    n, d = x.shape
    tm = _row_tile(n, ROW_TILE)
    row = pl.BlockSpec((tm, d), lambda i: (i, 0))
    in_specs = [row, row, _resident(w_glu.shape), _resident((1, d)), _resident(w_in.shape), _resident(w_out.shape),
                _resident((1, d))]
    args = [x, y, w_glu, g.reshape(1, d), w_in, w_out, next_norm.reshape(1, d)]
    if kv is None:
        tail, out_specs, out_shape = "rms", [row, row], [jax.ShapeDtypeStruct((n, d), F32)] * 2
    else:
        w_kv, k_norm = kv
        kvw = w_kv.shape[1] // 2
        e, et = _segment_mats(kvw // HEAD_DIM)
        kvrow = pl.BlockSpec((tm, kvw), lambda i: (i, 0))
        in_specs += [_resident(w_kv.shape), _resident((1, kvw)), _resident(e.shape), _resident(et.shape)]
        args += [w_kv, jnp.tile(k_norm, kvw // HEAD_DIM).reshape(1, kvw), e, et]
        tail, out_specs = "kv", [row, kvrow, kvrow]
        out_shape = [jax.ShapeDtypeStruct((n, d), F32)] + [jax.ShapeDtypeStruct((n, kvw), F32)] * 2
    return pl.pallas_call(
        functools.partial(_glu_ffn_body, tail),
        grid=(n // tm,),
        in_specs=in_specs,
        out_specs=out_specs,
        out_shape=out_shape,
        compiler_params=_cparams("parallel"),
        name="glu_ffn_" + tail,
    )(*args)


def _attn_body(chunk, shared, x_ref, g_ref, wq_ref, gain_ref, wo_ref, sink_ref, kown_ref, vown_ref, kleft_ref,
               vleft_ref, kmeta_ref, vmeta_ref, bband_ref, bmeta_ref, e_ref, o_ref,
               q_scr, rt_scr, kd_scr, vt_scr, lg_scr, att_scr):
    rows, d = x_ref.shape
    nchunks = rows // chunk
    n_heads = d // HEAD_DIM
    q_per_kv = n_heads // N_KV_HEADS
    band = WINDOW + chunk
    qlanes = q_per_kv * chunk
    kvw = N_KV_HEADS * HEAD_DIM

    x = x_ref[...]
    q = _dot(_rms(x, g_ref[...]).astype(BF16), wq_ref[...])
    q_scr[...] = q
    ss = _dot_nt(e_ref[...], (q * q).astype(BF16))
    rt_scr[...] = lax.rsqrt(ss * (1.0 / HEAD_DIM) + EPS)
    lane_row = lax.broadcasted_iota(jnp.int32, (1, LANES), 1)

    gain = gain_ref[...]
    low = lax.broadcasted_iota(jnp.int32, (1, LANES), 1) < HEAD_DIM

    def dup(kf):
        out = []
        for kv in range(N_KV_HEADS):
            t = kf[:, (kv // 2) * LANES:(kv // 2 + 1) * LANES]
            r = pltpu.roll(t, HEAD_DIM, 1)
            out.append((jnp.where(low, t, r) if kv % 2 == 0 else jnp.where(low, r, t)).astype(BF16))
        return out

    def put_keys(row0, kf, vf):
        for kv, t in enumerate(dup(kf * gain)):
            kd_scr[kv, row0:row0 + LANES, :] = t
        vt_scr[:, row0:row0 + LANES] = vf.T.astype(BF16)

    km = dup(kmeta_ref[...] * gain)
    vmt = vmeta_ref[...].astype(BF16)
    if shared:
        put_keys(0, kleft_ref[...], vleft_ref[...])
        for t in range(rows // LANES):
            put_keys(WINDOW + t * LANES, kown_ref[t * LANES:(t + 1) * LANES, :], vown_ref[t * LANES:(t + 1) * LANES, :])
        first_chunk = pl.program_id(1) * nchunks
    else:
        zpad = jnp.zeros((LANES - chunk, kvw), F32)
        for j in range(nchunks):
            left = lambda ref: jnp.concatenate([ref[j, :, kv, :] for kv in range(N_KV_HEADS)], axis=1)
            put_keys(j * 2 * LANES, left(kleft_ref), left(vleft_ref))
            put_keys(j * 2 * LANES + LANES, jnp.concatenate([kown_ref[j * chunk:(j + 1) * chunk, :], zpad], axis=0),
                     jnp.concatenate([vown_ref[j * chunk:(j + 1) * chunk, :], zpad], axis=0))

    for j in range(nchunks):
        r0 = j * chunk
        if shared:
            base = r0
            blocks = [(base + i * chunk, chunk, i * chunk, WINDOW // chunk - i) for i in range(band // chunk)]
        else:
            base = j * 2 * LANES
            blocks = [(base, WINDOW, 0, None), (base + WINDOW, chunk, WINDOW, None)]
        tile0 = (base // LANES) * LANES
        bm = bmeta_ref[j] if shared else bmeta_ref[0]
        for pair in range(N_KV_HEADS // 2):
            o_pair = []
            for kv in (2 * pair, 2 * pair + 1):
                qx = []
                for gq in range(q_per_kv):
                    h = kv * q_per_kv + gq
                    t = q_scr[r0:r0 + chunk, (h // 2) * LANES:(h // 2 + 1) * LANES]
                    qx.append(jnp.where(low if h % 2 == 0 else jnp.logical_not(low), t, 0.0))
                qx = jnp.concatenate(qx, axis=0).astype(BF16)
                s = sink_ref[kv]
                src = (r0 // LANES) * LANES
                tiles = [jnp.zeros((1, LANES), F32) for _ in range(qlanes // LANES)]
                for gq in range(q_per_kv):
                    h = kv * q_per_kv + gq
                    dst = gq * chunk
                    row = pltpu.roll(rt_scr[h:h + 1, src:src + LANES], (dst - r0) % LANES, 1)
                    pick = (lane_row >= dst % LANES) & (lane_row < dst % LANES + chunk)
                    tiles[dst // LANES] = jnp.where(pick, row, tiles[dst // LANES])
                rrow = tiles[0] if len(tiles) == 1 else jnp.concatenate(tiles, axis=1)
                raw = _dot_nt(jnp.concatenate([kd_scr[kv, base:base + band, :], km[kv]], axis=0), qx)
                run = jnp.full((SUBLANES, qlanes), NEG, F32)
                for (k0, nk, b0, back) in blocks:
                    lg = raw[b0:b0 + nk] * rrow + bband_ref[kv, b0:b0 + nk, :]
                    if back and j < back:
                        lg = jnp.where(first_chunk + j >= back, lg, NEG)
                    lg_scr[kv, b0:b0 + nk, :] = lg
                    run = jnp.maximum(run, jnp.max(lg.reshape(nk // SUBLANES, SUBLANES, qlanes), axis=0))
                lgm = raw[band:band + N_META] * rrow + bm[kv]
                run = jnp.maximum(run, jnp.max(lgm.reshape(N_META // SUBLANES, SUBLANES, qlanes), axis=0))
                mx = jnp.maximum(jnp.max(run, axis=0, keepdims=True), s)
                pm = jnp.exp(lgm - mx)
                den = jnp.sum(pm, axis=0, keepdims=True) + jnp.exp(s - mx)
                ot = _dot(vmt[kv * HEAD_DIM:(kv + 1) * HEAD_DIM, :], pm.astype(BF16))
                parts = {}
                for (k0, nk, b0, back) in blocks:
                    p = jnp.exp(lg_scr[kv, b0:b0 + nk, :] - mx)
                    den = den + jnp.sum(p, axis=0, keepdims=True)
                    parts[k0] = p.astype(BF16)
                for t0 in (tile0, tile0 + LANES):
                    segs, pos = [], t0
                    while pos < t0 + LANES:
                        if pos in parts:
                            segs.append(parts[pos])
                            pos += parts[pos].shape[0]
                        else:
                            nxt = min([k0 for k0 in parts if k0 > pos] + [t0 + LANES])
                            segs.append(jnp.zeros((nxt - pos, qlanes), BF16))
                            pos = nxt
                    pt = segs[0] if len(segs) == 1 else jnp.concatenate(segs, axis=0)
                    ot = ot + _dot(vt_scr[kv * HEAD_DIM:(kv + 1) * HEAD_DIM, t0:t0 + LANES], pt)
                o_pair.append(ot / den)
            o2 = jnp.concatenate(o_pair, axis=0).T
            for gq in range(q_per_kv):
                tile = pair * q_per_kv + gq
                att_scr[r0:r0 + chunk, tile * LANES:(tile + 1) * LANES] = o2[gq * chunk:(gq + 1) * chunk, :]

    o_ref[...] = x + _dot(att_scr[...].astype(BF16), wo_ref[...])


def _attn_call(x, g, w_q, q_norm, w_o, sinks, k_own, v_own, k_left, v_left, k_meta, v_meta, b_band, b_meta,
               chunk, shared, nseq, tiles_per_seq, chunks_per_tile):
    n, d = x.shape
    kvw = k_own.shape[1]
    n_heads = d // HEAD_DIM
    q_per_kv = n_heads // N_KV_HEADS
    rows = chunks_per_tile * chunk
    band = WINDOW + chunk
    qlanes = q_per_kv * chunk
    assert rows % LANES == 0 and LANES % chunk == 0
    key_rows = WINDOW + rows if shared else chunks_per_tile * 2 * LANES
    _, e = _segment_mats(n_heads)
    gain = jnp.tile(q_norm * (HEAD_DIM ** -0.5), N_KV_HEADS).reshape(1, kvw)
    sink_row = jnp.broadcast_to(sinks.reshape(N_KV_HEADS, 1, q_per_kv, 1),
                                (N_KV_HEADS, 1, q_per_kv, chunk)).reshape(N_KV_HEADS, 1, qlanes)
    w_o = jnp.concatenate([w_o[((2 * pair + kv2) * q_per_kv + gq) * HEAD_DIM:((2 * pair + kv2) * q_per_kv + gq + 1) * HEAD_DIM]
                           for pair in range(N_KV_HEADS // 2) for gq in range(q_per_kv) for kv2 in range(2)], axis=0)
    row_idx = lambda b, i: (b * tiles_per_seq + i, 0)
    if shared:
        halo_blocks = rows // WINDOW
        left_spec = pl.BlockSpec(
            (WINDOW, kvw), lambda b, i: (jnp.maximum(b * tiles_per_seq * halo_blocks + i * halo_blocks - 1, 0), 0))
        last_tile = b_meta.shape[0] // chunks_per_tile - 1
        bmeta_spec = pl.BlockSpec((chunks_per_tile,) + b_meta.shape[1:],
                                  lambda b, i: (jnp.minimum(i, last_tile), 0, 0, 0))
    else:
        left_spec = pl.BlockSpec((chunks_per_tile, WINDOW, N_KV_HEADS, HEAD_DIM),
                                 lambda b, i: (b * tiles_per_seq + i, 0, 0, 0))
        bmeta_spec = _resident(b_meta.shape)
    return pl.pallas_call(
        functools.partial(_attn_body, chunk, shared),
        grid=(nseq, tiles_per_seq),
        in_specs=[
            pl.BlockSpec((rows, d), row_idx), _resident((1, d)), _resident(w_q.shape), _resident((1, kvw)),
            _resident(w_o.shape), _resident(sink_row.shape),
            pl.BlockSpec((rows, kvw), row_idx), pl.BlockSpec((rows, kvw), row_idx), left_spec, left_spec,
            _resident(k_meta.shape), _resident((kvw, N_META)), _resident(b_band.shape), bmeta_spec,
            _resident(e.shape),
        ],
        out_specs=pl.BlockSpec((rows, d), row_idx),
        out_shape=jax.ShapeDtypeStruct((n, d), F32),
        scratch_shapes=[pltpu.VMEM((rows, d), F32), pltpu.VMEM((LANES, rows), F32),
                        pltpu.VMEM((N_KV_HEADS, key_rows, LANES), BF16),
                        pltpu.VMEM((kvw, key_rows), BF16), pltpu.VMEM((N_KV_HEADS, band, qlanes), F32),
                        pltpu.VMEM((rows, d), F32)],
        compiler_params=_cparams("parallel", "arbitrary"),
        name="attn_prompt" if shared else "attn_sample",
    )(x, g.reshape(1, d), w_q, gain, w_o, sink_row, k_own, v_own, k_left, v_left, k_meta, v_meta.T, b_band, b_meta,
      e)


def _rel_bucket(rel):
    half = N_BUCKETS // 2
    max_exact = half // 2
    n = jnp.abs(rel)
    nf = jnp.maximum(n, 1).astype(F32)
    large = max_exact + (jnp.log(nf / max_exact) / math.log(MAX_DISTANCE / max_exact)
                         * (half - max_exact)).astype(jnp.int32)
    large = jnp.minimum(large, half - 1)
    return jnp.where(rel > 0, half, 0) + jnp.where(n < max_exact, n, large)


def _bias_lookup(rel_bias, bucket):
    onehot = (bucket[..., None] == jnp.arange(N_BUCKETS)).astype(F32)
    return jnp.dot(onehot, rel_bias, precision=lax.Precision.HIGHEST)


def _band_bias(rel_bias, chunk):
    n_heads = rel_bias.shape[1]
    q_per_kv = n_heads // N_KV_HEADS
    band = WINDOW + chunk
    rel = jnp.arange(band)[None, :] - WINDOW - jnp.arange(chunk)[:, None]
    b = _bias_lookup(rel_bias, _rel_bucket(rel)).reshape(chunk, band, N_KV_HEADS, q_per_kv)
    return jnp.transpose(b, (2, 1, 3, 0)).reshape(N_KV_HEADS, band, q_per_kv * chunk)


def _meta_bias(t0, count, rel_bias, chunk):
    n_heads = rel_bias.shape[1]
    q_per_kv = n_heads // N_KV_HEADS
    dist = jnp.arange(t0 + 1, t0 + count + N_META + 1)
    tab = _bias_lookup(rel_bias, _rel_bucket(-dist)).T
    b = jnp.stack([tab[:, N_META - 1 - m:N_META - 1 - m + count] for m in range(N_META)])
    b = b.reshape(N_META, N_KV_HEADS, q_per_kv, count // chunk, chunk)
    return jnp.transpose(b, (3, 1, 0, 2, 4)).reshape(count // chunk, N_KV_HEADS, N_META, q_per_kv * chunk)


def _meta_bias_tiles(seq, rel_bias, chunk, tile_rows):
    tiles = min(seq // tile_rows, 2)
    assert tiles == seq // tile_rows or (tiles - 1) * tile_rows + 1 >= MAX_DISTANCE
    return _meta_bias(0, tiles * tile_rows, rel_bias, chunk)


def kernel(x_prompt, x_sample, state_ssm_re, state_ssm_im, cache_k, cache_v, meta_tokens, norm_mix, norm_ffn,
           ssm_a_re, ssm_a_im, ssm_log_dt, ssm_b_re, ssm_b_im, ssm_c_re, ssm_c_im, ssm_d, w_glu, w_ffn_in,
           w_ffn_out, norm_kv, w_kv, k_norm, w_q, q_norm, attn_sinks, w_o, rel_bias):
    bsz, seq, d = x_prompt.shape
    dbsz, dseq, _ = x_sample.shape
    n_a = ssm_a_re.shape[0]
    n_b = w_q.shape[0]
    n_groups, n_state = ssm_a_re.shape[1:]
    kvw = N_KV_HEADS * HEAD_DIM

    cast = lambda w: [w[i].astype(BF16) for i in range(w.shape[0])]
    w_glu_b, w_in_b, w_out_b, w_q_b, w_o_b = cast(w_glu), cast(w_ffn_in), cast(w_ffn_out), cast(w_q), cast(w_o)
    w_kv_b = w_kv.astype(BF16)
    tabs = [_s5_tables(ssm_a_re[l], ssm_a_im[l], ssm_log_dt[l], ssm_b_re[l], ssm_b_im[l], ssm_c_re[l],
                       ssm_c_im[l], ssm_d[l]) for l in range(n_a)]

    def s5_layers(x, view, h0_of_layer, spb, cps, tokens):
        states = []
        u = _rms_call(x, norm_mix[0])
        for l in range(n_a):
            y, hf = _s5_call(u.reshape(view), tabs[l], h0_of_layer(l), spb=spb, cps=cps, tokens=tokens)
            states.append(_cols_to_state(hf))
            args = (x, y.reshape(x.shape), w_glu_b[l], norm_ffn[l], w_in_b[l], w_out_b[l])
            if l + 1 < n_a:
                x, u = _glu_ffn_call(*args, norm_mix[l + 1])
            else:
                x, k, v = _glu_ffn_call(*args, norm_kv, kv=(w_kv_b, k_norm))
        return x, k, v, states

    n_s = dbsz * dseq
    n_small = n_s + dseq
    xs = jnp.concatenate([x_sample.reshape(n_s, d), jnp.zeros((dseq - N_META, d), F32), meta_tokens], axis=0)
    zero_state = jnp.zeros((1, n_groups, n_state), F32)
    h0_small = lambda l: _state_to_cols(jnp.concatenate([state_ssm_re[l], zero_state], axis=0),
                                        jnp.concatenate([state_ssm_im[l], zero_state], axis=0))
    xs, k_small, v_small, st_small = s5_layers(xs, (1, n_small, d), h0_small, dbsz + 1, dseq // S5_T, n_small)
    s_re = [hr[:dbsz] for hr, _ in st_small]
    s_im = [hi[:dbsz] for _, hi in st_small]
    k_s, v_s = k_small[:n_s], v_small[:n_s]
    k_meta, v_meta = k_small[n_small - N_META:], v_small[n_small - N_META:]

    bband_s = _band_bias(rel_bias, dseq)
    bmeta_s = _meta_bias(PAST_LEN, dseq, rel_bias, dseq)
    xq = xs[:n_s]
    for j in range(n_b):
        l = n_a + j
        xq = _attn_call(xq, norm_mix[l], w_q_b[j], q_norm[j], w_o_b[j], attn_sinks[j], k_s, v_s, cache_k, cache_v, k_meta,
                        v_meta, bband_s, bmeta_s, chunk=dseq, shared=False, nseq=1,
                        tiles_per_seq=dbsz // ATTN_SEQS_SAMPLE, chunks_per_tile=ATTN_SEQS_SAMPLE)
        xq = _ffn_call(xq, norm_ffn[l], w_in_b[l], w_out_b[l])
    y_sample = xq.reshape(dbsz, dseq, d)

    s5_tokens = min(S5_TOKENS, seq)
    h0_prompt = lambda l: _state_to_cols(jnp.broadcast_to(st_small[l][0][dbsz:], (bsz, n_groups, n_state)),
                                         jnp.broadcast_to(st_small[l][1][dbsz:], (bsz, n_groups, n_state)))
    xp, k_p, v_p, st_prompt = s5_layers(x_prompt.reshape(bsz * seq, d), (bsz, seq, d), h0_prompt, 1,
                                        s5_tokens // S5_T, s5_tokens)
    p_re = [hr for hr, _ in st_prompt]
    p_im = [hi for _, hi in st_prompt]

    bband_p = _band_bias(rel_bias, CHUNK)
    tile_rows = ATTN_CHUNKS_PROMPT * CHUNK
    bmeta_p = _meta_bias_tiles(seq, rel_bias, CHUNK, tile_rows)
    for j in range(n_b):
        l = n_a + j
        xp = _attn_call(xp, norm_mix[l], w_q_b[j], q_norm[j], w_o_b[j], attn_sinks[j], k_p, v_p, k_p, v_p, k_meta,
                        v_meta, bband_p, bmeta_p, chunk=CHUNK, shared=True, nseq=bsz,
                        tiles_per_seq=seq // tile_rows, chunks_per_tile=ATTN_CHUNKS_PROMPT)
        xp = _ffn_call(xp, norm_ffn[l], w_in_b[l], w_out_b[l])
    y_prompt = xp.reshape(bsz, seq, d)

    tail = lambda a: a.reshape(bsz, seq, kvw)[:, -WINDOW:].reshape(bsz, WINDOW, N_KV_HEADS, HEAD_DIM)
    return (y_prompt, y_sample, jnp.stack(p_re), jnp.stack(p_im), tail(k_p), tail(v_p),
            jnp.stack(s_re), jnp.stack(s_im), k_s.reshape(dbsz, dseq, N_KV_HEADS, HEAD_DIM),
            v_s.reshape(dbsz, dseq, N_KV_HEADS, HEAD_DIM))
```

```python
import functools
import math

import jax
import jax.numpy as jnp
from jax import lax
from jax.experimental import pallas as pl
from jax.experimental.pallas import tpu as pltpu

F32 = jnp.float32
BF16 = jnp.bfloat16

CHUNK = 64
N_META = 16
SSM_GROUP = 16
SSM_STATE = 64
HEAD_DIM = 64
N_KV_HEADS = 4
WINDOW = 128
N_BUCKETS = 32
MAX_DISTANCE = 128
PAST_LEN = 1024
EPS = 1e-6
NEG = -1e30

LANES = 128
SUBLANES = 8
VMEM_LIMIT_BYTES = 56 * 1024 * 1024

S5_T = 16
S5_PAIR = 2 * SSM_GROUP
S5_PAIRS = LANES // S5_PAIR
S5_TOKENS = 4096
S5_SCAN_UNROLL = 4
ROW_TILE = 512
FFN_CHUNK = 256
ATTN_CHUNKS_PROMPT = 8
ATTN_SEQS_SAMPLE = 8


def _row_tile(n, target):
    if n <= target:
        return n
    best = None
    for t in range(SUBLANES, target + 1, SUBLANES):
        if n % t == 0:
            best = t
    assert best is not None, n
    return best


def _cparams(*sem):
    return pltpu.CompilerParams(dimension_semantics=sem, vmem_limit_bytes=VMEM_LIMIT_BYTES)


def _resident(shape):
    nd = len(shape)
    return pl.BlockSpec(shape, lambda *_: (0,) * nd, pipeline_mode=pl.Buffered(1))


def _rms(x, g):
    ms = jnp.mean(x * x, axis=-1, keepdims=True)
    return (x * lax.rsqrt(ms + EPS)) * g


def _dot(a, b):
    return jnp.dot(a, b, preferred_element_type=F32)


def _dot_nt(a, b):
    return lax.dot_general(a, b, (((1,), (1,)), ((), ())), preferred_element_type=F32)


def _head_rms(x, e, et, gain):
    ss = _dot((x * x).astype(BF16), e)
    rinv = lax.rsqrt(ss * (1.0 / HEAD_DIM) + EPS)
    hi = rinv.astype(BF16)
    lo = (rinv - hi.astype(F32)).astype(BF16)
    rb = _dot(hi, et) + _dot(lo, et)
    return (x * rb) * gain


def _segment_mats(n_heads):
    idx = jnp.arange(n_heads * HEAD_DIM) // HEAD_DIM
    e = (idx[:, None] == jnp.arange(LANES)[None, :]).astype(BF16)
    return e, e.T


def _rms_body(x_ref, g_ref, o_ref):
    o_ref[...] = _rms(x_ref[...], g_ref[...])


def _rms_call(x, g):
    n, d = x.shape
    tm = _row_tile(n, 1024)
    return pl.pallas_call(
        _rms_body,
        grid=(n // tm,),
        in_specs=[pl.BlockSpec((tm, d), lambda i: (i, 0)), _resident((1, d))],
        out_specs=pl.BlockSpec((tm, d), lambda i: (i, 0)),
        out_shape=jax.ShapeDtypeStruct((n, d), F32),
        compiler_params=_cparams("parallel"),
        name="rms",
    )(x, g.reshape(1, d))


def _xpose4(a, lo64, lo32):
    a0, a1, a2, a3 = a
    r02 = pltpu.roll(jnp.where(lo64, a2, a0), 64, 1)
    r13 = pltpu.roll(jnp.where(lo64, a3, a1), 64, 1)
    b0 = jnp.where(lo64, a0, r02)
    b2 = jnp.where(lo64, r02, a2)
    b1 = jnp.where(lo64, a1, r13)
    b3 = jnp.where(lo64, r13, a3)
    return (jnp.where(lo32, b0, pltpu.roll(b1, 32, 1)), jnp.where(lo32, pltpu.roll(b0, 96, 1), b1),
            jnp.where(lo32, b2, pltpu.roll(b3, 32, 1)), jnp.where(lo32, pltpu.roll(b2, 96, 1), b3))


def _s5_body(spb, cps, u_ref, xb_ref, xc_ref, xd_ref, pw_ref, h0_ref, y_ref, hf_ref,
             s_scr, hin_scr, h_scr, m_scr, wo_scr, wit_scr):
    nb = u_ref.shape[0]
    t = S5_T
    rows_b = spb * cps
    nc = nb * rows_b
    ns = nb * spb
    ncol = 2 * S5_PAIRS
    width = t * S5_PAIR

    @pl.when(pl.program_id(1) == 0)
    def _():
        h_scr[...] = h0_ref[...]
        strip_lane = lax.broadcasted_iota(jnp.int32, (S5_PAIR, width), 1)
        for p in range(S5_PAIRS):
            br, bi = xb_ref[p, 0], xb_ref[p, 1]
            cr, ci = xc_ref[p, 0], xc_ref[p, 1]
            cw = []
            for s in range(t + 1):
                ar, ai = pw_ref[p, 0, s:s + 1, :], pw_ref[p, 1, s:s + 1, :]
                cw.append(jnp.concatenate([cr * ar - ci * ai, -(cr * ai + ci * ar)], axis=1))
            for s in range(t):
                rs = slice(S5_PAIR * s, S5_PAIR * (s + 1))
                ar, ai = pw_ref[p, 0, t - 1 - s:t - s, :], pw_ref[p, 1, t - 1 - s:t - s, :]
                wo_scr[p, rs, 0:LANES] = (br * ar - bi * ai).astype(BF16)
                wo_scr[p, rs, LANES:2 * LANES] = (br * ai + bi * ar).astype(BF16)
                wit_scr[p, rs, :] = cw[s + 1].astype(BF16)
            xbb = jnp.concatenate([br, bi], axis=1)
            cwt = jnp.concatenate(cw[:t], axis=0)
            x_hi, c_hi = xbb.astype(BF16), cwt.astype(BF16)
            x_lo, c_lo = (xbb - x_hi.astype(F32)).astype(BF16), (cwt - c_hi.astype(F32)).astype(BF16)
            strip = _dot_nt(x_hi, c_hi) + (_dot_nt(x_hi, c_lo) + _dot_nt(x_lo, c_hi))
            strip = strip + jnp.concatenate([xd_ref[p], jnp.zeros((S5_PAIR, width - LANES), F32)], axis=1)
            for s in range(t):
                blk = strip if s == 0 else jnp.where(strip_lane >= S5_PAIR * s, pltpu.roll(strip, S5_PAIR * s, 1), 0.0)
                m_scr[p, S5_PAIR * s:S5_PAIR * (s + 1), :] = blk.astype(BF16)

    lane = lax.broadcasted_iota(jnp.int32, (nc, LANES), 1)
    lo64 = lane < 64
    lo32 = (lane % 64) < 32

    v = []
    for tok in range(t):
        parts = [u_ref[b, pl.ds(tok, rows_b, stride=t), :] for b in range(nb)]
        v.append(parts[0] if nb == 1 else jnp.concatenate(parts, axis=0))
    quads = [_xpose4(v[4 * tq:4 * tq + 4], lo64, lo32) for tq in range(t // 4)]
    u_p = [jnp.concatenate([quads[tq][p] for tq in range(t // 4)], axis=1).astype(BF16) for p in range(S5_PAIRS)]

    if S5_PAIRS * ns == SUBLANES:
        for p in range(S5_PAIRS):
            s = _dot(u_p[p], wo_scr[p])
            for q in range(ns):
                rows_q = pl.ds(p * ns + q, cps, stride=SUBLANES)
                s_scr[0, rows_q, :] = s[q * cps:(q + 1) * cps, :LANES]
                s_scr[1, rows_q, :] = s[q * cps:(q + 1) * cps, LANES:]
        ar = jnp.concatenate([jnp.broadcast_to(pw_ref[p, 0, t:t + 1, :], (ns, LANES)) for p in range(S5_PAIRS)], axis=0)
        ai = jnp.concatenate([jnp.broadcast_to(pw_ref[p, 1, t:t + 1, :], (ns, LANES)) for p in range(S5_PAIRS)], axis=0)

        def step(k, carry):
            cr, ci = carry
            rows_k = pl.ds(pl.multiple_of(k * SUBLANES, SUBLANES), SUBLANES)
            hin_scr[0, rows_k, :] = cr
            hin_scr[1, rows_k, :] = ci
            return ar * cr - ai * ci + s_scr[0, rows_k, :], ar * ci + ai * cr + s_scr[1, rows_k, :]

        cr, ci = lax.fori_loop(0, cps, step, (jnp.concatenate([h_scr[2 * p] for p in range(S5_PAIRS)], axis=0),
                                              jnp.concatenate([h_scr[2 * p + 1] for p in range(S5_PAIRS)], axis=0)),
                               unroll=min(cps, S5_SCAN_UNROLL))
        carry = []
        for p in range(S5_PAIRS):
            carry += [cr[p * ns:(p + 1) * ns], ci[p * ns:(p + 1) * ns]]

        def chunk_states(p, plane):
            return jnp.concatenate([hin_scr[plane, pl.ds(p * ns + q, cps, stride=SUBLANES), :] for q in range(ns)],
                                   axis=0)
    else:
        for p in range(S5_PAIRS):
            s = _dot(u_p[p], wo_scr[p])
            s_scr[2 * p, 0:nc, :] = s[:, :LANES]
            s_scr[2 * p + 1, 0:nc, :] = s[:, LANES:]

        def step(k, carry):
            new = []
            for p in range(S5_PAIRS):
                cr, ci = carry[2 * p], carry[2 * p + 1]
                hin_scr[2 * p, pl.ds(k, ns, stride=cps), :] = cr
                hin_scr[2 * p + 1, pl.ds(k, ns, stride=cps), :] = ci
                ar = pw_ref[p, 0, t:t + 1, :]
                ai = pw_ref[p, 1, t:t + 1, :]
                tr = s_scr[2 * p, pl.ds(k, ns, stride=cps), :]
                ti = s_scr[2 * p + 1, pl.ds(k, ns, stride=cps), :]
                new += [ar * cr - ai * ci + tr, ar * ci + ai * cr + ti]
            return tuple(new)

        carry = lax.fori_loop(0, cps, step, tuple(h_scr[c] for c in range(ncol)), unroll=min(cps, S5_SCAN_UNROLL))

        def chunk_states(p, plane):
            return hin_scr[2 * p + plane, 0:nc, :]

    for c in range(ncol):
        h_scr[c] = carry[c]
        hf_ref[c] = carry[c]

    y_p = []
    for p in range(S5_PAIRS):
        h = jnp.concatenate([chunk_states(p, 0), chunk_states(p, 1)], axis=1).astype(BF16)
        y_p.append(_dot(u_p[p], m_scr[p]) + _dot_nt(h, wit_scr[p]))
    for tq in range(t // 4):
        w = _xpose4([y_p[p][:, LANES * tq:LANES * (tq + 1)] for p in range(S5_PAIRS)], lo64, lo32)
        for i in range(4):
            for b in range(nb):
                y_ref[b, pl.ds(4 * tq + i, rows_b, stride=t), :] = w[i][b * rows_b:(b + 1) * rows_b]


def _s5_call(u, tabs, h0, spb, cps, tokens):
    xb, xc, xd, pw = tabs
    nb, length, d = u.shape
    assert tokens == spb * cps * S5_T and length % tokens == 0
    ns = nb * spb
    ncol = 2 * S5_PAIRS
    nc_pad = -(-(nb * spb * cps) // SUBLANES) * SUBLANES
    scan_shape = (2, cps * SUBLANES, LANES) if S5_PAIRS * ns == SUBLANES else (ncol, nc_pad, LANES)
    width = S5_T * S5_PAIR
    blk = pl.BlockSpec((nb, tokens, LANES), lambda b, r: (0, r, b))
    state = pl.BlockSpec((ncol, ns, LANES), lambda b, r: (b, 0, 0))
    return pl.pallas_call(
        functools.partial(_s5_body, spb, cps),
        grid=(d // LANES, length // tokens),
        in_specs=[
            blk,
            pl.BlockSpec((S5_PAIRS, 2, S5_PAIR, LANES), lambda b, r: (b, 0, 0, 0)),
            pl.BlockSpec((S5_PAIRS, 2, S5_PAIR, LANES), lambda b, r: (b, 0, 0, 0)),
            pl.BlockSpec((S5_PAIRS, S5_PAIR, LANES), lambda b, r: (b, 0, 0)),
            pl.BlockSpec((S5_PAIRS, 2, S5_T + 1, LANES), lambda b, r: (b, 0, 0, 0)),
            state,
        ],
        out_specs=[blk, state],
        out_shape=[jax.ShapeDtypeStruct(u.shape, F32), jax.ShapeDtypeStruct(h0.shape, F32)],
        scratch_shapes=[pltpu.VMEM(scan_shape, F32), pltpu.VMEM(scan_shape, F32),
                        pltpu.VMEM((ncol, ns, LANES), F32), pltpu.VMEM((S5_PAIRS, width, width), BF16),
                        pltpu.VMEM((S5_PAIRS, width, 2 * LANES), BF16), pltpu.VMEM((S5_PAIRS, width, 2 * LANES), BF16)],
        compiler_params=_cparams("parallel", "arbitrary"),
        name="s5",
    )(u, xb, xc, xd, pw, h0)


def _s5_tables(a_re, a_im, log_dt, b_re, b_im, c_re, c_im, d_skip):
    t = S5_T
    g, p = a_re.shape
    c = b_re.shape[-1]
    dt = jnp.exp(log_dt)[:, None]
    mag = jnp.exp(a_re * dt)
    ang = a_im * dt
    ar = mag * jnp.cos(ang)
    ai = mag * jnp.sin(ang)
    num_re = ar - 1.0
    num_im = ai
    inv = 1.0 / (a_re * a_re + a_im * a_im)
    f_re = (num_re * a_re + num_im * a_im) * inv
    f_im = (num_im * a_re - num_re * a_im) * inv
    bb_re = f_re[..., None] * b_re - f_im[..., None] * b_im
    bb_im = f_re[..., None] * b_im + f_im[..., None] * b_re
    pr = jnp.ones((1, g, p), F32)
    pi = jnp.zeros((1, g, p), F32)
    sr, si = ar, ai
    while pr.shape[0] < t + 1:
        pr, pi = (jnp.concatenate([pr, pr * sr - pi * si]), jnp.concatenate([pi, pr * si + pi * sr]))
        sr, si = sr * sr - si * si, 2.0 * sr * si
    pr = pr[:t + 1]
    pi = pi[:t + 1]
    eye2 = jnp.eye(2, dtype=F32)

    def pad_pair(x):
        x = x.reshape(g // 2, 2, c, p)
        return (x[:, :, :, None, :] * eye2[None, :, None, :, None]).reshape(g // 2, 2 * c, 2 * p)

    xb = jnp.stack([pad_pair(jnp.transpose(bb_re, (0, 2, 1))), pad_pair(jnp.transpose(bb_im, (0, 2, 1)))], axis=1)
    xc = jnp.stack([pad_pair(c_re), pad_pair(c_im)], axis=1)
    xd = d_skip.reshape(g // 2, 2 * c, 1) * jnp.eye(2 * c, LANES, dtype=F32)[None]
    pw = jnp.stack([jnp.transpose(pr.reshape(t + 1, g // 2, 2 * p), (1, 0, 2)),
                    jnp.transpose(pi.reshape(t + 1, g // 2, 2 * p), (1, 0, 2))], axis=1)
    return xb, xc, xd, pw


def _state_to_cols(h_re, h_im):
    ns, g, p = h_re.shape
    f = lambda h: jnp.transpose(h.reshape(ns, g // 2, 2 * p), (1, 0, 2))
    return jnp.stack([f(h_re), f(h_im)], axis=1).reshape(g, ns, 2 * p)


def _cols_to_state(h):
    g, ns, w = h.shape
    h = jnp.transpose(h.reshape(g // 2, 2, ns, 2, w // 2), (1, 2, 0, 3, 4)).reshape(2, ns, g, w // 2)
    return h[0], h[1]


def _ffn_rows(x, g, win_ref, wout_ref):
    dff = wout_ref.shape[0]
    xn = _rms(x, g).astype(BF16)
    acc = x
    for j in range(dff // FFN_CHUNK):
        lo = j * FFN_CHUNK
        gate = _dot(xn, win_ref[:, lo:lo + FFN_CHUNK])
        up = _dot(xn, win_ref[:, dff + lo:dff + lo + FFN_CHUNK])
        a = (jax.nn.silu(gate) * up).astype(BF16)
        acc = acc + _dot(a, wout_ref[lo:lo + FFN_CHUNK, :])
    return acc


def _kv_rows(x, g_ref, w_ref, kn_ref, e_ref, et_ref, k_ref, v_ref):
    kvw = k_ref.shape[1]
    r = _dot(_rms(x, g_ref[...]).astype(BF16), w_ref[...])
    k_ref[...] = _head_rms(r[:, :kvw], e_ref[...], et_ref[...], kn_ref[...])
    v_ref[...] = r[:, kvw:]


def _ffn_body(x_ref, g_ref, win_ref, wout_ref, o_ref):
    o_ref[...] = _ffn_rows(x_ref[...], g_ref[...], win_ref, wout_ref)


def _glu_ffn_body(tail, x_ref, y_ref, wglu_ref, g_ref, win_ref, wout_ref, *rest):
    d = x_ref.shape[1]
    r = _dot(jax.nn.gelu(y_ref[...]).astype(BF16), wglu_ref[...])
    x = x_ref[...] + r[:, :d] * jax.nn.sigmoid(r[:, d:])
    out = _ffn_rows(x, g_ref[...], win_ref, wout_ref)
    if tail == "rms":
        gn_ref, o_ref, u_ref = rest
        u_ref[...] = _rms(out, gn_ref[...])
    else:
        gn_ref, wkv_ref, kn_ref, e_ref, et_ref, o_ref, k_ref, v_ref = rest
        _kv_rows(out, gn_ref, wkv_ref, kn_ref, e_ref, et_ref, k_ref, v_ref)
    o_ref[...] = out


def _layer(w, l):
    nd = w.ndim - 1
    return w, pl.BlockSpec((None,) + w.shape[1:], lambda *_: (l,) + (0,) * nd, pipeline_mode=pl.Buffered(1))


def _ffn_call(x, g, w_in, w_out):
    n, d = x.shape
    tm = _row_tile(n, ROW_TILE)
    row = pl.BlockSpec((tm, d), lambda i: (i, 0))
    return pl.pallas_call(
        _ffn_body,
        grid=(n // tm,),
        in_specs=[row, _resident((1, d)), w_in[1], w_out[1]],
        out_specs=row,
        out_shape=jax.ShapeDtypeStruct((n, d), F32),
        compiler_params=_cparams("parallel"),
        name="ffn",
    )(x, g.reshape(1, d), w_in[0], w_out[0])


def _glu_ffn_call(x, y, w_glu, g, w_in, w_out, next_norm, kv=None):
    n, d = x.shape
    tm = _row_tile(n, ROW_TILE)
    row = pl.BlockSpec((tm, d), lambda i: (i, 0))
    in_specs = [row, row, w_glu[1], _resident((1, d)), w_in[1], w_out[1], _resident((1, d))]
    args = [x, y, w_glu[0], g.reshape(1, d), w_in[0], w_out[0], next_norm.reshape(1, d)]
    if kv is None:
        tail, out_specs, out_shape = "rms", [row, row], [jax.ShapeDtypeStruct((n, d), F32)] * 2
    else:
        w_kv, k_norm = kv
        kvw = w_kv.shape[1] // 2
        e, et = _segment_mats(kvw // HEAD_DIM)
        kvrow = pl.BlockSpec((tm, kvw), lambda i: (i, 0))
        in_specs += [_resident(w_kv.shape), _resident((1, kvw)), _resident(e.shape), _resident(et.shape)]
        args += [w_kv, jnp.tile(k_norm, kvw // HEAD_DIM).reshape(1, kvw), e, et]
        tail, out_specs = "kv", [row, kvrow, kvrow]
        out_shape = [jax.ShapeDtypeStruct((n, d), F32)] + [jax.ShapeDtypeStruct((n, kvw), F32)] * 2
    return pl.pallas_call(
        functools.partial(_glu_ffn_body, tail),
        grid=(n // tm,),
        in_specs=in_specs,
        out_specs=out_specs,
        out_shape=out_shape,
        compiler_params=_cparams("parallel"),
        name="glu_ffn_" + tail,
    )(*args)


def _attn_body(chunk, shared, x_ref, g_ref, wq_ref, gain_ref, wo_ref, sink_ref, kown_ref, vown_ref, kleft_ref,
               vleft_ref, kmeta_ref, vmeta_ref, bband_ref, bmeta_ref, e_ref, o_ref,
               q_scr, rt_scr, kd_scr, vt_scr, lg_scr, att_scr):
    rows, d = x_ref.shape
    nchunks = rows // chunk
    n_heads = d // HEAD_DIM
    q_per_kv = n_heads // N_KV_HEADS
    band = WINDOW + chunk
    qlanes = q_per_kv * chunk
    kvw = N_KV_HEADS * HEAD_DIM

    x = x_ref[...]
    q = _dot(_rms(x, g_ref[...]).astype(BF16), wq_ref[...])
    q_scr[...] = q
    ss = _dot_nt(e_ref[...], (q * q).astype(BF16))
    rt_scr[...] = lax.rsqrt(ss * (1.0 / HEAD_DIM) + EPS)
    lane_row = lax.broadcasted_iota(jnp.int32, (1, LANES), 1)

    gain = gain_ref[...]
    low = lax.broadcasted_iota(jnp.int32, (1, LANES), 1) < HEAD_DIM

    def dup(kf):
        out = []
        for kv in range(N_KV_HEADS):
            t = kf[:, (kv // 2) * LANES:(kv // 2 + 1) * LANES]
            r = pltpu.roll(t, HEAD_DIM, 1)
            out.append((jnp.where(low, t, r) if kv % 2 == 0 else jnp.where(low, r, t)).astype(BF16))
        return out

    def put_keys(row0, kf, vf):
        for kv, t in enumerate(dup(kf * gain)):
            kd_scr[kv, row0:row0 + LANES, :] = t
        vt_scr[:, row0:row0 + LANES] = vf.T.astype(BF16)

    km = dup(kmeta_ref[...] * gain)
    vmt = vmeta_ref[...].astype(BF16)
    if shared:
        put_keys(0, kleft_ref[...], vleft_ref[...])
        for t in range(rows // LANES):
            put_keys(WINDOW + t * LANES, kown_ref[t * LANES:(t + 1) * LANES, :], vown_ref[t * LANES:(t + 1) * LANES, :])
        first_chunk = pl.program_id(1) * nchunks
    else:
        zpad = jnp.zeros((LANES - chunk, kvw), F32)
        for j in range(nchunks):
            left = lambda ref: jnp.concatenate([ref[j, :, kv, :] for kv in range(N_KV_HEADS)], axis=1)
            put_keys(j * 2 * LANES, left(kleft_ref), left(vleft_ref))
            put_keys(j * 2 * LANES + LANES, jnp.concatenate([kown_ref[j * chunk:(j + 1) * chunk, :], zpad], axis=0),
                     jnp.concatenate([vown_ref[j * chunk:(j + 1) * chunk, :], zpad], axis=0))

    for j in range(nchunks):
        r0 = j * chunk
        if shared:
            base = r0
            blocks = [(base + i * chunk, chunk, i * chunk, WINDOW // chunk - i) for i in range(band // chunk)]
        else:
            base = j * 2 * LANES
            blocks = [(base, WINDOW, 0, None), (base + WINDOW, chunk, WINDOW, None)]
        tile0 = (base // LANES) * LANES
        bm = bmeta_ref[j] if shared else bmeta_ref[0]
        for pair in range(N_KV_HEADS // 2):
            o_pair = []
            for kv in (2 * pair, 2 * pair + 1):
                qx = []
                for gq in range(q_per_kv):
                    h = kv * q_per_kv + gq
                    t = q_scr[r0:r0 + chunk, (h // 2) * LANES:(h // 2 + 1) * LANES]
                    qx.append(jnp.where(low if h % 2 == 0 else jnp.logical_not(low), t, 0.0))
                qx = jnp.concatenate(qx, axis=0).astype(BF16)
                s = sink_ref[kv]
                src = (r0 // LANES) * LANES
                tiles = [jnp.zeros((1, LANES), F32) for _ in range(qlanes // LANES)]
                for gq in range(q_per_kv):
                    h = kv * q_per_kv + gq
                    dst = gq * chunk
                    row = pltpu.roll(rt_scr[h:h + 1, src:src + LANES], (dst - r0) % LANES, 1)
                    pick = (lane_row >= dst % LANES) & (lane_row < dst % LANES + chunk)
                    tiles[dst // LANES] = jnp.where(pick, row, tiles[dst // LANES])
                rrow = tiles[0] if len(tiles) == 1 else jnp.concatenate(tiles, axis=1)
                raw = _dot_nt(jnp.concatenate([kd_scr[kv, base:base + band, :], km[kv]], axis=0), qx)
                run = jnp.full((SUBLANES, qlanes), NEG, F32)
                for (k0, nk, b0, back) in blocks:
                    lg = raw[b0:b0 + nk] * rrow + bband_ref[kv, b0:b0 + nk, :]
                    if back and j < back:
                        lg = jnp.where(first_chunk + j >= back, lg, NEG)
                    lg_scr[kv, b0:b0 + nk, :] = lg
                    run = jnp.maximum(run, jnp.max(lg.reshape(nk // SUBLANES, SUBLANES, qlanes), axis=0))
                lgm = raw[band:band + N_META] * rrow + bm[kv]
                run = jnp.maximum(run, jnp.max(lgm.reshape(N_META // SUBLANES, SUBLANES, qlanes), axis=0))
                mx = jnp.maximum(jnp.max(run, axis=0, keepdims=True), s)
                pm = jnp.exp(lgm - mx)
                den = jnp.sum(pm, axis=0, keepdims=True) + jnp.exp(s - mx)
                ot = _dot(vmt[kv * HEAD_DIM:(kv + 1) * HEAD_DIM, :], pm.astype(BF16))
                parts = {}
                for (k0, nk, b0, back) in blocks:
                    p = jnp.exp(lg_scr[kv, b0:b0 + nk, :] - mx)
                    den = den + jnp.sum(p, axis=0, keepdims=True)
                    parts[k0] = p.astype(BF16)
                for t0 in (tile0, tile0 + LANES):
                    segs, pos = [], t0
                    while pos < t0 + LANES:
                        if pos in parts:
                            segs.append(parts[pos])
                            pos += parts[pos].shape[0]
                        else:
                            nxt = min([k0 for k0 in parts if k0 > pos] + [t0 + LANES])
                            segs.append(jnp.zeros((nxt - pos, qlanes), BF16))
                            pos = nxt
                    pt = segs[0] if len(segs) == 1 else jnp.concatenate(segs, axis=0)
                    ot = ot + _dot(vt_scr[kv * HEAD_DIM:(kv + 1) * HEAD_DIM, t0:t0 + LANES], pt)
                o_pair.append(ot / den)
            o2 = jnp.concatenate(o_pair, axis=0).T
            for gq in range(q_per_kv):
                tile = pair * q_per_kv + gq
                att_scr[r0:r0 + chunk, tile * LANES:(tile + 1) * LANES] = o2[gq * chunk:(gq + 1) * chunk, :]

    o_ref[...] = x + _dot(att_scr[...].astype(BF16), wo_ref[...])


def _attn_call(x, g, w_q, q_norm, w_o, sinks, k_own, v_own, k_left, v_left, k_meta, v_meta, b_band, b_meta,
               chunk, shared, nseq, tiles_per_seq, chunks_per_tile):
    n, d = x.shape
    kvw = k_own.shape[1]
    n_heads = d // HEAD_DIM
    q_per_kv = n_heads // N_KV_HEADS
    rows = chunks_per_tile * chunk
    band = WINDOW + chunk
    qlanes = q_per_kv * chunk
    assert rows % LANES == 0 and LANES % chunk == 0
    key_rows = WINDOW + rows if shared else chunks_per_tile * 2 * LANES
    _, e = _segment_mats(n_heads)
    gain = jnp.tile(q_norm * (HEAD_DIM ** -0.5), N_KV_HEADS).reshape(1, kvw)
    sink_row = jnp.broadcast_to(sinks.reshape(N_KV_HEADS, 1, q_per_kv, 1),
                                (N_KV_HEADS, 1, q_per_kv, chunk)).reshape(N_KV_HEADS, 1, qlanes)
    w_o = jnp.concatenate([w_o[((2 * pair + kv2) * q_per_kv + gq) * HEAD_DIM:((2 * pair + kv2) * q_per_kv + gq + 1) * HEAD_DIM]
                           for pair in range(N_KV_HEADS // 2) for gq in range(q_per_kv) for kv2 in range(2)], axis=0)
    row_idx = lambda b, i: (b * tiles_per_seq + i, 0)
    if shared:
        halo_blocks = rows // WINDOW
        left_spec = pl.BlockSpec(
            (WINDOW, kvw), lambda b, i: (jnp.maximum(b * tiles_per_seq * halo_blocks + i * halo_blocks - 1, 0), 0))
        last_tile = b_meta.shape[0] // chunks_per_tile - 1
        bmeta_spec = pl.BlockSpec((chunks_per_tile,) + b_meta.shape[1:],
                                  lambda b, i: (jnp.minimum(i, last_tile), 0, 0, 0))
    else:
        left_spec = pl.BlockSpec((chunks_per_tile, WINDOW, N_KV_HEADS, HEAD_DIM),
                                 lambda b, i: (b * tiles_per_seq + i, 0, 0, 0))
        bmeta_spec = _resident(b_meta.shape)
    return pl.pallas_call(
        functools.partial(_attn_body, chunk, shared),
        grid=(nseq, tiles_per_seq),
        in_specs=[
            pl.BlockSpec((rows, d), row_idx), _resident((1, d)), _resident(w_q.shape), _resident((1, kvw)),
            _resident(w_o.shape), _resident(sink_row.shape),
            pl.BlockSpec((rows, kvw), row_idx), pl.BlockSpec((rows, kvw), row_idx), left_spec, left_spec,
            _resident(k_meta.shape), _resident((kvw, N_META)), _resident(b_band.shape), bmeta_spec,
            _resident(e.shape),
        ],
        out_specs=pl.BlockSpec((rows, d), row_idx),
        out_shape=jax.ShapeDtypeStruct((n, d), F32),
        scratch_shapes=[pltpu.VMEM((rows, d), F32), pltpu.VMEM((LANES, rows), F32),
                        pltpu.VMEM((N_KV_HEADS, key_rows, LANES), BF16),
                        pltpu.VMEM((kvw, key_rows), BF16), pltpu.VMEM((N_KV_HEADS, band, qlanes), F32),
                        pltpu.VMEM((rows, d), F32)],
        compiler_params=_cparams("parallel", "arbitrary"),
        name="attn_prompt" if shared else "attn_sample",
    )(x, g.reshape(1, d), w_q, gain, w_o, sink_row, k_own, v_own, k_left, v_left, k_meta, v_meta.T, b_band, b_meta,
      e)


def _rel_bucket(rel):
    half = N_BUCKETS // 2
    max_exact = half // 2
    n = jnp.abs(rel)
    nf = jnp.maximum(n, 1).astype(F32)
    large = max_exact + (jnp.log(nf / max_exact) / math.log(MAX_DISTANCE / max_exact)
                         * (half - max_exact)).astype(jnp.int32)
    large = jnp.minimum(large, half - 1)
    return jnp.where(rel > 0, half, 0) + jnp.where(n < max_exact, n, large)


def _bias_lookup(rel_bias, bucket):
    onehot = (bucket[..., None] == jnp.arange(N_BUCKETS)).astype(F32)
    return jnp.dot(onehot, rel_bias, precision=lax.Precision.HIGHEST)


def _band_bias(rel_bias, chunk):
    n_heads = rel_bias.shape[1]
    q_per_kv = n_heads // N_KV_HEADS
    band = WINDOW + chunk
    rel = jnp.arange(band)[None, :] - WINDOW - jnp.arange(chunk)[:, None]
    b = _bias_lookup(rel_bias, _rel_bucket(rel)).reshape(chunk, band, N_KV_HEADS, q_per_kv)
    return jnp.transpose(b, (2, 1, 3, 0)).reshape(N_KV_HEADS, band, q_per_kv * chunk)


def _meta_bias(t0, count, rel_bias, chunk):
    n_heads = rel_bias.shape[1]
    q_per_kv = n_heads // N_KV_HEADS
    dist = jnp.arange(t0 + 1, t0 + count + N_META + 1)
    tab = _bias_lookup(rel_bias, _rel_bucket(-dist)).T
    b = jnp.stack([tab[:, N_META - 1 - m:N_META - 1 - m + count] for m in range(N_META)])
    b = b.reshape(N_META, N_KV_HEADS, q_per_kv, count // chunk, chunk)
    return jnp.transpose(b, (3, 1, 0, 2, 4)).reshape(count // chunk, N_KV_HEADS, N_META, q_per_kv * chunk)


def _meta_bias_tiles(seq, rel_bias, chunk, tile_rows):
    tiles = min(seq // tile_rows, 2)
    assert tiles == seq // tile_rows or (tiles - 1) * tile_rows + 1 >= MAX_DISTANCE
    return _meta_bias(0, tiles * tile_rows, rel_bias, chunk)


def kernel(x_prompt, x_sample, state_ssm_re, state_ssm_im, cache_k, cache_v, meta_tokens, norm_mix, norm_ffn,
           ssm_a_re, ssm_a_im, ssm_log_dt, ssm_b_re, ssm_b_im, ssm_c_re, ssm_c_im, ssm_d, w_glu, w_ffn_in,
           w_ffn_out, norm_kv, w_kv, k_norm, w_q, q_norm, attn_sinks, w_o, rel_bias):
    bsz, seq, d = x_prompt.shape
    dbsz, dseq, _ = x_sample.shape
    n_a = ssm_a_re.shape[0]
    n_b = w_q.shape[0]
    n_groups, n_state = ssm_a_re.shape[1:]
    kvw = N_KV_HEADS * HEAD_DIM

    cast = lambda w: [w[i].astype(BF16) for i in range(w.shape[0])]
    w_q_b, w_o_b = cast(w_q), cast(w_o)
    w_glu_s, w_in_s, w_out_s = w_glu.astype(BF16), w_ffn_in.astype(BF16), w_ffn_out.astype(BF16)
    w_kv_b = w_kv.astype(BF16)
    tabs = [_s5_tables(ssm_a_re[l], ssm_a_im[l], ssm_log_dt[l], ssm_b_re[l], ssm_b_im[l], ssm_c_re[l],
                       ssm_c_im[l], ssm_d[l]) for l in range(n_a)]

    def s5_layers(x, view, h0_of_layer, spb, cps, tokens):
        states = []
        u = _rms_call(x, norm_mix[0])
        for l in range(n_a):
            y, hf = _s5_call(u.reshape(view), tabs[l], h0_of_layer(l), spb=spb, cps=cps, tokens=tokens)
            states.append(_cols_to_state(hf))
            args = (x, y.reshape(x.shape), _layer(w_glu_s, l), norm_ffn[l], _layer(w_in_s, l), _layer(w_out_s, l))
            if l + 1 < n_a:
                x, u = _glu_ffn_call(*args, norm_mix[l + 1])
            else:
                x, k, v = _glu_ffn_call(*args, norm_kv, kv=(w_kv_b, k_norm))
        return x, k, v, states

    n_s = dbsz * dseq
    n_small = n_s + dseq
    xs = jnp.concatenate([x_sample.reshape(n_s, d), jnp.zeros((dseq - N_META, d), F32), meta_tokens], axis=0)
    zero_state = jnp.zeros((1, n_groups, n_state), F32)
    h0_small = lambda l: _state_to_cols(jnp.concatenate([state_ssm_re[l], zero_state], axis=0),
                                        jnp.concatenate([state_ssm_im[l], zero_state], axis=0))
    xs, k_small, v_small, st_small = s5_layers(xs, (1, n_small, d), h0_small, dbsz + 1, dseq // S5_T, n_small)
    s_re = [hr[:dbsz] for hr, _ in st_small]
    s_im = [hi[:dbsz] for _, hi in st_small]
    k_s, v_s = k_small[:n_s], v_small[:n_s]
    k_meta, v_meta = k_small[n_small - N_META:], v_small[n_small - N_META:]

    bband_s = _band_bias(rel_bias, dseq)
    bmeta_s = _meta_bias(PAST_LEN, dseq, rel_bias, dseq)
    xq = xs[:n_s]
    for j in range(n_b):
        l = n_a + j
        xq = _attn_call(xq, norm_mix[l], w_q_b[j], q_norm[j], w_o_b[j], attn_sinks[j], k_s, v_s, cache_k, cache_v, k_meta,
                        v_meta, bband_s, bmeta_s, chunk=dseq, shared=False, nseq=1,
                        tiles_per_seq=dbsz // ATTN_SEQS_SAMPLE, chunks_per_tile=ATTN_SEQS_SAMPLE)
        xq = _ffn_call(xq, norm_ffn[l], _layer(w_in_s, l), _layer(w_out_s, l))
    y_sample = xq.reshape(dbsz, dseq, d)

    s5_tokens = min(S5_TOKENS, seq)
    h0_prompt = lambda l: _state_to_cols(jnp.broadcast_to(st_small[l][0][dbsz:], (bsz, n_groups, n_state)),
                                         jnp.broadcast_to(st_small[l][1][dbsz:], (bsz, n_groups, n_state)))
    xp, k_p, v_p, st_prompt = s5_layers(x_prompt.reshape(bsz * seq, d), (bsz, seq, d), h0_prompt, 1,
                                        s5_tokens // S5_T, s5_tokens)
    p_re = [hr for hr, _ in st_prompt]
    p_im = [hi for _, hi in st_prompt]

    bband_p = _band_bias(rel_bias, CHUNK)
    tile_rows = ATTN_CHUNKS_PROMPT * CHUNK
    bmeta_p = _meta_bias_tiles(seq, rel_bias, CHUNK, tile_rows)
    for j in range(n_b):
        l = n_a + j
        xp = _attn_call(xp, norm_mix[l], w_q_b[j], q_norm[j], w_o_b[j], attn_sinks[j], k_p, v_p, k_p, v_p, k_meta,
                        v_meta, bband_p, bmeta_p, chunk=CHUNK, shared=True, nseq=bsz,
                        tiles_per_seq=seq // tile_rows, chunks_per_tile=ATTN_CHUNKS_PROMPT)
        xp = _ffn_call(xp, norm_ffn[l], _layer(w_in_s, l), _layer(w_out_s, l))
    y_prompt = xp.reshape(bsz, seq, d)

    tail = lambda a: a.reshape(bsz, seq, kvw)[:, -WINDOW:].reshape(bsz, WINDOW, N_KV_HEADS, HEAD_DIM)
    return (y_prompt, y_sample, jnp.stack(p_re), jnp.stack(p_im), tail(k_p), tail(v_p),
            jnp.stack(s_re), jnp.stack(s_im), k_s.reshape(dbsz, dseq, N_KV_HEADS, HEAD_DIM),
            v_s.reshape(dbsz, dseq, N_KV_HEADS, HEAD_DIM))
```
